```python
import numpy as np
import jax, jax.numpy as jnp
from jax import lax

D_MODEL = 1024
BATCH = 2
SEQ = 8192
DEPTH = 2

GRID_W = 64
CTX_LEN = 256
EPS = 1e-6

NA_HEADS = 8
NA_HEAD_DIM = 64
NA_WIDTH = NA_HEADS * NA_HEAD_DIM
NA_KH = 8
NA_KW = 16

HG_HEADS = 4
HG_KEY_DIM = 128
HG_VAL_DIM = 128
HG_KEY = HG_HEADS * HG_KEY_DIM
HG_VAL = HG_HEADS * HG_VAL_DIM
HG_CHUNK = 32

D_FF = 2816
CONV_W = 3

N_EVEN = (DEPTH + 1) // 2
N_ODD = DEPTH // 2
EV_SIZES = [NA_WIDTH, NA_WIDTH, NA_WIDTH, HG_KEY, HG_KEY, HG_KEY, HG_VAL, HG_VAL]
EV_IN = sum(EV_SIZES)
EV_SPLITS = [int(s) for s in np.cumsum(EV_SIZES)[:-1]]
EV_MIX = NA_WIDTH + HG_VAL

kernel_name = "hybrid_na_hgrn2_shortconv_dit_block"


def rms_norm(x, w):
    xf = x.astype(jnp.float32)
    y = xf * lax.rsqrt(jnp.mean(xf * xf, axis=-1, keepdims=True) + EPS)
    return (y * w.astype(jnp.float32)).astype(x.dtype)


def modulate(h, shift, scale):
    return h * (1.0 + scale[:, None, :]) + shift[:, None, :]


def dwconv3(h, w, b):
    hp = jnp.pad(h, ((0, 0), (1, 1), (0, 0)))
    return hp[:, :-2] * w[0] + hp[:, 1:-1] * w[1] + hp[:, 2:] * w[2] + b


def neighbourhood_attention(q, k, v, k_ctx, v_ctx, rpb):
    b, s, h, d = q.shape
    rows = s // GRID_W
    kh = min(NA_KH, rows)
    kw = NA_KW
    scale = d ** -0.5
    cols = np.arange(GRID_W)
    col_start = np.clip(cols - kw // 2, 0, GRID_W - kw)
    col_idx = col_start[:, None] + np.arange(kw)[None, :]
    col_rel = col_idx - cols[:, None] + (NA_KW - 1)
    rpb_c = rpb.astype(jnp.float32)[:, :, col_rel]
    qg = q.reshape(b, rows, GRID_W, h, d)
    kg = k.reshape(b, rows, GRID_W, h, d)
    vg = v.reshape(b, rows, GRID_W, h, d)

    def row_block(args):
        r, q_row = args
        r0 = jnp.clip(r - kh // 2, 0, rows - kh)
        k_rows = lax.dynamic_slice_in_dim(kg, r0, kh, axis=1)
        v_rows = lax.dynamic_slice_in_dim(vg, r0, kh, axis=1)
        k_nb = k_rows[:, :, col_idx]
        v_nb = v_rows[:, :, col_idx]
        row_rel = r0 + jnp.arange(kh) - r + (NA_KH - 1)
        bias = rpb_c[:, row_rel].transpose(0, 2, 1, 3)
        s_loc = jnp.einsum('bqhd,brqwhd->bhqrw', q_row, k_nb).astype(jnp.float32) * scale + bias[None]
        s_ctx = jnp.einsum('bqhd,bkhd->bhqk', q_row, k_ctx).astype(jnp.float32) * scale
        logits = jnp.concatenate([s_loc.reshape(b, h, GRID_W, kh * kw), s_ctx], axis=-1)
        p = jax.nn.softmax(logits, axis=-1).astype(v.dtype)
        p_loc = p[..., :kh * kw].reshape(b, h, GRID_W, kh, kw)
        p_ctx = p[..., kh * kw:]
        return (jnp.einsum('bhqrw,brqwhd->bqhd', p_loc, v_nb)
                + jnp.einsum('bhqk,bkhd->bqhd', p_ctx, v_ctx))

    out = lax.map(row_block, (jnp.arange(rows), qg.transpose(1, 0, 2, 3, 4)))
    return out.transpose(1, 0, 2, 3, 4).reshape(b, s, h, d)


def context_attention(q, k, v):
    s = jnp.einsum('bqhd,bkhd->bhqk', q, k).astype(jnp.float32) * (q.shape[-1] ** -0.5)
    p = jax.nn.softmax(s, axis=-1).astype(v.dtype)
    return jnp.einsum('bhqk,bkhd->bqhd', p, v)


def hgrn2_scan(q, k, v, log_f, s0):
    b, h, l, _ = q.shape
    n = l // HG_CHUNK

    def to_chunks(t):
        return t.reshape(b, h, n, HG_CHUNK, t.shape[-1]).transpose(2, 0, 1, 3, 4)

    causal_in_chunk = jnp.tril(jnp.ones((HG_CHUNK, HG_CHUNK), dtype=bool))

    def step(state, inp):
        qc, kc, vc, gc = inp
        bcum = jnp.cumsum(gc, axis=2)
        btot = bcum[:, :, -1:]
        qe = qc * jnp.exp(bcum)
        ke = kc * jnp.exp(-bcum)
        att = jnp.where(causal_in_chunk, jnp.einsum('bhtk,bhsk->bhts', qe, ke), 0.0)
        o = jnp.einsum('bhtk,bhkv->bhtv', qe, state) + jnp.einsum('bhts,bhsv->bhtv', att, vc)
        kd = kc * jnp.exp(btot - bcum)
        new_state = jnp.exp(btot)[:, :, 0, :, None] * state + jnp.einsum('bhsk,bhsv->bhkv', kd, vc)
        return new_state, o

    s_end, o = lax.scan(step, s0, (to_chunks(q), to_chunks(k), to_chunks(v), to_chunks(log_f)))
    return o.transpose(1, 2, 0, 3, 4).reshape(b, h, l, v.shape[-1]), s_end


def hgrn2_final_state(k, v, log_f):
    bcum = jnp.cumsum(log_f, axis=2)
    return jnp.einsum('bhtk,bhtv->bhkv', k * jnp.exp(bcum[:, :, -1:] - bcum), v)


def _heads(a, dh):
    b, l, _ = a.shape
    return a.astype(jnp.float32).reshape(b, l, -1, dh).transpose(0, 2, 1, 3)


def _hg_gates(f_pres, i, lb):
    v = _heads(i, HG_VAL_DIM)
    dirs = []
    for d, f_pre in enumerate(f_pres):
        lbd = lb[d].astype(jnp.float32).reshape(HG_HEADS, 1, HG_KEY_DIM)
        fg = lbd + (1.0 - lbd) * jax.nn.sigmoid(_heads(f_pre, HG_KEY_DIM))
        dirs.append((1.0 - fg, jnp.log(fg)))
    return v, dirs


def _hg_readout(o, g, norm_w):
    b, _, l, _ = o.shape
    o = o * lax.rsqrt(jnp.mean(o * o, axis=-1, keepdims=True) + EPS)
    o = o * norm_w.astype(jnp.float32).reshape(HG_HEADS, 1, HG_VAL_DIM)
    o = o.transpose(0, 2, 1, 3).reshape(b, l, HG_VAL)
    return (o * jax.nn.silu(g.astype(jnp.float32))).astype(g.dtype)


def hgrn2_mixer(q, fs, i, g, cq, cfs, ci, cg, lb, norm_w, with_ctx_out):
    v, dirs = _hg_gates(fs, i, lb)
    vc, dirs_c = _hg_gates(cfs, ci, lb)
    qh = jax.nn.silu(_heads(q, HG_KEY_DIM))
    if with_ctx_out:
        cqh = jax.nn.silu(_heads(cq, HG_KEY_DIM))
    b = q.shape[0]
    zero_state = jnp.zeros((b, HG_HEADS, HG_KEY_DIM, HG_VAL_DIM), jnp.float32)
    o_lat = jnp.zeros_like(qh[..., :HG_VAL_DIM])
    o_ctx = None
    for d in range(2):
        tr = (lambda a: jnp.flip(a, axis=2)) if d == 1 else (lambda a: a)
        k, lf = dirs[d]
        kc, lfc = dirs_c[d]
        if with_ctx_out:
            oc, s_ctx = hgrn2_scan(tr(cqh), tr(kc), tr(vc), tr(lfc), zero_state)
            o_ctx = tr(oc) if o_ctx is None else o_ctx + tr(oc)
        else:
            s_ctx = hgrn2_final_state(tr(kc), tr(vc), tr(lfc))
        ol, _ = hgrn2_scan(tr(qh), tr(k), tr(v), tr(lf), s_ctx)
        o_lat = o_lat + tr(ol)
    y = _hg_readout(o_lat, g, norm_w)
    y_ctx = _hg_readout(o_ctx, cg, norm_w) if with_ctx_out else None
    return y, y_ctx


def even_mixer(h, h_ctx, w_in, w_out, rpb, lb, hg_norm_w, with_ctx_out):
    b, s, _ = h.shape
    na_q, na_k, na_v, hg_q, hg_ff, hg_fb, hg_i, hg_g = jnp.split(h @ w_in, EV_SPLITS, axis=-1)
    cna_q, cna_k, cna_v, chg_q, chg_ff, chg_fb, chg_i, chg_g = jnp.split(h_ctx @ w_in, EV_SPLITS, axis=-1)

    def nh(a):
        return a.reshape(a.shape[0], a.shape[1], NA_HEADS, NA_HEAD_DIM)

    a_lat = neighbourhood_attention(nh(na_q), nh(na_k), nh(na_v), nh(cna_k), nh(cna_v), rpb)
    g_lat, g_ctx = hgrn2_mixer(hg_q, (hg_ff, hg_fb), hg_i, hg_g,
                               chg_q, (chg_ff, chg_fb), chg_i, chg_g, lb, hg_norm_w, with_ctx_out)
    y = jnp.concatenate([a_lat.reshape(b, s, NA_WIDTH), g_lat], axis=-1) @ w_out
    y_ctx = None
    if with_ctx_out:
        a_ctx = context_attention(nh(cna_q), nh(cna_k), nh(cna_v))
        y_ctx = jnp.concatenate([a_ctx.reshape(b, h_ctx.shape[1], NA_WIDTH), g_ctx], axis=-1) @ w_out
    return y, y_ctx


def short_conv_mixer(h, w_in, conv_w, conv_b, w_out):
    gate_b, gate_c, u = jnp.split(h @ w_in, 3, axis=-1)
    return (gate_b * dwconv3(gate_c * u, conv_w, conv_b)) @ w_out


def conv_ffn(h, w_up, conv_w, conv_b, w_down):
    a, val = jnp.split(h @ w_up, 2, axis=-1)
    return (jax.nn.gelu(dwconv3(a, conv_w, conv_b), approximate=False) * val) @ w_down


def setup_inputs(seed: int = 0) -> dict:
    key = jax.random.key(seed)
    ks = jax.random.split(key, 24)
    f32 = jnp.float32
    nrm = lambda k, shape, s: jax.random.normal(k, shape, f32) * s
    D = D_MODEL
    return {
        "x": nrm(ks[0], (BATCH, SEQ, D), 1.0),
        "c": nrm(ks[1], (BATCH, D), 1.0),
        "ctx": nrm(ks[2], (BATCH, CTX_LEN, D), 1.0),
        "c_ctx": nrm(ks[3], (D,), 1.0),
        "ada_w": nrm(ks[4], (DEPTH, D, 6 * D), 0.5 * D ** -0.5),
        "ada_b": nrm(ks[5], (DEPTH, 6 * D), 0.02),
        "norm_mix_w": 1.0 + nrm(ks[6], (DEPTH, D), 0.02),
        "norm_ffn_w": 1.0 + nrm(ks[7], (DEPTH, D), 0.02),
        "ev_w_in": nrm(ks[8], (N_EVEN, D, EV_IN), D ** -0.5),
        "ev_w_out": nrm(ks[9], (N_EVEN, EV_MIX, D), EV_MIX ** -0.5),
        "na_rpb": nrm(ks[10], (N_EVEN, NA_HEADS, 2 * NA_KH - 1, 2 * NA_KW - 1), 0.1),
        "hg_lb_logits": nrm(ks[11], (N_EVEN + 1, 2, HG_KEY), 0.1),
        "hg_norm_w": 1.0 + nrm(ks[12], (N_EVEN, HG_VAL), 0.02),
        "od_w_in": nrm(ks[13], (N_ODD, D, 3 * D), D ** -0.5),
        "od_conv_w": nrm(ks[14], (N_ODD, CONV_W, D), CONV_W ** -0.5),
        "od_conv_b": nrm(ks[15], (N_ODD, D), 0.02),
        "od_w_out": nrm(ks[16], (N_ODD, D, D), D ** -0.5),
        "ffn_w_up": nrm(ks[17], (DEPTH, D, 2 * D_FF), D ** -0.5),
        "ffn_conv_w": nrm(ks[18], (DEPTH, CONV_W, D_FF), CONV_W ** -0.5),
        "ffn_conv_b": nrm(ks[19], (DEPTH, D_FF), 0.02),
        "ffn_w_down": nrm(ks[20], (DEPTH, D_FF, D), D_FF ** -0.5),
        "final_norm_w": 1.0 + nrm(ks[21], (D,), 0.02),
    }


def reference(x, c, ctx, c_ctx, ada_w, ada_b, norm_mix_w, norm_ffn_w, ev_w_in, ev_w_out,
              na_rpb, hg_lb_logits, hg_norm_w, od_w_in, od_conv_w, od_conv_b, od_w_out,
              ffn_w_up, ffn_conv_w, ffn_conv_b, ffn_w_down, final_norm_w):
    lb_all = jnp.cumsum(jax.nn.softmax(hg_lb_logits.astype(jnp.float32), axis=0), axis=0)[:N_EVEN]
    silu_c = jax.nn.silu(c)
    silu_cc = jax.nn.silu(c_ctx)[None]
    xc = ctx
    for layer in range(DEPTH):
        j = layer // 2
        is_even = layer % 2 == 0
        ctx_feeds_later = any(l % 2 == 0 for l in range(layer + 1, DEPTH))
        mod = silu_c @ ada_w[layer] + ada_b[layer]
        shift_m, scale_m, gate_m, shift_f, scale_f, gate_f = jnp.split(mod, 6, axis=-1)
        if is_even or ctx_feeds_later:
            mod_c = silu_cc @ ada_w[layer] + ada_b[layer]
            cshift_m, cscale_m, cgate_m, cshift_f, cscale_f, cgate_f = jnp.split(mod_c, 6, axis=-1)
            h_ctx = modulate(rms_norm(xc, norm_mix_w[layer]), cshift_m, cscale_m)
        h = modulate(rms_norm(x, norm_mix_w[layer]), shift_m, scale_m)
        if is_even:
            y, y_ctx = even_mixer(h, h_ctx, ev_w_in[j], ev_w_out[j], na_rpb[j], lb_all[j],
                                  hg_norm_w[j], ctx_feeds_later)
        else:
            y = short_conv_mixer(h, od_w_in[j], od_conv_w[j], od_conv_b[j], od_w_out[j])
            y_ctx = (short_conv_mixer(h_ctx, od_w_in[j], od_conv_w[j], od_conv_b[j], od_w_out[j])
                     if ctx_feeds_later else None)
        x = x + gate_m[:, None, :] * y
        h = modulate(rms_norm(x, norm_ffn_w[layer]), shift_f, scale_f)
        x = x + gate_f[:, None, :] * conv_ffn(h, ffn_w_up[layer], ffn_conv_w[layer],
                                              ffn_conv_b[layer], ffn_w_down[layer])
        if ctx_feeds_later:
            xc = xc + cgate_m[:, None, :] * y_ctx
            hc = modulate(rms_norm(xc, norm_ffn_w[layer]), cshift_f, cscale_f)
            xc = xc + cgate_f[:, None, :] * conv_ffn(hc, ffn_w_up[layer], ffn_conv_w[layer],
                                                     ffn_conv_b[layer], ffn_w_down[layer])
    return rms_norm(x, final_norm_w)
```

```python
import functools

import numpy as np
import jax
import jax.numpy as jnp
from jax import lax
from jax.experimental import pallas as pl
from jax.experimental.pallas import tpu as pltpu

F32 = jnp.float32
BF16 = jnp.bfloat16
HIGHEST = lax.Precision.HIGHEST

D_MODEL = 1024
GRID_W = 64
EPS = 1e-6
NA_HEADS = 8
NA_HEAD_DIM = 64
NA_WIDTH = NA_HEADS * NA_HEAD_DIM
NA_KH = 8
NA_KW = 16
HG_HEADS = 4
HG_DIM = 128
HG_WIDTH = HG_HEADS * HG_DIM
D_FF = 2816
EV_IN = 4096

V7X_VMEM_BYTES = 64 * 1024 * 1024
VMEM_LIMIT = V7X_VMEM_BYTES - 8 * 1024 * 1024
LANES = 128
HALO = 16
NEG = -1e30

NA_QROWS = 4
NA_KROWS = NA_QROWS + NA_KH
HG_CHUNK = 64
HG_UNROLL = 2
FF_CHUNK = 256
TM = 512


def _cparams(sem):
    return pltpu.CompilerParams(dimension_semantics=sem, vmem_limit_bytes=VMEM_LIMIT)


def _rms_mod(x, nw, shift, scale):
    ms = jnp.mean(x * x, axis=-1, keepdims=True)
    y = x * lax.rsqrt(ms + EPS) * nw
    return y * (1.0 + scale) + shift


def _ada_kernel(c_ref, w_ref, b_ref, o_ref):
    s = jax.nn.silu(c_ref[...])
    o_ref[0] = jnp.dot(s, w_ref[0], precision=HIGHEST, preferred_element_type=F32) + b_ref[0]


def _ada(c_stack, ada_w, ada_b):
    depth, d, n6 = ada_w.shape
    tn = 1536
    return pl.pallas_call(
        _ada_kernel,
        grid=(depth, n6 // tn),
        in_specs=[
            pl.BlockSpec((8, d), lambda l, j: (0, 0)),
            pl.BlockSpec((1, d, tn), lambda l, j: (l, 0, j)),
            pl.BlockSpec((1, 1, tn), lambda l, j: (l, 0, j)),
        ],
        out_specs=pl.BlockSpec((1, 8, tn), lambda l, j: (l, 0, j)),
        out_shape=jax.ShapeDtypeStruct((depth, 8, n6), F32),
        compiler_params=_cparams(("arbitrary", "arbitrary")),
        name="ada",
    )(c_stack, ada_w, ada_b.reshape(depth, 1, n6))


def _proj_kernel(x_ref, nw_ref, shift_ref, scale_ref, w_ref, o_ref, h_s, *, tpb, fixed_row, ncol):
    if fixed_row is None:
        row = pl.program_id(0) // tpb
    else:
        row = fixed_row
    h = _rms_mod(x_ref[...], nw_ref[...], shift_ref[pl.ds(row, 1), :], scale_ref[pl.ds(row, 1), :])
    h_s[...] = h.astype(BF16)
    n = w_ref.shape[1]
    for c0 in range(0, n, ncol):
        o_ref[:, c0:c0 + ncol] = jnp.dot(h_s[...], w_ref[:, c0:c0 + ncol],
                                         preferred_element_type=F32).astype(o_ref.dtype)


def _proj(x2d, nw, mod, shift_blk, scale_blk, w, *, tm, tpb, fixed_row):
    n_rows, d = x2d.shape
    n = w.shape[1]
    kern = functools.partial(_proj_kernel, tpb=tpb, fixed_row=fixed_row, ncol=512)
    return pl.pallas_call(
        kern,
        grid=(n_rows // tm,),
        in_specs=[
            pl.BlockSpec((tm, d), lambda i: (i, 0)),
            pl.BlockSpec((1, d), lambda i: (0, 0)),
            pl.BlockSpec((8, d), lambda i: (0, shift_blk)),
            pl.BlockSpec((8, d), lambda i: (0, scale_blk)),
            pl.BlockSpec((d, n), lambda i: (0, 0), pipeline_mode=pl.Buffered(1)),
        ],
        out_specs=pl.BlockSpec((tm, n), lambda i: (i, 0)),
        out_shape=jax.ShapeDtypeStruct((n_rows, n), BF16),
        scratch_shapes=[pltpu.VMEM((tm, d), BF16)],
        compiler_params=_cparams(("arbitrary",)),
        name="proj",
    )(x2d, nw, mod, mod, w)


def _na_bias_tables(rpb):
    rows = 128
    cols = np.arange(GRID_W)
    cs = np.clip(cols - NA_KW // 2, 0, GRID_W - NA_KW)
    kc = np.arange(GRID_W)[None, :]
    valid_c = (kc >= cs[:, None]) & (kc < cs[:, None] + NA_KW)
    col_rel = np.clip(kc - cols[:, None] + (NA_KW - 1), 0, 2 * NA_KW - 2)
    t = jnp.where(valid_c[None, None], rpb.astype(F32)[:, :, col_rel], NEG)
    t_ext = jnp.concatenate([t, jnp.full_like(t[:, :1], NEG)], axis=1)

    def block_idx(rb):
        kr0 = int(np.clip(NA_QROWS * rb - NA_KH // 2, 0, rows - NA_KROWS))
        idx = np.empty((NA_QROWS, NA_KROWS), np.int32)
        for qi in range(NA_QROWS):
            qr = NA_QROWS * rb + qi
            r0 = int(np.clip(qr - NA_KH // 2, 0, rows - NA_KH))
            for kj in range(NA_KROWS):
                kr = kr0 + kj
                idx[qi, kj] = kr - qr + (NA_KH - 1) if r0 <= kr < r0 + NA_KH else 2 * NA_KH - 1
        return idx

    nblk = rows // NA_QROWS
    interior = block_idx(1)
    assert all(np.array_equal(block_idx(rb), interior) for rb in range(1, nblk - 1))
    idx = np.stack([block_idx(0), interior, block_idx(nblk - 1)])
    bb = t_ext[:, idx]
    bb = bb.transpose(0, 1, 2, 4, 3, 5)
    return bb.reshape(NA_HEADS, 3, NA_QROWS * GRID_W, NA_KROWS * GRID_W)


def _na_kernel(q_ref, k_ref, v_ref, kc_ref, vc_ref, bias_ref, o_ref):
    s_len = q_ref.shape[0]
    qb = NA_QROWS * GRID_W
    kb = NA_KROWS * GRID_W
    nblk = s_len // qb
    rows = s_len // GRID_W
    kc = kc_ref[...]
    vc = vc_ref[...]
    lane = lax.broadcasted_iota(jnp.int32, (qb, LANES), 1)
    dn_t = (((1,), (1,)), ((), ()))

    def body(rb, carry):
        kr0 = jnp.clip(NA_QROWS * rb - NA_KH // 2, 0, rows - NA_KROWS)
        ks = pl.multiple_of(kr0 * GRID_W, qb)
        qs = pl.multiple_of(rb * qb, qb)
        kw = k_ref[pl.ds(ks, kb), :]
        vw = v_ref[pl.ds(ks, kb), :]
        q = q_ref[pl.ds(qs, qb), :]
        btype = jnp.where(rb == 0, 0, jnp.where(rb == nblk - 1, 2, 1))
        outs = []
        for hh in range(2):
            in_head = (lane >= NA_HEAD_DIM * hh) & (lane < NA_HEAD_DIM * (hh + 1))
            qh = jnp.where(in_head, q, jnp.zeros_like(q))
            s_loc = lax.dot_general(qh, kw, dn_t, preferred_element_type=F32) + bias_ref[hh, btype]
            s_ctx = lax.dot_general(qh, kc, dn_t, preferred_element_type=F32)
            m = jnp.maximum(jnp.max(s_loc, axis=-1, keepdims=True), jnp.max(s_ctx, axis=-1, keepdims=True))
            p_loc = jnp.exp(s_loc - m)
            p_ctx = jnp.exp(s_ctx - m)
            den = jnp.sum(p_loc, axis=-1, keepdims=True) + jnp.sum(p_ctx, axis=-1, keepdims=True)
            o = (jnp.dot(p_loc.astype(BF16), vw, preferred_element_type=F32)
                 + jnp.dot(p_ctx.astype(BF16), vc, preferred_element_type=F32))
            outs.append(o / den)
        o_ref[pl.ds(qs, qb), :] = jnp.where(lane < NA_HEAD_DIM, outs[0], outs[1]).astype(o_ref.dtype)
        return carry

    lax.fori_loop(0, nblk, body, 0)


def _na(p_lat, p_ctx, bias, *, batch, s_len, t_len):
    hp = NA_HEADS // 2
    kblk = NA_WIDTH // LANES
    qb = NA_QROWS * GRID_W
    kbk = NA_KROWS * GRID_W
    return pl.pallas_call(
        _na_kernel,
        grid=(batch, hp),
        in_specs=[
            pl.BlockSpec((s_len, LANES), lambda b, h: (b, h)),
            pl.BlockSpec((s_len, LANES), lambda b, h: (b, kblk + h)),
            pl.BlockSpec((s_len, LANES), lambda b, h: (b, 2 * kblk + h)),
            pl.BlockSpec((t_len, LANES), lambda b, h: (b, kblk + h)),
            pl.BlockSpec((t_len, LANES), lambda b, h: (b, 2 * kblk + h)),
            pl.BlockSpec((2, 3, qb, kbk), lambda b, h: (h, 0, 0, 0)),
        ],
        out_specs=pl.BlockSpec((s_len, LANES), lambda b, h: (b, h)),
        out_shape=jax.ShapeDtypeStruct((batch * s_len, NA_WIDTH), BF16),
        compiler_params=_cparams(("arbitrary", "arbitrary")),
        name="na",
    )(p_lat, p_lat, p_lat, p_ctx, p_ctx, bias)


def _hg_kernel(q_ref, ff_ref, fb_ref, i_ref, g_ref, cff_ref, cfb_ref, ci_ref, lbl_ref, nw_ref,
               o_ref, acc_ref):
    c = HG_CHUNK
    s_len = q_ref.shape[0]
    t_len = ci_ref.shape[0]
    n = s_len // c
    dn_t = (((1,), (1,)), ((), ()))

    ll = lbl_ref[...]
    mx = jnp.maximum(ll[0], ll[1])
    e0 = jnp.exp(ll[0] - mx)
    lb = e0 / (e0 + jnp.exp(ll[1] - mx))

    def gates(pre, d):
        lbd = lb[d:d + 1]
        fg = lbd + (1.0 - lbd) * jax.nn.sigmoid(pre.astype(F32))
        return 1.0 - fg, jnp.log(fg)

    r_t = lax.broadcasted_iota(jnp.int32, (t_len, t_len), 0)
    c_t = lax.broadcasted_iota(jnp.int32, (t_len, t_len), 1)
    vct = ci_ref[...].astype(F32).T.astype(BF16)
    states = []
    for d, (fref, after) in enumerate(((cff_ref, c_t > r_t), (cfb_ref, c_t < r_t))):
        kk, lf = gates(fref[...], d)
        rest = jnp.dot(after.astype(F32), lf, precision=HIGHEST, preferred_element_type=F32)
        kd = (kk * jnp.exp(rest)).astype(BF16)
        states.append(jnp.dot(vct, kd, preferred_element_type=F32))

    r_c = lax.broadcasted_iota(jnp.int32, (c, c), 0)
    c_c = lax.broadcasted_iota(jnp.int32, (c, c), 1)
    tri = (c_c <= r_c, c_c >= r_c)

    def chunk(d, start, st):
        rows = pl.ds(start, c)
        kk, lf = gates((ff_ref, fb_ref)[d][rows, :], d)
        qf = q_ref[rows, :].astype(F32)
        qh = qf * jax.nn.sigmoid(qf)
        v = i_ref[rows, :]
        bc = jnp.dot(tri[d].astype(F32), lf, precision=HIGHEST, preferred_element_type=F32)
        btot = bc[c - 1:c] if d == 0 else bc[0:1]
        mref = bc[c // 2:c // 2 + 1]
        qe = (qh * jnp.exp(bc)).astype(BF16)
        qm = (qh * jnp.exp(bc - mref)).astype(BF16)
        km = (kk * jnp.exp(mref - bc)).astype(BF16)
        kd = (kk * jnp.exp(btot - bc)).astype(BF16)
        att = lax.dot_general(qm, km, dn_t, preferred_element_type=F32)
        att = jnp.where(tri[d], att, 0.0).astype(BF16)
        o = (jnp.dot(att, v, preferred_element_type=F32)
             + lax.dot_general(qe, st.astype(BF16), dn_t, preferred_element_type=F32))
        vt = v.astype(F32).T.astype(BF16)
        st_new = jnp.exp(btot) * st + jnp.dot(vt, kd, preferred_element_type=F32)
        return o, st_new

    acc_ref[...] = jnp.zeros_like(acc_ref)

    def body(jj, carry):
        st_f, st_b = carry
        for u in range(HG_UNROLL):
            j = jj * HG_UNROLL + u
            sf = pl.multiple_of(j * c, c)
            sb = pl.multiple_of((n - 1 - j) * c, c)
            o_f, st_f = chunk(0, sf, st_f)
            o_b, st_b = chunk(1, sb, st_b)
            acc_ref[pl.ds(sf, c), :] += o_f
            acc_ref[pl.ds(sb, c), :] += o_b
        return st_f, st_b

    lax.fori_loop(0, n // HG_UNROLL, body, (states[0], states[1]))

    rb = 512
    nw = nw_ref[...]

    def readout(j, carry):
        rows = pl.ds(pl.multiple_of(j * rb, rb), rb)
        o = acc_ref[rows, :]
        o = o * lax.rsqrt(jnp.mean(o * o, axis=-1, keepdims=True) + EPS) * nw
        o_ref[rows, :] = (o * jax.nn.silu(g_ref[rows, :].astype(F32))).astype(o_ref.dtype)
        return carry

    lax.fori_loop(0, s_len // rb, readout, 0)


def _hgrn(p_lat, p_ctx, lb_logits, norm_w, *, batch, s_len, t_len):
    base = 3 * NA_WIDTH // LANES
    nh = HG_HEADS

    def lat(group):
        return pl.BlockSpec((s_len, LANES), lambda b, h: (b, base + group * nh + h))

    def ctx(group):
        return pl.BlockSpec((t_len, LANES), lambda b, h: (b, base + group * nh + h))

    return pl.pallas_call(
        _hg_kernel,
        grid=(batch, nh),
        in_specs=[lat(0), lat(1), lat(2), lat(3), lat(4), ctx(1), ctx(2), ctx(3),
                  pl.BlockSpec((2, 2, LANES), lambda b, h: (0, 0, h)),
                  pl.BlockSpec((1, LANES), lambda b, h: (0, h))],
        out_specs=pl.BlockSpec((s_len, LANES), lambda b, h: (b, h)),
        out_shape=jax.ShapeDtypeStruct((batch * s_len, HG_WIDTH), BF16),
        scratch_shapes=[pltpu.VMEM((s_len, LANES), F32)],
        compiler_params=_cparams(("arbitrary", "arbitrary")),
        name="hgrn",
    )(p_lat, p_lat, p_lat, p_lat, p_lat, p_ctx, p_ctx, p_ctx, lb_logits, norm_w)


def _halo_specs(tm, width, n_rows):
    per = tm // HALO
    last = n_rows // HALO - 1
    return [
        pl.BlockSpec((tm, width), lambda i: (i, 0)),
        pl.BlockSpec((HALO, width), lambda i: (jnp.maximum(i * per - 1, 0), 0)),
        pl.BlockSpec((HALO, width), lambda i: (jnp.minimum((i + 1) * per, last), 0)),
    ]


def _assemble(dst, main, prev, nxt, tm):
    dst[0:HALO, :] = prev[...]
    dst[HALO:HALO + tm, :] = main[...]
    dst[HALO + tm:, :] = nxt[...]


def _edge_keep(tm, tpb):
    t = pl.program_id(0) % tpb
    row = lax.broadcasted_iota(jnp.int32, (tm + 2 * HALO, 1), 0)
    return ((row >= HALO) | (t != 0)) & ((row < HALO + tm) | (t != tpb - 1))


def _conv3(a_ext, cw, cb, tm):
    rows = tm + 2 * HALO
    prev = pltpu.roll(a_ext, 1, 0)[HALO:HALO + tm]
    nxt = pltpu.roll(a_ext, rows - 1, 0)[HALO:HALO + tm]
    cur = a_ext[HALO:HALO + tm]
    return prev * cw[0:1] + cur * cw[1:2] + nxt * cw[2:3] + cb


def _gelu_exact(x):
    return 0.5 * x * (1.0 + lax.erf(x * np.float32(np.sqrt(0.5))))


def _convglu(he_s, acc_s, wup_ref, cw_ref, cb_ref, wdown_ref, tm):
    nchunk = wdown_ref.shape[0]
    acc_s[...] = jnp.zeros_like(acc_s)

    def body(j, carry):
        a = jnp.dot(he_s[...], wup_ref[j], preferred_element_type=F32)
        val = jnp.dot(he_s[HALO:HALO + tm, :], wup_ref[nchunk + j], preferred_element_type=F32)
        cv = _conv3(a, cw_ref[j], cb_ref[j], tm)
        act = (_gelu_exact(cv) * val).astype(BF16)
        acc_s[...] += jnp.dot(act, wdown_ref[j], preferred_element_type=F32)
        return carry

    lax.fori_loop(0, nchunk, body, 0)


def _ffn0_kernel(xm, xp, xn, am, ap, an, gm, gp, gn, gate_m, shift_f, scale_f, gate_f, nw,
                 wo_ref, wup_ref, cw_ref, cb_ref, wdown_ref, o_ref,
                 xe_s, ae_s, ge_s, he_s, acc_s, *, tm, tpb):
    b = pl.program_id(0) // tpb
    _assemble(xe_s, xm, xp, xn, tm)
    _assemble(ae_s, am, ap, an, tm)
    _assemble(ge_s, gm, gp, gn, tm)
    y = (jnp.dot(ae_s[...], wo_ref[0], preferred_element_type=F32)
         + jnp.dot(ge_s[...], wo_ref[1], preferred_element_type=F32))
    x1 = xe_s[...] + gate_m[pl.ds(b, 1), :] * y
    xe_s[...] = x1
    h = _rms_mod(x1, nw[...], shift_f[pl.ds(b, 1), :], scale_f[pl.ds(b, 1), :])
    he_s[...] = jnp.where(_edge_keep(tm, tpb), h, 0.0).astype(BF16)
    _convglu(he_s, acc_s, wup_ref, cw_ref, cb_ref, wdown_ref, tm)
    o_ref[...] = xe_s[HALO:HALO + tm, :] + gate_f[pl.ds(b, 1), :] * acc_s[...]


def _ffn1_kernel(xm, xp, xn, shift_f, scale_f, gate_f, nw, fnw,
                 wup_ref, cw_ref, cb_ref, wdown_ref, o_ref,
                 xe_s, he_s, acc_s, *, tm, tpb):
    b = pl.program_id(0) // tpb
    _assemble(xe_s, xm, xp, xn, tm)
    h = _rms_mod(xe_s[...], nw[...], shift_f[pl.ds(b, 1), :], scale_f[pl.ds(b, 1), :])
    he_s[...] = jnp.where(_edge_keep(tm, tpb), h, 0.0).astype(BF16)
    _convglu(he_s, acc_s, wup_ref, cw_ref, cb_ref, wdown_ref, tm)
    x2 = xm[...] + gate_f[pl.ds(b, 1), :] * acc_s[...]
    ms = jnp.mean(x2 * x2, axis=-1, keepdims=True)
    o_ref[...] = x2 * lax.rsqrt(ms + EPS) * fnw[...]


def _mod_spec(k):
    return pl.BlockSpec((8, D_MODEL), lambda i: (0, k))


def _const_spec(shape):
    nd = len(shape)
    return pl.BlockSpec(shape, lambda i: (0,) * nd, pipeline_mode=pl.Buffered(1))


def _ffn_weight_specs(nchunk):
    return [
        _const_spec((2 * nchunk, D_MODEL, FF_CHUNK)),
        _const_spec((nchunk, 3, FF_CHUNK)),
        _const_spec((nchunk, 1, FF_CHUNK)),
        _const_spec((nchunk, FF_CHUNK, D_MODEL)),
    ]


def _ffn0(x2d, a, g, mod, nw, wo, wup, cw, cb, wdown, *, tm, tpb):
    n_rows = x2d.shape[0]
    nchunk = wdown.shape[0]
    ext = tm + 2 * HALO
    kern = functools.partial(_ffn0_kernel, tm=tm, tpb=tpb)
    return pl.pallas_call(
        kern,
        grid=(n_rows // tm,),
        in_specs=(_halo_specs(tm, D_MODEL, n_rows) + _halo_specs(tm, NA_WIDTH, n_rows)
                  + _halo_specs(tm, HG_WIDTH, n_rows)
                  + [_mod_spec(2), _mod_spec(3), _mod_spec(4), _mod_spec(5),
                     _const_spec((1, D_MODEL)), _const_spec((2, NA_WIDTH, D_MODEL))]
                  + _ffn_weight_specs(nchunk)),
        out_specs=pl.BlockSpec((tm, D_MODEL), lambda i: (i, 0)),
        out_shape=jax.ShapeDtypeStruct((n_rows, D_MODEL), F32),
        scratch_shapes=[pltpu.VMEM((ext, D_MODEL), F32), pltpu.VMEM((ext, NA_WIDTH), BF16),
                        pltpu.VMEM((ext, HG_WIDTH), BF16), pltpu.VMEM((ext, D_MODEL), BF16),
                        pltpu.VMEM((tm, D_MODEL), F32)],
        compiler_params=_cparams(("arbitrary",)),
        name="ffn0",
    )(x2d, x2d, x2d, a, a, a, g, g, g, mod, mod, mod, mod, nw, wo, wup, cw, cb, wdown)


def _ffn1(x2d, mod, nw, fnw, wup, cw, cb, wdown, *, tm, tpb):
    n_rows = x2d.shape[0]
    nchunk = wdown.shape[0]
    ext = tm + 2 * HALO
    kern = functools.partial(_ffn1_kernel, tm=tm, tpb=tpb)
    return pl.pallas_call(
        kern,
        grid=(n_rows // tm,),
        in_specs=(_halo_specs(tm, D_MODEL, n_rows)
                  + [_mod_spec(3), _mod_spec(4), _mod_spec(5),
                     _const_spec((1, D_MODEL)), _const_spec((1, D_MODEL))]
                  + _ffn_weight_specs(nchunk)),
        out_specs=pl.BlockSpec((tm, D_MODEL), lambda i: (i, 0)),
        out_shape=jax.ShapeDtypeStruct((n_rows, D_MODEL), F32),
        scratch_shapes=[pltpu.VMEM((ext, D_MODEL), F32), pltpu.VMEM((ext, D_MODEL), BF16),
                        pltpu.VMEM((tm, D_MODEL), F32)],
        compiler_params=_cparams(("arbitrary",)),
        name="ffn1",
    )(x2d, x2d, x2d, mod, mod, mod, nw, fnw, wup, cw, cb, wdown)


def _mix_kernel(xm, xp, xn, shift_m, scale_m, gate_m, nw, win_ref, cw_ref, cb_ref, wout_ref, o_ref,
                xe_s, he_s, acc_s, *, tm, tpb):
    b = pl.program_id(0) // tpb
    nchunk = wout_ref.shape[0]
    _assemble(xe_s, xm, xp, xn, tm)
    h = _rms_mod(xe_s[...], nw[...], shift_m[pl.ds(b, 1), :], scale_m[pl.ds(b, 1), :])
    he_s[...] = jnp.where(_edge_keep(tm, tpb), h, 0.0).astype(BF16)
    acc_s[...] = jnp.zeros_like(acc_s)

    def body(j, carry):
        gb = jnp.dot(he_s[HALO:HALO + tm, :], win_ref[j], preferred_element_type=F32)
        gc = jnp.dot(he_s[...], win_ref[nchunk + j], preferred_element_type=F32)
        u = jnp.dot(he_s[...], win_ref[2 * nchunk + j], preferred_element_type=F32)
        z = (gb * _conv3(gc * u, cw_ref[j], cb_ref[j], tm)).astype(BF16)
        acc_s[...] += jnp.dot(z, wout_ref[j], preferred_element_type=F32)
        return carry

    lax.fori_loop(0, nchunk, body, 0)
    o_ref[...] = xm[...] + gate_m[pl.ds(b, 1), :] * acc_s[...]


def _mix(x2d, mod, nw, win, cw, cb, wout, *, tm, tpb):
    n_rows = x2d.shape[0]
    nchunk = wout.shape[0]
    ext = tm + 2 * HALO
    kern = functools.partial(_mix_kernel, tm=tm, tpb=tpb)
    return pl.pallas_call(
        kern,
        grid=(n_rows // tm,),
        in_specs=(_halo_specs(tm, D_MODEL, n_rows)
                  + [_mod_spec(0), _mod_spec(1), _mod_spec(2), _const_spec((1, D_MODEL)),
                     _const_spec((3 * nchunk, D_MODEL, FF_CHUNK)), _const_spec((nchunk, 3, FF_CHUNK)),
                     _const_spec((nchunk, 1, FF_CHUNK)), _const_spec((nchunk, FF_CHUNK, D_MODEL))]),
        out_specs=pl.BlockSpec((tm, D_MODEL), lambda i: (i, 0)),
        out_shape=jax.ShapeDtypeStruct((n_rows, D_MODEL), F32),
        scratch_shapes=[pltpu.VMEM((ext, D_MODEL), F32), pltpu.VMEM((ext, D_MODEL), BF16),
                        pltpu.VMEM((tm, D_MODEL), F32)],
        compiler_params=_cparams(("arbitrary",)),
        name="mix",
    )(x2d, x2d, x2d, mod, mod, mod, nw, win, cw, cb, wout)


def _col_chunks(w, groups):
    k, n = w.shape
    return w.astype(BF16).reshape(k, n // FF_CHUNK, FF_CHUNK).transpose(1, 0, 2)


def _row_chunks(w):
    k, n = w.shape
    return w.astype(BF16).reshape(k // FF_CHUNK, FF_CHUNK, n)


def _vec_chunks(v):
    r, n = v.shape
    return v.astype(F32).reshape(r, n // FF_CHUNK, FF_CHUNK).transpose(1, 0, 2)


def kernel(x, c, ctx, c_ctx, ada_w, ada_b, norm_mix_w, norm_ffn_w, ev_w_in, ev_w_out, na_rpb, hg_lb_logits,
           hg_norm_w, od_w_in, od_conv_w, od_conv_b, od_w_out, ffn_w_up, ffn_conv_w, ffn_conv_b, ffn_w_down,
           final_norm_w):
    batch, s_len, d = x.shape
    t_len = ctx.shape[1]
    n_rows = batch * s_len
    tpb = s_len // TM
    x2d = x.reshape(n_rows, d)
    ctx2d = ctx.reshape(batch * t_len, d)

    c_stack = jnp.concatenate([c, c_ctx[None], jnp.zeros((8 - batch - 1, d), F32)], axis=0)
    mods = _ada(c_stack, ada_w, ada_b)

    w_in0 = ev_w_in[0].at[:, :NA_WIDTH].multiply(NA_HEAD_DIM ** -0.5).astype(BF16)
    nw_mix = norm_mix_w.reshape(-1, 1, d)
    nw_ffn = norm_ffn_w.reshape(-1, 1, d)
    p_lat = _proj(x2d, nw_mix[0], mods[0], 0, 1, w_in0, tm=TM, tpb=tpb, fixed_row=None)
    p_ctx = _proj(ctx2d, nw_mix[0], mods[0], 0, 1, w_in0, tm=t_len, tpb=1, fixed_row=batch)
    bias = _na_bias_tables(na_rpb[0])
    a_lat = _na(p_lat, p_ctx, bias, batch=batch, s_len=s_len, t_len=t_len)
    g_lat = _hgrn(p_lat, p_ctx, hg_lb_logits, hg_norm_w[0].reshape(1, HG_WIDTH),
                  batch=batch, s_len=s_len, t_len=t_len)
    wo0 = ev_w_out[0].astype(BF16).reshape(2, NA_WIDTH, d)
    x1 = _ffn0(x2d, a_lat, g_lat, mods[0], nw_ffn[0], wo0,
               _col_chunks(ffn_w_up[0], 2), _vec_chunks(ffn_conv_w[0]), _vec_chunks(ffn_conv_b[0][None]),
               _row_chunks(ffn_w_down[0]), tm=TM, tpb=tpb)

    x2 = _mix(x1, mods[1], nw_mix[1], _col_chunks(od_w_in[0], 3), _vec_chunks(od_conv_w[0]),
              _vec_chunks(od_conv_b[0][None]), _row_chunks(od_w_out[0]), tm=TM, tpb=tpb)
    out = _ffn1(x2, mods[1], nw_ffn[1], final_norm_w.reshape(1, d),
                _col_chunks(ffn_w_up[1], 2), _vec_chunks(ffn_conv_w[1]), _vec_chunks(ffn_conv_b[1][None]),
                _row_chunks(ffn_w_down[1]), tm=TM, tpb=tpb)
    return out.reshape(batch, s_len, d)
```

```python
import functools

import numpy as np
import jax
import jax.numpy as jnp
from jax import lax
from jax.experimental import pallas as pl
from jax.experimental.pallas import tpu as pltpu

F32 = jnp.float32
BF16 = jnp.bfloat16
HIGHEST = lax.Precision.HIGHEST

D_MODEL = 1024
GRID_W = 64
EPS = 1e-6
NA_HEADS = 8
NA_HEAD_DIM = 64
NA_WIDTH = NA_HEADS * NA_HEAD_DIM
NA_KH = 8
NA_KW = 16
HG_HEADS = 4
HG_DIM = 128
HG_WIDTH = HG_HEADS * HG_DIM
D_FF = 2816
EV_IN = 4096

V7X_VMEM_BYTES = 64 * 1024 * 1024
VMEM_LIMIT = V7X_VMEM_BYTES - 8 * 1024 * 1024
LANES = 128
HALO = 16
NEG = -1e30

NA_QROWS = 4
NA_KROWS = NA_QROWS + NA_KH
HG_CHUNK = 64
HG_UNROLL = 8
FF_CHUNK = 256
TM = 512


def _cparams(sem):
    return pltpu.CompilerParams(dimension_semantics=sem, vmem_limit_bytes=VMEM_LIMIT)


def _rms_mod(x, nw, shift, scale):
    ms = jnp.mean(x * x, axis=-1, keepdims=True)
    y = x * lax.rsqrt(ms + EPS) * nw
    return y * (1.0 + scale) + shift


def _ada_kernel(c_ref, w_ref, b_ref, o_ref):
    s = jax.nn.silu(c_ref[...])
    o_ref[0] = jnp.dot(s, w_ref[0], precision=HIGHEST, preferred_element_type=F32) + b_ref[0]


def _ada(c_stack, ada_w, ada_b):
    depth, d, n6 = ada_w.shape
    tn = 1536
    return pl.pallas_call(
        _ada_kernel,
        grid=(depth, n6 // tn),
        in_specs=[
            pl.BlockSpec((8, d), lambda l, j: (0, 0)),
            pl.BlockSpec((1, d, tn), lambda l, j: (l, 0, j)),
            pl.BlockSpec((1, 1, tn), lambda l, j: (l, 0, j)),
        ],
        out_specs=pl.BlockSpec((1, 8, tn), lambda l, j: (l, 0, j)),
        out_shape=jax.ShapeDtypeStruct((depth, 8, n6), F32),
        compiler_params=_cparams(("arbitrary", "arbitrary")),
        name="ada",
    )(c_stack, ada_w, ada_b.reshape(depth, 1, n6))


def _proj_kernel(x_ref, nw_ref, shift_ref, scale_ref, w_ref, o_ref, h_s, *, tpb, fixed_row, ncol):
    if fixed_row is None:
        row = pl.program_id(0) // tpb
    else:
        row = fixed_row
    h = _rms_mod(x_ref[...], nw_ref[...], shift_ref[pl.ds(row, 1), :], scale_ref[pl.ds(row, 1), :])
    h_s[...] = h.astype(BF16)
    n = w_ref.shape[1]
    for c0 in range(0, n, ncol):
        o_ref[:, c0:c0 + ncol] = jnp.dot(h_s[...], w_ref[:, c0:c0 + ncol],
                                         preferred_element_type=F32).astype(o_ref.dtype)


def _proj(x2d, nw, mod, shift_blk, scale_blk, w, *, tm, tpb, fixed_row):
    n_rows, d = x2d.shape
    n = w.shape[1]
    kern = functools.partial(_proj_kernel, tpb=tpb, fixed_row=fixed_row, ncol=512)
    return pl.pallas_call(
        kern,
        grid=(n_rows // tm,),
        in_specs=[
            pl.BlockSpec((tm, d), lambda i: (i, 0)),
            pl.BlockSpec((1, d), lambda i: (0, 0)),
            pl.BlockSpec((8, d), lambda i: (0, shift_blk)),
            pl.BlockSpec((8, d), lambda i: (0, scale_blk)),
            pl.BlockSpec((d, n), lambda i: (0, 0), pipeline_mode=pl.Buffered(1)),
        ],
        out_specs=pl.BlockSpec((tm, n), lambda i: (i, 0)),
        out_shape=jax.ShapeDtypeStruct((n_rows, n), BF16),
        scratch_shapes=[pltpu.VMEM((tm, d), BF16)],
        compiler_params=_cparams(("arbitrary",)),
        name="proj",
    )(x2d, nw, mod, mod, w)


def _na_bias_tables(rpb):
    rows = 128
    cols = np.arange(GRID_W)
    cs = np.clip(cols - NA_KW // 2, 0, GRID_W - NA_KW)
    kc = np.arange(GRID_W)[None, :]
    valid_c = (kc >= cs[:, None]) & (kc < cs[:, None] + NA_KW)
    col_rel = np.clip(kc - cols[:, None] + (NA_KW - 1), 0, 2 * NA_KW - 2)
    t = jnp.where(valid_c[None, None], rpb.astype(F32)[:, :, col_rel], NEG)
    t_ext = jnp.concatenate([t, jnp.full_like(t[:, :1], NEG)], axis=1)

    def block_idx(rb):
        kr0 = int(np.clip(NA_QROWS * rb - NA_KH // 2, 0, rows - NA_KROWS))
        idx = np.empty((NA_QROWS, NA_KROWS), np.int32)
        for qi in range(NA_QROWS):
            qr = NA_QROWS * rb + qi
            r0 = int(np.clip(qr - NA_KH // 2, 0, rows - NA_KH))
            for kj in range(NA_KROWS):
                kr = kr0 + kj
                idx[qi, kj] = kr - qr + (NA_KH - 1) if r0 <= kr < r0 + NA_KH else 2 * NA_KH - 1
        return idx

    nblk = rows // NA_QROWS
    interior = block_idx(1)
    assert all(np.array_equal(block_idx(rb), interior) for rb in range(1, nblk - 1))
    idx = np.stack([block_idx(0), interior, block_idx(nblk - 1)])
    bb = t_ext[:, idx]
    bb = bb.transpose(0, 1, 2, 4, 3, 5)
    return bb.reshape(NA_HEADS, 3, NA_QROWS * GRID_W, NA_KROWS * GRID_W)


def _na_kernel(q_ref, k_ref, v_ref, kc_ref, vc_ref, bias_ref, o_ref):
    s_len = q_ref.shape[0]
    qb = NA_QROWS * GRID_W
    kb = NA_KROWS * GRID_W
    nblk = s_len // qb
    rows = s_len // GRID_W
    kc = kc_ref[...]
    vc = vc_ref[...]
    lane = lax.broadcasted_iota(jnp.int32, (qb, LANES), 1)
    dn_t = (((1,), (1,)), ((), ()))

    def body(rb, carry):
        kr0 = jnp.clip(NA_QROWS * rb - NA_KH // 2, 0, rows - NA_KROWS)
        ks = pl.multiple_of(kr0 * GRID_W, qb)
        qs = pl.multiple_of(rb * qb, qb)
        kw = k_ref[pl.ds(ks, kb), :]
        vw = v_ref[pl.ds(ks, kb), :]
        q = q_ref[pl.ds(qs, qb), :]
        btype = jnp.where(rb == 0, 0, jnp.where(rb == nblk - 1, 2, 1))
        outs = []
        for hh in range(2):
            in_head = (lane >= NA_HEAD_DIM * hh) & (lane < NA_HEAD_DIM * (hh + 1))
            qh = jnp.where(in_head, q, jnp.zeros_like(q))
            s_loc = lax.dot_general(qh, kw, dn_t, preferred_element_type=F32) + bias_ref[hh, btype]
            s_ctx = lax.dot_general(qh, kc, dn_t, preferred_element_type=F32)
            m = jnp.maximum(jnp.max(s_loc, axis=-1, keepdims=True), jnp.max(s_ctx, axis=-1, keepdims=True))
            p_loc = jnp.exp(s_loc - m)
            p_ctx = jnp.exp(s_ctx - m)
            den = jnp.sum(p_loc, axis=-1, keepdims=True) + jnp.sum(p_ctx, axis=-1, keepdims=True)
            o = (jnp.dot(p_loc.astype(BF16), vw, preferred_element_type=F32)
                 + jnp.dot(p_ctx.astype(BF16), vc, preferred_element_type=F32))
            outs.append(o / den)
        o_ref[pl.ds(qs, qb), :] = jnp.where(lane < NA_HEAD_DIM, outs[0], outs[1]).astype(o_ref.dtype)
        return carry

    lax.fori_loop(0, nblk, body, 0)


def _na(p_lat, p_ctx, bias, *, batch, s_len, t_len):
    hp = NA_HEADS // 2
    kblk = NA_WIDTH // LANES
    qb = NA_QROWS * GRID_W
    kbk = NA_KROWS * GRID_W
    return pl.pallas_call(
        _na_kernel,
        grid=(batch, hp),
        in_specs=[
            pl.BlockSpec((s_len, LANES), lambda b, h: (b, h)),
            pl.BlockSpec((s_len, LANES), lambda b, h: (b, kblk + h)),
            pl.BlockSpec((s_len, LANES), lambda b, h: (b, 2 * kblk + h)),
            pl.BlockSpec((t_len, LANES), lambda b, h: (b, kblk + h)),
            pl.BlockSpec((t_len, LANES), lambda b, h: (b, 2 * kblk + h)),
            pl.BlockSpec((2, 3, qb, kbk), lambda b, h: (h, 0, 0, 0)),
        ],
        out_specs=pl.BlockSpec((s_len, LANES), lambda b, h: (b, h)),
        out_shape=jax.ShapeDtypeStruct((batch * s_len, NA_WIDTH), BF16),
        compiler_params=_cparams(("arbitrary", "arbitrary")),
        name="na",
    )(p_lat, p_lat, p_lat, p_ctx, p_ctx, bias)


def _hg_kernel(q_ref, ff_ref, fb_ref, i_ref, g_ref, cff_ref, cfb_ref, ci_ref, lbl_ref, nw_ref,
               o_ref, acc_ref):
    c = HG_CHUNK
    s_len = q_ref.shape[0]
    t_len = ci_ref.shape[0]
    n = s_len // c
    dn_t = (((1,), (1,)), ((), ()))

    ll = lbl_ref[...]
    mx = jnp.maximum(ll[0], ll[1])
    e0 = jnp.exp(ll[0] - mx)
    lb = e0 / (e0 + jnp.exp(ll[1] - mx))

    def gates(pre, d):
        lbd = lb[d:d + 1]
        fg = lbd + (1.0 - lbd) * jax.nn.sigmoid(pre.astype(F32))
        return 1.0 - fg, jnp.log(fg)

    r_t = lax.broadcasted_iota(jnp.int32, (t_len, t_len), 0)
    c_t = lax.broadcasted_iota(jnp.int32, (t_len, t_len), 1)
    vct = ci_ref[...].astype(F32).T.astype(BF16)
    states = []
    for d, (fref, after) in enumerate(((cff_ref, c_t > r_t), (cfb_ref, c_t < r_t))):
        kk, lf = gates(fref[...], d)
        rest = jnp.dot(after.astype(F32), lf, precision=HIGHEST, preferred_element_type=F32)
        kd = (kk * jnp.exp(rest)).astype(BF16)
        states.append(jnp.dot(vct, kd, preferred_element_type=F32))

    r_c = lax.broadcasted_iota(jnp.int32, (c, c), 0)
    c_c = lax.broadcasted_iota(jnp.int32, (c, c), 1)
    tri = (c_c <= r_c, c_c >= r_c)

    acc_ref[...] = jnp.zeros_like(acc_ref)

    def body(jj, carry):
        chains = []
        for u in range(HG_UNROLL):
            j = jj * HG_UNROLL + u
            chains.append((0, pl.multiple_of(j * c, c)))
            chains.append((1, pl.multiple_of((n - 1 - j) * c, c)))
        pre = []
        for d, start in chains:
            rows = pl.ds(start, c)
            kk, lf = gates((ff_ref, fb_ref)[d][rows, :], d)
            qf = q_ref[rows, :].astype(F32)
            pre.append((kk, lf, qf * jax.nn.sigmoid(qf), i_ref[rows, :]))
        bcs = [jnp.dot(tri[d].astype(F32), p[1], precision=HIGHEST, preferred_element_type=F32)
               for (d, _), p in zip(chains, pre)]
        ops = []
        for (d, _), (kk, lf, qh, v), bc in zip(chains, pre, bcs):
            btot = bc[c - 1:c] if d == 0 else bc[0:1]
            mref = bc[c // 2:c // 2 + 1]
            qe = (qh * jnp.exp(bc)).astype(BF16)
            qm = (qh * jnp.exp(bc - mref)).astype(BF16)
            km = (kk * jnp.exp(mref - bc)).astype(BF16)
            kd = (kk * jnp.exp(btot - bc)).astype(BF16)
            ops.append((qe, qm, km, kd, jnp.exp(btot), v))
        atts = [jnp.where(tri[d], lax.dot_general(qm, km, dn_t, preferred_element_type=F32), 0.0).astype(BF16)
                for (d, _), (qe, qm, km, kd, eb, v) in zip(chains, ops)]
        upds = [jnp.dot(v.astype(F32).T.astype(BF16), kd, preferred_element_type=F32)
                for (qe, qm, km, kd, eb, v) in ops]
        intra = [jnp.dot(att, o[5], preferred_element_type=F32) for att, o in zip(atts, ops)]
        st = list(carry)
        for idx, (d, start) in enumerate(chains):
            qe, eb = ops[idx][0], ops[idx][4]
            o = intra[idx] + lax.dot_general(qe, st[d].astype(BF16), dn_t, preferred_element_type=F32)
            st[d] = eb * st[d] + upds[idx]
            acc_ref[pl.ds(start, c), :] += o
        return st[0], st[1]

    lax.fori_loop(0, n // HG_UNROLL, body, (states[0], states[1]))

    rb = 512
    nw = nw_ref[...]

    def readout(j, carry):
        rows = pl.ds(pl.multiple_of(j * rb, rb), rb)
        o = acc_ref[rows, :]
        o = o * lax.rsqrt(jnp.mean(o * o, axis=-1, keepdims=True) + EPS) * nw
        o_ref[rows, :] = (o * jax.nn.silu(g_ref[rows, :].astype(F32))).astype(o_ref.dtype)
        return carry

    lax.fori_loop(0, s_len // rb, readout, 0)


def _hgrn(p_lat, p_ctx, lb_logits, norm_w, *, batch, s_len, t_len):
    base = 3 * NA_WIDTH // LANES
    nh = HG_HEADS

    def lat(group):
        return pl.BlockSpec((s_len, LANES), lambda b, h: (b, base + group * nh + h))

    def ctx(group):
        return pl.BlockSpec((t_len, LANES), lambda b, h: (b, base + group * nh + h))

    return pl.pallas_call(
        _hg_kernel,
        grid=(batch, nh),
        in_specs=[lat(0), lat(1), lat(2), lat(3), lat(4), ctx(1), ctx(2), ctx(3),
                  pl.BlockSpec((2, 2, LANES), lambda b, h: (0, 0, h)),
                  pl.BlockSpec((1, LANES), lambda b, h: (0, h))],
        out_specs=pl.BlockSpec((s_len, LANES), lambda b, h: (b, h)),
        out_shape=jax.ShapeDtypeStruct((batch * s_len, HG_WIDTH), BF16),
        scratch_shapes=[pltpu.VMEM((s_len, LANES), F32)],
        compiler_params=_cparams(("arbitrary", "arbitrary")),
        name="hgrn",
    )(p_lat, p_lat, p_lat, p_lat, p_lat, p_ctx, p_ctx, p_ctx, lb_logits, norm_w)


def _halo_specs(tm, width, n_rows):
    per = tm // HALO
    last = n_rows // HALO - 1
    return [
        pl.BlockSpec((tm, width), lambda i: (i, 0)),
        pl.BlockSpec((HALO, width), lambda i: (jnp.maximum(i * per - 1, 0), 0)),
        pl.BlockSpec((HALO, width), lambda i: (jnp.minimum((i + 1) * per, last), 0)),
    ]


def _assemble(dst, main, prev, nxt, tm):
    dst[0:HALO, :] = prev[...]
    dst[HALO:HALO + tm, :] = main[...]
    dst[HALO + tm:, :] = nxt[...]


def _edge_keep(tm, tpb):
    t = pl.program_id(0) % tpb
    row = lax.broadcasted_iota(jnp.int32, (tm + 2 * HALO, 1), 0)
    return ((row >= HALO) | (t != 0)) & ((row < HALO + tm) | (t != tpb - 1))


def _conv3(a_ext, cw, cb, tm):
    rows = tm + 2 * HALO
    prev = pltpu.roll(a_ext, 1, 0)[HALO:HALO + tm]
    nxt = pltpu.roll(a_ext, rows - 1, 0)[HALO:HALO + tm]
    cur = a_ext[HALO:HALO + tm]
    return prev * cw[0:1] + cur * cw[1:2] + nxt * cw[2:3] + cb


def _gelu_exact(x):
    return 0.5 * x * (1.0 + lax.erf(x * np.float32(np.sqrt(0.5))))


def _convglu(he_s, act_s, wup_ref, cw_ref, cb_ref, wdown_ref, tm):
    nchunk = wup_ref.shape[0] // 2

    def up(j):
        a = jnp.dot(he_s[...], wup_ref[j], preferred_element_type=F32)
        val = jnp.dot(he_s[HALO:HALO + tm, :], wup_ref[nchunk + j], preferred_element_type=F32)
        return a, val

    nxt = up(0)
    for j in range(nchunk):
        a, val = nxt
        if j + 1 < nchunk:
            nxt = up(j + 1)
        cv = _conv3(a, cw_ref[j], cb_ref[j], tm)
        act_s[:, j * FF_CHUNK:(j + 1) * FF_CHUNK] = (_gelu_exact(cv) * val).astype(BF16)
    return jnp.dot(act_s[...], wdown_ref[...], preferred_element_type=F32)


def _ffn0_kernel(xm, xp, xn, am, ap, an, gm, gp, gn, gate_m, shift_f, scale_f, gate_f, nw,
                 wo_ref, wup_ref, cw_ref, cb_ref, wdown_ref, o_ref,
                 xe_s, ae_s, ge_s, he_s, act_s, *, tm, tpb):
    b = pl.program_id(0) // tpb
    _assemble(xe_s, xm, xp, xn, tm)
    _assemble(ae_s, am, ap, an, tm)
    _assemble(ge_s, gm, gp, gn, tm)
    y = (jnp.dot(ae_s[...], wo_ref[0], preferred_element_type=F32)
         + jnp.dot(ge_s[...], wo_ref[1], preferred_element_type=F32))
    x1 = xe_s[...] + gate_m[pl.ds(b, 1), :] * y
    xe_s[...] = x1
    h = _rms_mod(x1, nw[...], shift_f[pl.ds(b, 1), :], scale_f[pl.ds(b, 1), :])
    he_s[...] = jnp.where(_edge_keep(tm, tpb), h, 0.0).astype(BF16)
    y2 = _convglu(he_s, act_s, wup_ref, cw_ref, cb_ref, wdown_ref, tm)
    o_ref[...] = xe_s[HALO:HALO + tm, :] + gate_f[pl.ds(b, 1), :] * y2


def _ffn1_kernel(xm, xp, xn, shift_f, scale_f, gate_f, nw, fnw,
                 wup_ref, cw_ref, cb_ref, wdown_ref, o_ref,
                 xe_s, he_s, act_s, *, tm, tpb):
    b = pl.program_id(0) // tpb
    _assemble(xe_s, xm, xp, xn, tm)
    h = _rms_mod(xe_s[...], nw[...], shift_f[pl.ds(b, 1), :], scale_f[pl.ds(b, 1), :])
    he_s[...] = jnp.where(_edge_keep(tm, tpb), h, 0.0).astype(BF16)
    y2 = _convglu(he_s, act_s, wup_ref, cw_ref, cb_ref, wdown_ref, tm)
    x2 = xm[...] + gate_f[pl.ds(b, 1), :] * y2
    ms = jnp.mean(x2 * x2, axis=-1, keepdims=True)
    o_ref[...] = x2 * lax.rsqrt(ms + EPS) * fnw[...]


def _mod_spec(k):
    return pl.BlockSpec((8, D_MODEL), lambda i: (0, k))


def _const_spec(shape):
    nd = len(shape)
    return pl.BlockSpec(shape, lambda i: (0,) * nd, pipeline_mode=pl.Buffered(1))


def _ffn_weight_specs(nchunk):
    return [
        _const_spec((2 * nchunk, D_MODEL, FF_CHUNK)),
        _const_spec((nchunk, 3, FF_CHUNK)),
        _const_spec((nchunk, 1, FF_CHUNK)),
        _const_spec((nchunk * FF_CHUNK, D_MODEL)),
    ]


def _ffn0(x2d, a, g, mod, nw, wo, wup, cw, cb, wdown, *, tm, tpb):
    n_rows = x2d.shape[0]
    nchunk = wdown.shape[0] // FF_CHUNK
    ext = tm + 2 * HALO
    kern = functools.partial(_ffn0_kernel, tm=tm, tpb=tpb)
    return pl.pallas_call(
        kern,
        grid=(n_rows // tm,),
        in_specs=(_halo_specs(tm, D_MODEL, n_rows) + _halo_specs(tm, NA_WIDTH, n_rows)
                  + _halo_specs(tm, HG_WIDTH, n_rows)
                  + [_mod_spec(2), _mod_spec(3), _mod_spec(4), _mod_spec(5),
                     _const_spec((1, D_MODEL)), _const_spec((2, NA_WIDTH, D_MODEL))]
                  + _ffn_weight_specs(nchunk)),
        out_specs=pl.BlockSpec((tm, D_MODEL), lambda i: (i, 0)),
        out_shape=jax.ShapeDtypeStruct((n_rows, D_MODEL), F32),
        scratch_shapes=[pltpu.VMEM((ext, D_MODEL), F32), pltpu.VMEM((ext, NA_WIDTH), BF16),
                        pltpu.VMEM((ext, HG_WIDTH), BF16), pltpu.VMEM((ext, D_MODEL), BF16),
                        pltpu.VMEM((tm, nchunk * FF_CHUNK), BF16)],
        compiler_params=_cparams(("arbitrary",)),
        name="ffn0",
    )(x2d, x2d, x2d, a, a, a, g, g, g, mod, mod, mod, mod, nw, wo, wup, cw, cb, wdown)


def _ffn1(x2d, mod, nw, fnw, wup, cw, cb, wdown, *, tm, tpb):
    n_rows = x2d.shape[0]
    nchunk = wdown.shape[0] // FF_CHUNK
    ext = tm + 2 * HALO
    kern = functools.partial(_ffn1_kernel, tm=tm, tpb=tpb)
    return pl.pallas_call(
        kern,
        grid=(n_rows // tm,),
        in_specs=(_halo_specs(tm, D_MODEL, n_rows)
                  + [_mod_spec(3), _mod_spec(4), _mod_spec(5),
                     _const_spec((1, D_MODEL)), _const_spec((1, D_MODEL))]
                  + _ffn_weight_specs(nchunk)),
        out_specs=pl.BlockSpec((tm, D_MODEL), lambda i: (i, 0)),
        out_shape=jax.ShapeDtypeStruct((n_rows, D_MODEL), F32),
        scratch_shapes=[pltpu.VMEM((ext, D_MODEL), F32), pltpu.VMEM((ext, D_MODEL), BF16),
                        pltpu.VMEM((tm, nchunk * FF_CHUNK), BF16)],
        compiler_params=_cparams(("arbitrary",)),
        name="ffn1",
    )(x2d, x2d, x2d, mod, mod, mod, nw, fnw, wup, cw, cb, wdown)


def _mix_kernel(xm, xp, xn, shift_m, scale_m, gate_m, nw, win_ref, cw_ref, cb_ref, wout_ref, o_ref,
                xe_s, he_s, z_s, *, tm, tpb):
    b = pl.program_id(0) // tpb
    nchunk = win_ref.shape[0] // 3
    _assemble(xe_s, xm, xp, xn, tm)
    h = _rms_mod(xe_s[...], nw[...], shift_m[pl.ds(b, 1), :], scale_m[pl.ds(b, 1), :])
    he_s[...] = jnp.where(_edge_keep(tm, tpb), h, 0.0).astype(BF16)

    def up(j):
        gb = jnp.dot(he_s[HALO:HALO + tm, :], win_ref[j], preferred_element_type=F32)
        gc = jnp.dot(he_s[...], win_ref[nchunk + j], preferred_element_type=F32)
        u = jnp.dot(he_s[...], win_ref[2 * nchunk + j], preferred_element_type=F32)
        return gb, gc, u

    nxt = up(0)
    for j in range(nchunk):
        gb, gc, u = nxt
        if j + 1 < nchunk:
            nxt = up(j + 1)
        z_s[:, j * FF_CHUNK:(j + 1) * FF_CHUNK] = (gb * _conv3(gc * u, cw_ref[j], cb_ref[j], tm)).astype(BF16)
    y = jnp.dot(z_s[...], wout_ref[...], preferred_element_type=F32)
    o_ref[...] = xm[...] + gate_m[pl.ds(b, 1), :] * y


def _mix(x2d, mod, nw, win, cw, cb, wout, *, tm, tpb):
    n_rows = x2d.shape[0]
    nchunk = wout.shape[0] // FF_CHUNK
    ext = tm + 2 * HALO
    kern = functools.partial(_mix_kernel, tm=tm, tpb=tpb)
    return pl.pallas_call(
        kern,
        grid=(n_rows // tm,),
        in_specs=(_halo_specs(tm, D_MODEL, n_rows)
                  + [_mod_spec(0), _mod_spec(1), _mod_spec(2), _const_spec((1, D_MODEL)),
                     _const_spec((3 * nchunk, D_MODEL, FF_CHUNK)), _const_spec((nchunk, 3, FF_CHUNK)),
                     _const_spec((nchunk, 1, FF_CHUNK)), _const_spec((nchunk * FF_CHUNK, D_MODEL))]),
        out_specs=pl.BlockSpec((tm, D_MODEL), lambda i: (i, 0)),
        out_shape=jax.ShapeDtypeStruct((n_rows, D_MODEL), F32),
        scratch_shapes=[pltpu.VMEM((ext, D_MODEL), F32), pltpu.VMEM((ext, D_MODEL), BF16),
                        pltpu.VMEM((tm, nchunk * FF_CHUNK), BF16)],
        compiler_params=_cparams(("arbitrary",)),
        name="mix",
    )(x2d, x2d, x2d, mod, mod, mod, nw, win, cw, cb, wout)


def _col_chunks(w, groups):
    k, n = w.shape
    return w.astype(BF16).reshape(k, n // FF_CHUNK, FF_CHUNK).transpose(1, 0, 2)


def _vec_chunks(v):
    r, n = v.shape
    return v.astype(F32).reshape(r, n // FF_CHUNK, FF_CHUNK).transpose(1, 0, 2)


def kernel(x, c, ctx, c_ctx, ada_w, ada_b, norm_mix_w, norm_ffn_w, ev_w_in, ev_w_out, na_rpb, hg_lb_logits,
           hg_norm_w, od_w_in, od_conv_w, od_conv_b, od_w_out, ffn_w_up, ffn_conv_w, ffn_conv_b, ffn_w_down,
           final_norm_w):
    batch, s_len, d = x.shape
    t_len = ctx.shape[1]
    n_rows = batch * s_len
    tpb = s_len // TM
    x2d = x.reshape(n_rows, d)
    ctx2d = ctx.reshape(batch * t_len, d)

    c_stack = jnp.concatenate([c, c_ctx[None], jnp.zeros((8 - batch - 1, d), F32)], axis=0)
    mods = _ada(c_stack, ada_w, ada_b)

    w_in0 = ev_w_in[0].at[:, :NA_WIDTH].multiply(NA_HEAD_DIM ** -0.5).astype(BF16)
    nw_mix = norm_mix_w.reshape(-1, 1, d)
    nw_ffn = norm_ffn_w.reshape(-1, 1, d)
    p_lat = _proj(x2d, nw_mix[0], mods[0], 0, 1, w_in0, tm=TM, tpb=tpb, fixed_row=None)
    p_ctx = _proj(ctx2d, nw_mix[0], mods[0], 0, 1, w_in0, tm=t_len, tpb=1, fixed_row=batch)
    bias = _na_bias_tables(na_rpb[0])
    a_lat = _na(p_lat, p_ctx, bias, batch=batch, s_len=s_len, t_len=t_len)
    g_lat = _hgrn(p_lat, p_ctx, hg_lb_logits, hg_norm_w[0].reshape(1, HG_WIDTH),
                  batch=batch, s_len=s_len, t_len=t_len)
    wo0 = ev_w_out[0].astype(BF16).reshape(2, NA_WIDTH, d)
    x1 = _ffn0(x2d, a_lat, g_lat, mods[0], nw_ffn[0], wo0,
               _col_chunks(ffn_w_up[0], 2), _vec_chunks(ffn_conv_w[0]), _vec_chunks(ffn_conv_b[0][None]),
               ffn_w_down[0].astype(BF16), tm=TM, tpb=tpb)

    x2 = _mix(x1, mods[1], nw_mix[1], _col_chunks(od_w_in[0], 3), _vec_chunks(od_conv_w[0]),
              _vec_chunks(od_conv_b[0][None]), od_w_out[0].astype(BF16), tm=TM, tpb=tpb)
    out = _ffn1(x2, mods[1], nw_ffn[1], final_norm_w.reshape(1, d),
                _col_chunks(ffn_w_up[1], 2), _vec_chunks(ffn_conv_w[1]), _vec_chunks(ffn_conv_b[1][None]),
                ffn_w_down[1].astype(BF16), tm=TM, tpb=tpb)
    return out.reshape(batch, s_len, d)
```

```python
import functools

import numpy as np
import jax
import jax.numpy as jnp
from jax import lax
from jax.experimental import pallas as pl
from jax.experimental.pallas import tpu as pltpu

F32 = jnp.float32
BF16 = jnp.bfloat16
HIGHEST = lax.Precision.HIGHEST

D_MODEL = 1024
GRID_W = 64
EPS = 1e-6
NA_HEADS = 8
NA_HEAD_DIM = 64
NA_WIDTH = NA_HEADS * NA_HEAD_DIM
NA_KH = 8
NA_KW = 16
HG_HEADS = 4
HG_DIM = 128
HG_WIDTH = HG_HEADS * HG_DIM
D_FF = 2816
EV_IN = 4096

V7X_VMEM_BYTES = 64 * 1024 * 1024
VMEM_LIMIT = V7X_VMEM_BYTES - 8 * 1024 * 1024
LANES = 128
HALO = 16
NEG = -1e30

NA_QROWS = 4
NA_KROWS = NA_QROWS + NA_KH
NA_UNROLL = 2
HG_CHUNK = 64
HG_UNROLL = 8
FF_CHUNK = 256
TM = 512


def _cparams(sem):
    return pltpu.CompilerParams(dimension_semantics=sem, vmem_limit_bytes=VMEM_LIMIT)


def _rms_mod(x, nw, shift, scale):
    ms = jnp.mean(x * x, axis=-1, keepdims=True)
    y = x * lax.rsqrt(ms + EPS) * nw
    return y * (1.0 + scale) + shift


def _ada_kernel(ct_ref, w_ref, b_ref, o_ref, *, n_cond):
    s = jax.nn.silu(ct_ref[...])
    o_ref[0] = jnp.zeros(o_ref.shape[1:], F32)
    for r in range(n_cond):
        o_ref[0, r:r + 1, :] = jnp.sum(s[:, r:r + 1] * w_ref[0], axis=0, keepdims=True) + b_ref[0]


def _ada(c_stack, ada_w, ada_b, n_cond):
    depth, d, n6 = ada_w.shape
    tn = 1536
    return pl.pallas_call(
        functools.partial(_ada_kernel, n_cond=n_cond),
        grid=(depth, n6 // tn),
        in_specs=[
            pl.BlockSpec((d, 8), lambda l, j: (0, 0)),
            pl.BlockSpec((1, d, tn), lambda l, j: (l, 0, j)),
            pl.BlockSpec((1, 1, tn), lambda l, j: (l, 0, j)),
        ],
        out_specs=pl.BlockSpec((1, 8, tn), lambda l, j: (l, 0, j)),
        out_shape=jax.ShapeDtypeStruct((depth, 8, n6), F32),
        compiler_params=_cparams(("arbitrary", "arbitrary")),
        name="ada",
    )(c_stack.T, ada_w, ada_b.reshape(depth, 1, n6))


def _proj_kernel(x_ref, nw_ref, shift_ref, scale_ref, w_ref, o_ref, h_s, *, tpb, fixed_row, ncol):
    if fixed_row is None:
        row = pl.program_id(0) // tpb
    else:
        row = fixed_row
    h = _rms_mod(x_ref[...], nw_ref[...], shift_ref[pl.ds(row, 1), :], scale_ref[pl.ds(row, 1), :])
    h_s[...] = h.astype(BF16)
    n = w_ref.shape[1]
    for c0 in range(0, n, ncol):
        o_ref[:, c0:c0 + ncol] = jnp.dot(h_s[...], w_ref[:, c0:c0 + ncol],
                                         preferred_element_type=F32).astype(o_ref.dtype)


def _proj(x2d, nw, mod, shift_blk, scale_blk, w, *, tm, tpb, fixed_row):
    n_rows, d = x2d.shape
    n = w.shape[1]
    kern = functools.partial(_proj_kernel, tpb=tpb, fixed_row=fixed_row, ncol=512)
    return pl.pallas_call(
        kern,
        grid=(n_rows // tm,),
        in_specs=[
            pl.BlockSpec((tm, d), lambda i: (i, 0)),
            pl.BlockSpec((1, d), lambda i: (0, 0)),
            pl.BlockSpec((8, d), lambda i: (0, shift_blk)),
            pl.BlockSpec((8, d), lambda i: (0, scale_blk)),
            pl.BlockSpec((d, n), lambda i: (0, 0), pipeline_mode=pl.Buffered(1)),
        ],
        out_specs=pl.BlockSpec((tm, n), lambda i: (i, 0)),
        out_shape=jax.ShapeDtypeStruct((n_rows, n), BF16),
        scratch_shapes=[pltpu.VMEM((tm, d), BF16)],
        compiler_params=_cparams(("arbitrary",)),
        name="proj",
    )(x2d, nw, mod, mod, w)


def _na_band_tables(rpb):
    cols = np.arange(GRID_W)
    cs = np.clip(cols - NA_KW // 2, 0, GRID_W - NA_KW)
    kc = np.arange(GRID_W)[None, :]
    valid_c = (kc >= cs[:, None]) & (kc < cs[:, None] + NA_KW)
    col_rel = np.clip(kc - cols[:, None] + (NA_KW - 1), 0, 2 * NA_KW - 2)
    t = jnp.where(valid_c[None, None], rpb.astype(F32)[:, :, col_rel], NEG)
    t_ext = jnp.concatenate([t, jnp.full_like(t[:, :1], NEG)], axis=1)
    return jnp.concatenate([t_ext, t_ext], axis=-1)


def _na_block_index(rows):
    masked = 2 * NA_KH - 1

    def block_idx(rb):
        kr0 = int(np.clip(NA_QROWS * rb - NA_KH // 2, 0, rows - NA_KROWS))
        idx = np.empty((NA_QROWS, NA_KROWS), np.int32)
        for qi in range(NA_QROWS):
            qr = NA_QROWS * rb + qi
            r0 = int(np.clip(qr - NA_KH // 2, 0, rows - NA_KH))
            for kj in range(NA_KROWS):
                kr = kr0 + kj
                idx[qi, kj] = kr - qr + (NA_KH - 1) if r0 <= kr < r0 + NA_KH else masked
        return idx

    nblk = rows // NA_QROWS
    interior = block_idx(1)
    assert all(np.array_equal(block_idx(rb), interior) for rb in range(1, nblk - 1))
    return np.stack([block_idx(0), interior, block_idx(nblk - 1)])


def _na_kernel(q_ref, k_ref, v_ref, kc_ref, vc_ref, band_ref, o_ref, bias_ref):
    s_len = q_ref.shape[0]
    qb = NA_QROWS * GRID_W
    kb = NA_KROWS * GRID_W
    nblk = s_len // qb
    rows = s_len // GRID_W

    idx = _na_block_index(rows)
    left = lax.broadcasted_iota(jnp.int32, (GRID_W, LANES), 1) < GRID_W
    for hh in range(2):
        for t in range(3):
            for qi in range(NA_QROWS):
                for j in range(NA_KROWS // 2):
                    a, b = int(idx[t, qi, 2 * j]), int(idx[t, qi, 2 * j + 1])
                    tile = band_ref[hh, a] if a == b else jnp.where(left, band_ref[hh, a], band_ref[hh, b])
                    bias_ref[hh, t, qi * GRID_W:(qi + 1) * GRID_W, j * LANES:(j + 1) * LANES] = tile

    kc = kc_ref[...]
    vc = vc_ref[...]
    lane = lax.broadcasted_iota(jnp.int32, (qb, LANES), 1)
    dn_t = (((1,), (1,)), ((), ()))

    def body(it, carry):
        blocks = []
        for u in range(NA_UNROLL):
            rb = it * NA_UNROLL + u
            kr0 = jnp.clip(NA_QROWS * rb - NA_KH // 2, 0, rows - NA_KROWS)
            ks = pl.multiple_of(kr0 * GRID_W, qb)
            qs = pl.multiple_of(rb * qb, qb)
            btype = jnp.where(rb == 0, 0, jnp.where(rb == nblk - 1, 2, 1))
            blocks.append((qs, k_ref[pl.ds(ks, kb), :], v_ref[pl.ds(ks, kb), :], q_ref[pl.ds(qs, qb), :], btype))
        logits = []
        for qs, kw, vw, q, btype in blocks:
            for hh in range(2):
                in_head = (lane >= NA_HEAD_DIM * hh) & (lane < NA_HEAD_DIM * (hh + 1))
                qh = jnp.where(in_head, q, jnp.zeros_like(q))
                s_loc = lax.dot_general(qh, kw, dn_t, preferred_element_type=F32) + bias_ref[hh, btype]
                s_ctx = lax.dot_general(qh, kc, dn_t, preferred_element_type=F32)
                logits.append((s_loc, s_ctx))
        probs = []
        for s_loc, s_ctx in logits:
            m = jnp.maximum(jnp.max(s_loc, axis=-1, keepdims=True), jnp.max(s_ctx, axis=-1, keepdims=True))
            p_loc = jnp.exp(s_loc - m)
            p_ctx = jnp.exp(s_ctx - m)
            den = jnp.sum(p_loc, axis=-1, keepdims=True) + jnp.sum(p_ctx, axis=-1, keepdims=True)
            probs.append((p_loc.astype(BF16), p_ctx.astype(BF16), den))
        outs = []
        for idx, (p_loc, p_ctx, den) in enumerate(probs):
            vw = blocks[idx // 2][2]
            o = (jnp.dot(p_loc, vw, preferred_element_type=F32)
                 + jnp.dot(p_ctx, vc, preferred_element_type=F32))
            outs.append(o / den)
        for u, blk in enumerate(blocks):
            o_ref[pl.ds(blk[0], qb), :] = jnp.where(lane < NA_HEAD_DIM, outs[2 * u],
                                                     outs[2 * u + 1]).astype(o_ref.dtype)
        return carry

    lax.fori_loop(0, nblk // NA_UNROLL, body, 0)


def _na(p_lat, p_ctx, band, *, batch, s_len, t_len):
    hp = NA_HEADS // 2
    kblk = NA_WIDTH // LANES
    qb = NA_QROWS * GRID_W
    kbk = NA_KROWS * GRID_W
    return pl.pallas_call(
        _na_kernel,
        grid=(batch, hp),
        in_specs=[
            pl.BlockSpec((s_len, LANES), lambda b, h: (b, h)),
            pl.BlockSpec((s_len, LANES), lambda b, h: (b, kblk + h)),
            pl.BlockSpec((s_len, LANES), lambda b, h: (b, 2 * kblk + h)),
            pl.BlockSpec((t_len, LANES), lambda b, h: (b, kblk + h)),
            pl.BlockSpec((t_len, LANES), lambda b, h: (b, 2 * kblk + h)),
            pl.BlockSpec((2, 2 * NA_KH, GRID_W, LANES), lambda b, h: (h, 0, 0, 0)),
        ],
        out_specs=pl.BlockSpec((s_len, LANES), lambda b, h: (b, h)),
        out_shape=jax.ShapeDtypeStruct((batch * s_len, NA_WIDTH), BF16),
        scratch_shapes=[pltpu.VMEM((2, 3, qb, kbk), F32)],
        compiler_params=_cparams(("arbitrary", "arbitrary")),
        name="na",
    )(p_lat, p_lat, p_lat, p_ctx, p_ctx, band)


def _hg_kernel(q_ref, ff_ref, fb_ref, i_ref, g_ref, cff_ref, cfb_ref, ci_ref, lbl_ref, nw_ref,
               o_ref, acc_ref):
    c = HG_CHUNK
    s_len = q_ref.shape[0]
    t_len = ci_ref.shape[0]
    n = s_len // c
    dn_t = (((1,), (1,)), ((), ()))

    ll = lbl_ref[...]
    mx = jnp.maximum(ll[0], ll[1])
    e0 = jnp.exp(ll[0] - mx)
    lb = e0 / (e0 + jnp.exp(ll[1] - mx))

    def gates(pre, d):
        lbd = lb[d:d + 1]
        fg = lbd + (1.0 - lbd) * jax.nn.sigmoid(pre.astype(F32))
        return 1.0 - fg, jnp.log(fg) * np.float32(1.0 / np.log(2.0))

    r_t = lax.broadcasted_iota(jnp.int32, (t_len, t_len), 0)
    c_t = lax.broadcasted_iota(jnp.int32, (t_len, t_len), 1)
    vct = ci_ref[...].astype(F32).T.astype(BF16)
    states = []
    for d, (fref, after) in enumerate(((cff_ref, c_t > r_t), (cfb_ref, c_t < r_t))):
        kk, lf = gates(fref[...], d)
        rest = jnp.dot(after.astype(F32), lf, precision=HIGHEST, preferred_element_type=F32)
        kd = (kk * jnp.exp2(rest)).astype(BF16)
        states.append(jnp.dot(vct, kd, preferred_element_type=F32))

    r_c = lax.broadcasted_iota(jnp.int32, (c, c), 0)
    c_c = lax.broadcasted_iota(jnp.int32, (c, c), 1)
    tri = (c_c <= r_c, c_c >= r_c)
    tri_b = tuple(t.astype(BF16) for t in tri)

    def cum(d, lf):
        hi = lf.astype(BF16)
        lo = (lf - hi.astype(F32)).astype(BF16)
        r = jnp.dot(tri_b[d], jnp.concatenate([hi, lo], axis=1), preferred_element_type=F32)
        return r[:, :LANES] + r[:, LANES:]

    acc_ref[...] = jnp.zeros_like(acc_ref)

    def body(jj, carry):
        chains = []
        for u in range(HG_UNROLL):
            j = jj * HG_UNROLL + u
            chains.append((0, pl.multiple_of(j * c, c)))
            chains.append((1, pl.multiple_of((n - 1 - j) * c, c)))
        pre = []
        for d, start in chains:
            rows = pl.ds(start, c)
            kk, lf = gates((ff_ref, fb_ref)[d][rows, :], d)
            qf = q_ref[rows, :].astype(F32)
            pre.append((kk, lf, qf * jax.nn.sigmoid(qf), i_ref[rows, :]))
        bcs = [cum(d, p[1]) for (d, _), p in zip(chains, pre)]
        ops = []
        for (d, _), (kk, lf, qh, v), bc in zip(chains, pre, bcs):
            btot = bc[c - 1:c] if d == 0 else bc[0:1]
            mref = bc[c // 2:c // 2 + 1]
            qe = (qh * jnp.exp2(bc)).astype(BF16)
            qm = (qh * jnp.exp2(bc - mref)).astype(BF16)
            km = (kk * jnp.exp2(mref - bc)).astype(BF16)
            kd = (kk * jnp.exp2(btot - bc)).astype(BF16)
            ops.append((qe, qm, km, kd, jnp.exp2(btot), v))
        atts = [jnp.where(tri[d], lax.dot_general(qm, km, dn_t, preferred_element_type=F32), 0.0).astype(BF16)
                for (d, _), (qe, qm, km, kd, eb, v) in zip(chains, ops)]
        upds = [jnp.dot(v.astype(F32).T.astype(BF16), kd, preferred_element_type=F32)
                for (qe, qm, km, kd, eb, v) in ops]
        intra = [jnp.dot(att, o[5], preferred_element_type=F32) for att, o in zip(atts, ops)]
        st = list(carry)
        for idx, (d, start) in enumerate(chains):
            qe, eb = ops[idx][0], ops[idx][4]
            o = intra[idx] + lax.dot_general(qe, st[d].astype(BF16), dn_t, preferred_element_type=F32)
            st[d] = eb * st[d] + upds[idx]
            acc_ref[pl.ds(start, c), :] += o
        return st[0], st[1]

    lax.fori_loop(0, n // HG_UNROLL, body, (states[0], states[1]))

    rb = 512
    nw = nw_ref[...]

    def readout(j, carry):
        rows = pl.ds(pl.multiple_of(j * rb, rb), rb)
        o = acc_ref[rows, :]
        o = o * lax.rsqrt(jnp.mean(o * o, axis=-1, keepdims=True) + EPS) * nw
        o_ref[rows, :] = (o * jax.nn.silu(g_ref[rows, :].astype(F32))).astype(o_ref.dtype)
        return carry

    lax.fori_loop(0, s_len // rb, readout, 0)


def _hgrn(p_lat, p_ctx, lb_logits, norm_w, *, batch, s_len, t_len):
    base = 3 * NA_WIDTH // LANES
    nh = HG_HEADS

    def lat(group):
        return pl.BlockSpec((s_len, LANES), lambda b, h: (b, base + group * nh + h))

    def ctx(group):
        return pl.BlockSpec((t_len, LANES), lambda b, h: (b, base + group * nh + h))

    return pl.pallas_call(
        _hg_kernel,
        grid=(batch, nh),
        in_specs=[lat(0), lat(1), lat(2), lat(3), lat(4), ctx(1), ctx(2), ctx(3),
                  pl.BlockSpec((2, 2, LANES), lambda b, h: (0, 0, h)),
                  pl.BlockSpec((1, LANES), lambda b, h: (0, h))],
        out_specs=pl.BlockSpec((s_len, LANES), lambda b, h: (b, h)),
        out_shape=jax.ShapeDtypeStruct((batch * s_len, HG_WIDTH), BF16),
        scratch_shapes=[pltpu.VMEM((s_len, LANES), F32)],
        compiler_params=_cparams(("arbitrary", "arbitrary")),
        name="hgrn",
    )(p_lat, p_lat, p_lat, p_lat, p_lat, p_ctx, p_ctx, p_ctx, lb_logits, norm_w)


def _halo_specs(tm, width, n_rows):
    per = tm // HALO
    last = n_rows // HALO - 1
    return [
        pl.BlockSpec((tm, width), lambda i: (i, 0)),
        pl.BlockSpec((HALO, width), lambda i: (jnp.maximum(i * per - 1, 0), 0)),
        pl.BlockSpec((HALO, width), lambda i: (jnp.minimum((i + 1) * per, last), 0)),
    ]


def _assemble(dst, main, prev, nxt, tm):
    dst[0:HALO, :] = prev[...]
    dst[HALO:HALO + tm, :] = main[...]
    dst[HALO + tm:, :] = nxt[...]


def _edge_keep(tm, tpb):
    t = pl.program_id(0) % tpb
    row = lax.broadcasted_iota(jnp.int32, (tm + 2 * HALO, 1), 0)
    return ((row >= HALO) | (t != 0)) & ((row < HALO + tm) | (t != tpb - 1))


def _conv3(a_ext, cw, cb, tm):
    rows = tm + 2 * HALO
    prev = pltpu.roll(a_ext, 1, 0)[HALO:HALO + tm]
    nxt = pltpu.roll(a_ext, rows - 1, 0)[HALO:HALO + tm]
    cur = a_ext[HALO:HALO + tm]
    return prev * cw[0:1] + cur * cw[1:2] + nxt * cw[2:3] + cb


def _gelu_exact(x):
    return 0.5 * x * (1.0 + lax.erf(x * np.float32(np.sqrt(0.5))))


def _convglu(he_s, act_s, wup_ref, cw_ref, cb_ref, wdown_ref, tm):
    d_ff = wdown_ref.shape[0]
    nchunk = d_ff // FF_CHUNK

    def cols(j, base=0):
        return slice(base + j * FF_CHUNK, base + (j + 1) * FF_CHUNK)

    def up(j):
        a = jnp.dot(he_s[...], wup_ref[:, cols(j)], preferred_element_type=F32)
        val = jnp.dot(he_s[HALO:HALO + tm, :], wup_ref[:, cols(j, d_ff)], preferred_element_type=F32)
        return a, val

    nxt = up(0)
    for j in range(nchunk):
        a, val = nxt
        if j + 1 < nchunk:
            nxt = up(j + 1)
        cv = _conv3(a, cw_ref[:, cols(j)], cb_ref[:, cols(j)], tm)
        act_s[:, cols(j)] = (_gelu_exact(cv) * val).astype(BF16)
    return jnp.dot(act_s[...], wdown_ref[...], preferred_element_type=F32)


def _ffn0_kernel(xm, xp, xn, am, ap, an, gm, gp, gn, gate_m, shift_f, scale_f, gate_f, nw,
                 wo_ref, wup_ref, cw_ref, cb_ref, wdown_ref, o_ref,
                 xe_s, ae_s, ge_s, he_s, act_s, *, tm, tpb):
    b = pl.program_id(0) // tpb
    _assemble(xe_s, xm, xp, xn, tm)
    _assemble(ae_s, am, ap, an, tm)
    _assemble(ge_s, gm, gp, gn, tm)
    y = (jnp.dot(ae_s[...], wo_ref[0], preferred_element_type=F32)
         + jnp.dot(ge_s[...], wo_ref[1], preferred_element_type=F32))
    x1 = xe_s[...] + gate_m[pl.ds(b, 1), :] * y
    xe_s[...] = x1
    h = _rms_mod(x1, nw[...], shift_f[pl.ds(b, 1), :], scale_f[pl.ds(b, 1), :])
    he_s[...] = jnp.where(_edge_keep(tm, tpb), h, 0.0).astype(BF16)
    y2 = _convglu(he_s, act_s, wup_ref, cw_ref, cb_ref, wdown_ref, tm)
    o_ref[...] = xe_s[HALO:HALO + tm, :] + gate_f[pl.ds(b, 1), :] * y2


def _ffn1_kernel(xm, xp, xn, shift_f, scale_f, gate_f, nw, fnw,
                 wup_ref, cw_ref, cb_ref, wdown_ref, o_ref,
                 xe_s, he_s, act_s, *, tm, tpb):
    b = pl.program_id(0) // tpb
    _assemble(xe_s, xm, xp, xn, tm)
    h = _rms_mod(xe_s[...], nw[...], shift_f[pl.ds(b, 1), :], scale_f[pl.ds(b, 1), :])
    he_s[...] = jnp.where(_edge_keep(tm, tpb), h, 0.0).astype(BF16)
    y2 = _convglu(he_s, act_s, wup_ref, cw_ref, cb_ref, wdown_ref, tm)
    x2 = xm[...] + gate_f[pl.ds(b, 1), :] * y2
    ms = jnp.mean(x2 * x2, axis=-1, keepdims=True)
    o_ref[...] = x2 * lax.rsqrt(ms + EPS) * fnw[...]


def _mod_spec(k):
    return pl.BlockSpec((8, D_MODEL), lambda i: (0, k))


def _const_spec(shape):
    nd = len(shape)
    return pl.BlockSpec(shape, lambda i: (0,) * nd, pipeline_mode=pl.Buffered(1))


def _ffn_weight_specs(d_ff):
    return [
        _const_spec((D_MODEL, 2 * d_ff)),
        _const_spec((3, d_ff)),
        _const_spec((1, d_ff)),
        _const_spec((d_ff, D_MODEL)),
    ]


def _ffn0(x2d, a, g, mod, nw, wo, wup, cw, cb, wdown, *, tm, tpb):
    n_rows = x2d.shape[0]
    d_ff = wdown.shape[0]
    ext = tm + 2 * HALO
    kern = functools.partial(_ffn0_kernel, tm=tm, tpb=tpb)
    return pl.pallas_call(
        kern,
        grid=(n_rows // tm,),
        in_specs=(_halo_specs(tm, D_MODEL, n_rows) + _halo_specs(tm, NA_WIDTH, n_rows)
                  + _halo_specs(tm, HG_WIDTH, n_rows)
                  + [_mod_spec(2), _mod_spec(3), _mod_spec(4), _mod_spec(5),
                     _const_spec((1, D_MODEL)), _const_spec((2, NA_WIDTH, D_MODEL))]
                  + _ffn_weight_specs(d_ff)),
        out_specs=pl.BlockSpec((tm, D_MODEL), lambda i: (i, 0)),
        out_shape=jax.ShapeDtypeStruct((n_rows, D_MODEL), F32),
        scratch_shapes=[pltpu.VMEM((ext, D_MODEL), F32), pltpu.VMEM((ext, NA_WIDTH), BF16),
                        pltpu.VMEM((ext, HG_WIDTH), BF16), pltpu.VMEM((ext, D_MODEL), BF16),
                        pltpu.VMEM((tm, d_ff), BF16)],
        compiler_params=_cparams(("arbitrary",)),
        name="ffn0",
    )(x2d, x2d, x2d, a, a, a, g, g, g, mod, mod, mod, mod, nw, wo, wup, cw, cb, wdown)


def _ffn1(x2d, mod, nw, fnw, wup, cw, cb, wdown, *, tm, tpb):
    n_rows = x2d.shape[0]
    d_ff = wdown.shape[0]
    ext = tm + 2 * HALO
    kern = functools.partial(_ffn1_kernel, tm=tm, tpb=tpb)
    return pl.pallas_call(
        kern,
        grid=(n_rows // tm,),
        in_specs=(_halo_specs(tm, D_MODEL, n_rows)
                  + [_mod_spec(3), _mod_spec(4), _mod_spec(5),
                     _const_spec((1, D_MODEL)), _const_spec((1, D_MODEL))]
                  + _ffn_weight_specs(d_ff)),
        out_specs=pl.BlockSpec((tm, D_MODEL), lambda i: (i, 0)),
        out_shape=jax.ShapeDtypeStruct((n_rows, D_MODEL), F32),
        scratch_shapes=[pltpu.VMEM((ext, D_MODEL), F32), pltpu.VMEM((ext, D_MODEL), BF16),
                        pltpu.VMEM((tm, d_ff), BF16)],
        compiler_params=_cparams(("arbitrary",)),
        name="ffn1",
    )(x2d, x2d, x2d, mod, mod, mod, nw, fnw, wup, cw, cb, wdown)


def _mix_kernel(xm, xp, xn, shift_m, scale_m, gate_m, nw, win_ref, cw_ref, cb_ref, wout_ref, o_ref,
                xe_s, he_s, z_s, *, tm, tpb):
    b = pl.program_id(0) // tpb
    width = wout_ref.shape[0]
    nchunk = width // FF_CHUNK
    _assemble(xe_s, xm, xp, xn, tm)
    h = _rms_mod(xe_s[...], nw[...], shift_m[pl.ds(b, 1), :], scale_m[pl.ds(b, 1), :])
    he_s[...] = jnp.where(_edge_keep(tm, tpb), h, 0.0).astype(BF16)

    def cols(j, base=0):
        return slice(base + j * FF_CHUNK, base + (j + 1) * FF_CHUNK)

    def up(j):
        gb = jnp.dot(he_s[HALO:HALO + tm, :], win_ref[:, cols(j)], preferred_element_type=F32)
        gc = jnp.dot(he_s[...], win_ref[:, cols(j, width)], preferred_element_type=F32)
        u = jnp.dot(he_s[...], win_ref[:, cols(j, 2 * width)], preferred_element_type=F32)
        return gb, gc, u

    nxt = up(0)
    for j in range(nchunk):
        gb, gc, u = nxt
        if j + 1 < nchunk:
            nxt = up(j + 1)
        z_s[:, cols(j)] = (gb * _conv3(gc * u, cw_ref[:, cols(j)], cb_ref[:, cols(j)], tm)).astype(BF16)
    y = jnp.dot(z_s[...], wout_ref[...], preferred_element_type=F32)
    o_ref[...] = xm[...] + gate_m[pl.ds(b, 1), :] * y


def _mix(x2d, mod, nw, win, cw, cb, wout, *, tm, tpb):
    n_rows = x2d.shape[0]
    width = wout.shape[0]
    ext = tm + 2 * HALO
    kern = functools.partial(_mix_kernel, tm=tm, tpb=tpb)
    return pl.pallas_call(
        kern,
        grid=(n_rows // tm,),
        in_specs=(_halo_specs(tm, D_MODEL, n_rows)
                  + [_mod_spec(0), _mod_spec(1), _mod_spec(2), _const_spec((1, D_MODEL)),
                     _const_spec((D_MODEL, 3 * width)), _const_spec((3, width)),
                     _const_spec((1, width)), _const_spec((width, D_MODEL))]),
        out_specs=pl.BlockSpec((tm, D_MODEL), lambda i: (i, 0)),
        out_shape=jax.ShapeDtypeStruct((n_rows, D_MODEL), F32),
        scratch_shapes=[pltpu.VMEM((ext, D_MODEL), F32), pltpu.VMEM((ext, D_MODEL), BF16),
                        pltpu.VMEM((tm, width), BF16)],
        compiler_params=_cparams(("arbitrary",)),
        name="mix",
    )(x2d, x2d, x2d, mod, mod, mod, nw, win, cw, cb, wout)


def kernel(x, c, ctx, c_ctx, ada_w, ada_b, norm_mix_w, norm_ffn_w, ev_w_in, ev_w_out, na_rpb, hg_lb_logits,
           hg_norm_w, od_w_in, od_conv_w, od_conv_b, od_w_out, ffn_w_up, ffn_conv_w, ffn_conv_b, ffn_w_down,
           final_norm_w):
    batch, s_len, d = x.shape
    t_len = ctx.shape[1]
    n_rows = batch * s_len
    tpb = s_len // TM
    x2d = x.reshape(n_rows, d)
    ctx2d = ctx.reshape(batch * t_len, d)

    c_stack = jnp.concatenate([c, c_ctx[None], jnp.zeros((8 - batch - 1, d), F32)], axis=0)
    mods = _ada(c_stack, ada_w, ada_b, batch + 1)

    w_in0 = ev_w_in[0].at[:, :NA_WIDTH].multiply(NA_HEAD_DIM ** -0.5).astype(BF16)
    nw_mix = norm_mix_w.reshape(-1, 1, d)
    nw_ffn = norm_ffn_w.reshape(-1, 1, d)
    p_lat = _proj(x2d, nw_mix[0], mods[0], 0, 1, w_in0, tm=TM, tpb=tpb, fixed_row=None)
    p_ctx = _proj(ctx2d, nw_mix[0], mods[0], 0, 1, w_in0, tm=t_len, tpb=1, fixed_row=batch)
    a_lat = _na(p_lat, p_ctx, _na_band_tables(na_rpb[0]), batch=batch, s_len=s_len, t_len=t_len)
    g_lat = _hgrn(p_lat, p_ctx, hg_lb_logits, hg_norm_w[0].reshape(1, HG_WIDTH),
                  batch=batch, s_len=s_len, t_len=t_len)
    wo0 = ev_w_out[0].astype(BF16).reshape(2, NA_WIDTH, d)
    x1 = _ffn0(x2d, a_lat, g_lat, mods[0], nw_ffn[0], wo0,
               ffn_w_up[0].astype(BF16), ffn_conv_w[0], ffn_conv_b[0][None],
               ffn_w_down[0].astype(BF16), tm=TM, tpb=tpb)

    x2 = _mix(x1, mods[1], nw_mix[1], od_w_in[0].astype(BF16), od_conv_w[0],
              od_conv_b[0][None], od_w_out[0].astype(BF16), tm=TM, tpb=tpb)
    out = _ffn1(x2, mods[1], nw_ffn[1], final_norm_w.reshape(1, d),
                ffn_w_up[1].astype(BF16), ffn_conv_w[1], ffn_conv_b[1][None],
                ffn_w_down[1].astype(BF16), tm=TM, tpb=tpb)
    return out.reshape(batch, s_len, d)
```

```python
import functools

import numpy as np
import jax
import jax.numpy as jnp
from jax import lax
from jax.experimental import pallas as pl
from jax.experimental.pallas import tpu as pltpu

F32 = jnp.float32
BF16 = jnp.bfloat16
HIGHEST = lax.Precision.HIGHEST

D_MODEL = 1024
GRID_W = 64
EPS = 1e-6
NA_HEADS = 8
NA_HEAD_DIM = 64
NA_WIDTH = NA_HEADS * NA_HEAD_DIM
NA_KH = 8
NA_KW = 16
HG_HEADS = 4
HG_DIM = 128
HG_WIDTH = HG_HEADS * HG_DIM
D_FF = 2816
EV_IN = 4096

V7X_VMEM_BYTES = 64 * 1024 * 1024
VMEM_LIMIT = V7X_VMEM_BYTES - 8 * 1024 * 1024
LANES = 128
HALO = 16
NEG = -1e30
LOG2E = float(np.log2(np.e))

NA_QROWS = 4
NA_KROWS = NA_QROWS + NA_KH
NA_UNROLL = 2
HG_CHUNK = 64
HG_UNROLL = 8
FF_CHUNK = 256
TM = 512


def _cparams(sem):
    return pltpu.CompilerParams(dimension_semantics=sem, vmem_limit_bytes=VMEM_LIMIT)


def _rms_mod(x, nw, shift, scale):
    ms = jnp.mean(x * x, axis=-1, keepdims=True)
    y = x * lax.rsqrt(ms + EPS) * nw
    return y * (1.0 + scale) + shift


def _ada_kernel(ct_ref, w_ref, b_ref, o_ref, *, n_cond):
    s = jax.nn.silu(ct_ref[...])
    o_ref[0] = jnp.zeros(o_ref.shape[1:], F32)
    for r in range(n_cond):
        o_ref[0, r:r + 1, :] = jnp.sum(s[:, r:r + 1] * w_ref[0], axis=0, keepdims=True) + b_ref[0]


def _ada(c_stack, ada_w, ada_b, n_cond):
    depth, d, n6 = ada_w.shape
    tn = 1536
    return pl.pallas_call(
        functools.partial(_ada_kernel, n_cond=n_cond),
        grid=(depth, n6 // tn),
        in_specs=[
            pl.BlockSpec((d, 8), lambda l, j: (0, 0)),
            pl.BlockSpec((1, d, tn), lambda l, j: (l, 0, j)),
            pl.BlockSpec((1, 1, tn), lambda l, j: (l, 0, j)),
        ],
        out_specs=pl.BlockSpec((1, 8, tn), lambda l, j: (l, 0, j)),
        out_shape=jax.ShapeDtypeStruct((depth, 8, n6), F32),
        compiler_params=_cparams(("arbitrary", "arbitrary")),
        name="ada",
    )(c_stack.T, ada_w, ada_b.reshape(depth, 1, n6))


def _proj_kernel(x_ref, nw_ref, shift_ref, scale_ref, w_ref, o_ref, h_s, *, tpb, fixed_row, ncol):
    if fixed_row is None:
        row = pl.program_id(0) // tpb
    else:
        row = fixed_row
    h = _rms_mod(x_ref[...], nw_ref[...], shift_ref[pl.ds(row, 1), :], scale_ref[pl.ds(row, 1), :])
    h_s[...] = h.astype(BF16)
    n = w_ref.shape[1]
    for c0 in range(0, n, ncol):
        o_ref[:, c0:c0 + ncol] = jnp.dot(h_s[...], w_ref[:, c0:c0 + ncol],
                                         preferred_element_type=F32).astype(o_ref.dtype)


def _proj(x2d, nw, mod, shift_blk, scale_blk, w, *, tm, tpb, fixed_row):
    n_rows, d = x2d.shape
    n = w.shape[1]
    kern = functools.partial(_proj_kernel, tpb=tpb, fixed_row=fixed_row, ncol=512)
    return pl.pallas_call(
        kern,
        grid=(n_rows // tm,),
        in_specs=[
            pl.BlockSpec((tm, d), lambda i: (i, 0)),
            pl.BlockSpec((1, d), lambda i: (0, 0)),
            pl.BlockSpec((8, d), lambda i: (0, shift_blk)),
            pl.BlockSpec((8, d), lambda i: (0, scale_blk)),
            pl.BlockSpec((d, n), lambda i: (0, 0), pipeline_mode=pl.Buffered(1)),
        ],
        out_specs=pl.BlockSpec((tm, n), lambda i: (i, 0)),
        out_shape=jax.ShapeDtypeStruct((n_rows, n), BF16),
        scratch_shapes=[pltpu.VMEM((tm, d), BF16)],
        compiler_params=_cparams(("arbitrary",)),
        name="proj",
    )(x2d, nw, mod, mod, w)


def _na_band_tables(rpb):
    cols = np.arange(GRID_W)
    cs = np.clip(cols - NA_KW // 2, 0, GRID_W - NA_KW)
    kc = np.arange(GRID_W)[None, :]
    valid_c = (kc >= cs[:, None]) & (kc < cs[:, None] + NA_KW)
    pad = GRID_W - NA_KW
    rp = jnp.pad(rpb.astype(F32) * LOG2E, ((0, 0), (0, 0), (pad, pad)))
    t = jnp.stack([rp[:, :, GRID_W - 1 - qc:2 * GRID_W - 1 - qc] for qc in range(GRID_W)], axis=2)
    t = jnp.where(valid_c[None, None], t, NEG)
    t_ext = jnp.concatenate([t, jnp.full_like(t[:, :1], NEG)], axis=1)
    return jnp.concatenate([t_ext, t_ext], axis=-1)


def _na_block_index(rows):
    masked = 2 * NA_KH - 1

    def block_idx(rb):
        kr0 = int(np.clip(NA_QROWS * rb - NA_KH // 2, 0, rows - NA_KROWS))
        idx = np.empty((NA_QROWS, NA_KROWS), np.int32)
        for qi in range(NA_QROWS):
            qr = NA_QROWS * rb + qi
            r0 = int(np.clip(qr - NA_KH // 2, 0, rows - NA_KH))
            for kj in range(NA_KROWS):
                kr = kr0 + kj
                idx[qi, kj] = kr - qr + (NA_KH - 1) if r0 <= kr < r0 + NA_KH else masked
        return idx

    nblk = rows // NA_QROWS
    interior = block_idx(1)
    assert all(np.array_equal(block_idx(rb), interior) for rb in range(1, nblk - 1))
    return np.stack([block_idx(0), interior, block_idx(nblk - 1)])


def _na_kernel(q_ref, k_ref, v_ref, kc_ref, vc_ref, band_ref, o_ref, bias_ref):
    s_len = q_ref.shape[0]
    qb = NA_QROWS * GRID_W
    kb = NA_KROWS * GRID_W
    nblk = s_len // qb
    rows = s_len // GRID_W

    idx = _na_block_index(rows)
    left = lax.broadcasted_iota(jnp.int32, (GRID_W, LANES), 1) < GRID_W
    for hh in range(2):
        for t in range(3):
            for qi in range(NA_QROWS):
                for j in range(NA_KROWS // 2):
                    a, b = int(idx[t, qi, 2 * j]), int(idx[t, qi, 2 * j + 1])
                    tile = band_ref[hh, a] if a == b else jnp.where(left, band_ref[hh, a], band_ref[hh, b])
                    bias_ref[hh, t, qi * GRID_W:(qi + 1) * GRID_W, j * LANES:(j + 1) * LANES] = tile

    lane = lax.broadcasted_iota(jnp.int32, (qb, LANES), 1)
    in_head = [(lane >= NA_HEAD_DIM * hh) & (lane < NA_HEAD_DIM * (hh + 1)) for hh in range(2)]
    dn_t = (((1,), (1,)), ((), ()))

    def head_values(v, hh):
        return jnp.concatenate([jnp.where(in_head[hh], v[r:r + qb], jnp.ones_like(v[r:r + qb]))
                                for r in range(0, v.shape[0], qb)], axis=0)

    kc = kc_ref[...]
    vc = [head_values(vc_ref[...], hh) for hh in range(2)]

    def body(it, carry):
        blocks = []
        for u in range(NA_UNROLL):
            rb = it * NA_UNROLL + u
            kr0 = jnp.clip(NA_QROWS * rb - NA_KH // 2, 0, rows - NA_KROWS)
            ks = pl.multiple_of(kr0 * GRID_W, qb)
            qs = pl.multiple_of(rb * qb, qb)
            btype = jnp.where(rb == 0, 0, jnp.where(rb == nblk - 1, 2, 1))
            blocks.append((qs, k_ref[pl.ds(ks, kb), :], v_ref[pl.ds(ks, kb), :], q_ref[pl.ds(qs, qb), :], btype))
        logits = []
        for qs, kw, vw, q, btype in blocks:
            q2 = jnp.concatenate([jnp.where(in_head[hh], q, jnp.zeros_like(q)) for hh in range(2)], axis=0)
            s_loc = lax.dot_general(q2, kw, dn_t, preferred_element_type=F32)
            s_ctx = lax.dot_general(q2, kc, dn_t, preferred_element_type=F32)
            for hh in range(2):
                logits.append((s_loc[hh * qb:(hh + 1) * qb] + bias_ref[hh, btype], s_ctx[hh * qb:(hh + 1) * qb]))
        probs = []
        for s_loc, s_ctx in logits:
            m = jnp.maximum(jnp.max(s_loc, axis=-1, keepdims=True), jnp.max(s_ctx, axis=-1, keepdims=True))
            probs.append((jnp.exp2(s_loc - m).astype(BF16), jnp.exp2(s_ctx - m).astype(BF16)))
        outs = []
        for i, (p_loc, p_ctx) in enumerate(probs):
            hh = i % 2
            o = (jnp.dot(p_loc, head_values(blocks[i // 2][2], hh), preferred_element_type=F32)
                 + jnp.dot(p_ctx, vc[hh], preferred_element_type=F32))
            outs.append(o / pltpu.roll(o, NA_HEAD_DIM, 1))
        for u, blk in enumerate(blocks):
            o_ref[pl.ds(blk[0], qb), :] = jnp.where(in_head[0], outs[2 * u], outs[2 * u + 1]).astype(o_ref.dtype)
        return carry

    lax.fori_loop(0, nblk // NA_UNROLL, body, 0)


def _na(p_lat, p_ctx, band, *, batch, s_len, t_len):
    hp = NA_HEADS // 2
    kblk = NA_WIDTH // LANES
    qb = NA_QROWS * GRID_W
    kbk = NA_KROWS * GRID_W
    return pl.pallas_call(
        _na_kernel,
        grid=(batch, hp),
        in_specs=[
            pl.BlockSpec((s_len, LANES), lambda b, h: (b, h)),
            pl.BlockSpec((s_len, LANES), lambda b, h: (b, kblk + h)),
            pl.BlockSpec((s_len, LANES), lambda b, h: (b, 2 * kblk + h)),
            pl.BlockSpec((t_len, LANES), lambda b, h: (b, kblk + h)),
            pl.BlockSpec((t_len, LANES), lambda b, h: (b, 2 * kblk + h)),
            pl.BlockSpec((2, 2 * NA_KH, GRID_W, LANES), lambda b, h: (h, 0, 0, 0)),
        ],
        out_specs=pl.BlockSpec((s_len, LANES), lambda b, h: (b, h)),
        out_shape=jax.ShapeDtypeStruct((batch * s_len, NA_WIDTH), BF16),
        scratch_shapes=[pltpu.VMEM((2, 3, qb, kbk), F32)],
        compiler_params=_cparams(("arbitrary", "arbitrary")),
        name="na",
    )(p_lat, p_lat, p_lat, p_ctx, p_ctx, band)


def _hg_kernel(q_ref, ff_ref, fb_ref, i_ref, g_ref, cff_ref, cfb_ref, ci_ref, lbl_ref, nw_ref,
               o_ref, acc_ref):
    c = HG_CHUNK
    s_len = q_ref.shape[0]
    t_len = ci_ref.shape[0]
    n = s_len // c
    dn_t = (((1,), (1,)), ((), ()))

    ll = lbl_ref[...]
    mx = jnp.maximum(ll[0], ll[1])
    e0 = jnp.exp(ll[0] - mx)
    lb = e0 / (e0 + jnp.exp(ll[1] - mx))

    f_mid = 0.5 * (1.0 + lb)
    f_amp = 0.5 * (1.0 - lb)

    def gates(pre, d):
        swing = f_amp[d:d + 1] * jnp.tanh(0.5 * pre.astype(F32))
        fg = f_mid[d:d + 1] + swing
        return (1.0 - f_mid[d:d + 1]) - swing, jnp.log(fg) * np.float32(1.0 / np.log(2.0))

    def silu(x):
        half = 0.5 * x
        return half + half * jnp.tanh(half)

    r_t = lax.broadcasted_iota(jnp.int32, (t_len, t_len), 0)
    c_t = lax.broadcasted_iota(jnp.int32, (t_len, t_len), 1)
    vct = ci_ref[...].astype(F32).T.astype(BF16)
    states = []
    for d, (fref, after) in enumerate(((cff_ref, c_t > r_t), (cfb_ref, c_t < r_t))):
        kk, lf = gates(fref[...], d)
        rest = jnp.dot(after.astype(F32), lf, precision=HIGHEST, preferred_element_type=F32)
        kd = (kk * jnp.exp2(rest)).astype(BF16)
        states.append(jnp.dot(vct, kd, preferred_element_type=F32))

    r_c = lax.broadcasted_iota(jnp.int32, (c, c), 0)
    c_c = lax.broadcasted_iota(jnp.int32, (c, c), 1)
    tri = (c_c <= r_c, c_c >= r_c)
    tri_b = tuple(t.astype(BF16) for t in tri)

    def cum(d, lf):
        hi = lf.astype(BF16)
        lo = (lf - hi.astype(F32)).astype(BF16)
        r = jnp.dot(tri_b[d], jnp.concatenate([hi, lo], axis=1), preferred_element_type=F32)
        return r[:, :LANES] + r[:, LANES:]

    acc_ref[...] = jnp.zeros_like(acc_ref)

    def body(jj, carry):
        chains = []
        for u in range(HG_UNROLL):
            j = jj * HG_UNROLL + u
            chains.append((0, pl.multiple_of(j * c, c)))
            chains.append((1, pl.multiple_of((n - 1 - j) * c, c)))
        pre = []
        for d, start in chains:
            rows = pl.ds(start, c)
            kk, lf = gates((ff_ref, fb_ref)[d][rows, :], d)
            pre.append((kk, lf, silu(q_ref[rows, :].astype(F32)), i_ref[rows, :]))
        bcs = [cum(d, p[1]) for (d, _), p in zip(chains, pre)]
        ops = []
        for (d, _), (kk, lf, qh, v), bc in zip(chains, pre, bcs):
            btot = bc[c - 1:c] if d == 0 else bc[0:1]
            mref = bc[c // 2:c // 2 + 1]
            qm = qh * jnp.exp2(bc - mref)
            km = kk * jnp.exp2(mref - bc)
            qe = (qm * jnp.exp2(mref)).astype(BF16)
            kd = (km * jnp.exp2(btot - mref)).astype(BF16)
            ops.append((qe, qm.astype(BF16), km.astype(BF16), kd, jnp.exp2(btot), v))
        atts = [jnp.where(tri[d], lax.dot_general(qm, km, dn_t, preferred_element_type=F32), 0.0).astype(BF16)
                for (d, _), (qe, qm, km, kd, eb, v) in zip(chains, ops)]
        upds = [lax.dot_general(v, kd, (((0,), (0,)), ((), ())), preferred_element_type=F32)
                for (qe, qm, km, kd, eb, v) in ops]
        intra = [jnp.dot(att, o[5], preferred_element_type=F32) for att, o in zip(atts, ops)]
        st = list(carry)
        for idx, (d, start) in enumerate(chains):
            qe, eb = ops[idx][0], ops[idx][4]
            o = intra[idx] + lax.dot_general(qe, st[d].astype(BF16), dn_t, preferred_element_type=F32)
            st[d] = eb * st[d] + upds[idx]
            acc_ref[pl.ds(start, c), :] += o
        return st[0], st[1]

    lax.fori_loop(0, n // HG_UNROLL, body, (states[0], states[1]))

    rb = 512
    nw = nw_ref[...]

    def readout(j, carry):
        rows = pl.ds(pl.multiple_of(j * rb, rb), rb)
        o = acc_ref[rows, :]
        o = o * lax.rsqrt(jnp.mean(o * o, axis=-1, keepdims=True) + EPS) * nw
        o_ref[rows, :] = (o * jax.nn.silu(g_ref[rows, :].astype(F32))).astype(o_ref.dtype)
        return carry

    lax.fori_loop(0, s_len // rb, readout, 0)


def _hgrn(p_lat, p_ctx, lb_logits, norm_w, *, batch, s_len, t_len):
    base = 3 * NA_WIDTH // LANES
    nh = HG_HEADS

    def lat(group):
        return pl.BlockSpec((s_len, LANES), lambda b, h: (b, base + group * nh + h))

    def ctx(group):
        return pl.BlockSpec((t_len, LANES), lambda b, h: (b, base + group * nh + h))

    return pl.pallas_call(
        _hg_kernel,
        grid=(batch, nh),
        in_specs=[lat(0), lat(1), lat(2), lat(3), lat(4), ctx(1), ctx(2), ctx(3),
                  pl.BlockSpec((2, 2, LANES), lambda b, h: (0, 0, h)),
                  pl.BlockSpec((1, LANES), lambda b, h: (0, h))],
        out_specs=pl.BlockSpec((s_len, LANES), lambda b, h: (b, h)),
        out_shape=jax.ShapeDtypeStruct((batch * s_len, HG_WIDTH), BF16),
        scratch_shapes=[pltpu.VMEM((s_len, LANES), F32)],
        compiler_params=_cparams(("arbitrary", "arbitrary")),
        name="hgrn",
    )(p_lat, p_lat, p_lat, p_lat, p_lat, p_ctx, p_ctx, p_ctx, lb_logits, norm_w)


def _halo_specs(tm, width, n_rows):
    per = tm // HALO
    last = n_rows // HALO - 1
    return [
        pl.BlockSpec((tm, width), lambda i: (i, 0)),
        pl.BlockSpec((HALO, width), lambda i: (jnp.maximum(i * per - 1, 0), 0)),
        pl.BlockSpec((HALO, width), lambda i: (jnp.minimum((i + 1) * per, last), 0)),
    ]


def _assemble(dst, main, prev, nxt, tm):
    dst[0:HALO, :] = prev[...]
    dst[HALO:HALO + tm, :] = main[...]
    dst[HALO + tm:, :] = nxt[...]


def _edge_keep(tm, tpb):
    t = pl.program_id(0) % tpb
    row = lax.broadcasted_iota(jnp.int32, (tm + 2 * HALO, 1), 0)
    return ((row >= HALO) | (t != 0)) & ((row < HALO + tm) | (t != tpb - 1))


def _conv3(a_ext, cw, cb, tm):
    rows = tm + 2 * HALO
    prev = pltpu.roll(a_ext, 1, 0)[HALO:HALO + tm]
    nxt = pltpu.roll(a_ext, rows - 1, 0)[HALO:HALO + tm]
    cur = a_ext[HALO:HALO + tm]
    return prev * cw[0:1] + cur * cw[1:2] + nxt * cw[2:3] + cb


def _gelu_exact(x):
    return 0.5 * x * (1.0 + lax.erf(x * np.float32(np.sqrt(0.5))))


def _convglu(he_s, act_s, wup_ref, cw_ref, cb_ref, wdown_ref, tm):
    d_ff = wdown_ref.shape[0]
    nchunk = d_ff // FF_CHUNK

    def cols(j, base=0):
        return slice(base + j * FF_CHUNK, base + (j + 1) * FF_CHUNK)

    def up(j):
        a = jnp.dot(he_s[...], wup_ref[:, cols(j)], preferred_element_type=F32)
        val = jnp.dot(he_s[HALO:HALO + tm, :], wup_ref[:, cols(j, d_ff)], preferred_element_type=F32)
        return a, val

    nxt = up(0)
    for j in range(nchunk):
        a, val = nxt
        if j + 1 < nchunk:
            nxt = up(j + 1)
        cv = _conv3(a, cw_ref[:, cols(j)], cb_ref[:, cols(j)], tm)
        act_s[:, cols(j)] = (_gelu_exact(cv) * val).astype(BF16)
    return jnp.dot(act_s[...], wdown_ref[...], preferred_element_type=F32)


def _ffn0_kernel(xm, xp, xn, am, ap, an, gm, gp, gn, gate_m, shift_f, scale_f, gate_f, nw,
                 wo_ref, wup_ref, cw_ref, cb_ref, wdown_ref, o_ref,
                 xe_s, ae_s, ge_s, he_s, act_s, *, tm, tpb):
    b = pl.program_id(0) // tpb
    _assemble(xe_s, xm, xp, xn, tm)
    _assemble(ae_s, am, ap, an, tm)
    _assemble(ge_s, gm, gp, gn, tm)
    y = (jnp.dot(ae_s[...], wo_ref[0], preferred_element_type=F32)
         + jnp.dot(ge_s[...], wo_ref[1], preferred_element_type=F32))
    x1 = xe_s[...] + gate_m[pl.ds(b, 1), :] * y
    xe_s[...] = x1
    h = _rms_mod(x1, nw[...], shift_f[pl.ds(b, 1), :], scale_f[pl.ds(b, 1), :])
    he_s[...] = jnp.where(_edge_keep(tm, tpb), h, 0.0).astype(BF16)
    y2 = _convglu(he_s, act_s, wup_ref, cw_ref, cb_ref, wdown_ref, tm)
    o_ref[...] = xe_s[HALO:HALO + tm, :] + gate_f[pl.ds(b, 1), :] * y2


def _ffn1_kernel(xm, xp, xn, shift_f, scale_f, gate_f, nw, fnw,
                 wup_ref, cw_ref, cb_ref, wdown_ref, o_ref,
                 xe_s, he_s, act_s, *, tm, tpb):
    b = pl.program_id(0) // tpb
    _assemble(xe_s, xm, xp, xn, tm)
    h = _rms_mod(xe_s[...], nw[...], shift_f[pl.ds(b, 1), :], scale_f[pl.ds(b, 1), :])
    he_s[...] = jnp.where(_edge_keep(tm, tpb), h, 0.0).astype(BF16)
    y2 = _convglu(he_s, act_s, wup_ref, cw_ref, cb_ref, wdown_ref, tm)
    x2 = xm[...] + gate_f[pl.ds(b, 1), :] * y2
    ms = jnp.mean(x2 * x2, axis=-1, keepdims=True)
    o_ref[...] = x2 * lax.rsqrt(ms + EPS) * fnw[...]


def _mod_spec(k):
    return pl.BlockSpec((8, D_MODEL), lambda i: (0, k))


def _const_spec(shape):
    nd = len(shape)
    return pl.BlockSpec(shape, lambda i: (0,) * nd, pipeline_mode=pl.Buffered(1))


def _ffn_weight_specs(d_ff):
    return [
        _const_spec((D_MODEL, 2 * d_ff)),
        _const_spec((3, d_ff)),
        _const_spec((1, d_ff)),
        _const_spec((d_ff, D_MODEL)),
    ]


def _ffn0(x2d, a, g, mod, nw, wo, wup, cw, cb, wdown, *, tm, tpb):
    n_rows = x2d.shape[0]
    d_ff = wdown.shape[0]
    ext = tm + 2 * HALO
    kern = functools.partial(_ffn0_kernel, tm=tm, tpb=tpb)
    return pl.pallas_call(
        kern,
        grid=(n_rows // tm,),
        in_specs=(_halo_specs(tm, D_MODEL, n_rows) + _halo_specs(tm, NA_WIDTH, n_rows)
                  + _halo_specs(tm, HG_WIDTH, n_rows)
                  + [_mod_spec(2), _mod_spec(3), _mod_spec(4), _mod_spec(5),
                     _const_spec((1, D_MODEL)), _const_spec((2, NA_WIDTH, D_MODEL))]
                  + _ffn_weight_specs(d_ff)),
        out_specs=pl.BlockSpec((tm, D_MODEL), lambda i: (i, 0)),
        out_shape=jax.ShapeDtypeStruct((n_rows, D_MODEL), F32),
        scratch_shapes=[pltpu.VMEM((ext, D_MODEL), F32), pltpu.VMEM((ext, NA_WIDTH), BF16),
                        pltpu.VMEM((ext, HG_WIDTH), BF16), pltpu.VMEM((ext, D_MODEL), BF16),
                        pltpu.VMEM((tm, d_ff), BF16)],
        compiler_params=_cparams(("arbitrary",)),
        name="ffn0",
    )(x2d, x2d, x2d, a, a, a, g, g, g, mod, mod, mod, mod, nw, wo, wup, cw, cb, wdown)


def _ffn1(x2d, mod, nw, fnw, wup, cw, cb, wdown, *, tm, tpb):
    n_rows = x2d.shape[0]
    d_ff = wdown.shape[0]
    ext = tm + 2 * HALO
    kern = functools.partial(_ffn1_kernel, tm=tm, tpb=tpb)
    return pl.pallas_call(
        kern,
        grid=(n_rows // tm,),
        in_specs=(_halo_specs(tm, D_MODEL, n_rows)
                  + [_mod_spec(3), _mod_spec(4), _mod_spec(5),
                     _const_spec((1, D_MODEL)), _const_spec((1, D_MODEL))]
                  + _ffn_weight_specs(d_ff)),
        out_specs=pl.BlockSpec((tm, D_MODEL), lambda i: (i, 0)),
        out_shape=jax.ShapeDtypeStruct((n_rows, D_MODEL), F32),
        scratch_shapes=[pltpu.VMEM((ext, D_MODEL), F32), pltpu.VMEM((ext, D_MODEL), BF16),
                        pltpu.VMEM((tm, d_ff), BF16)],
        compiler_params=_cparams(("arbitrary",)),
        name="ffn1",
    )(x2d, x2d, x2d, mod, mod, mod, nw, fnw, wup, cw, cb, wdown)


def _mix_kernel(xm, xp, xn, shift_m, scale_m, gate_m, nw, win_ref, cw_ref, cb_ref, wout_ref, o_ref,
                xe_s, he_s, z_s, *, tm, tpb):
    b = pl.program_id(0) // tpb
    width = wout_ref.shape[0]
    nchunk = width // FF_CHUNK
    _assemble(xe_s, xm, xp, xn, tm)
    h = _rms_mod(xe_s[...], nw[...], shift_m[pl.ds(b, 1), :], scale_m[pl.ds(b, 1), :])
    he_s[...] = jnp.where(_edge_keep(tm, tpb), h, 0.0).astype(BF16)

    def cols(j, base=0):
        return slice(base + j * FF_CHUNK, base + (j + 1) * FF_CHUNK)

    def up(j):
        gb = jnp.dot(he_s[HALO:HALO + tm, :], win_ref[:, cols(j)], preferred_element_type=F32)
        gc = jnp.dot(he_s[...], win_ref[:, cols(j, width)], preferred_element_type=F32)
        u = jnp.dot(he_s[...], win_ref[:, cols(j, 2 * width)], preferred_element_type=F32)
        return gb, gc, u

    nxt = up(0)
    for j in range(nchunk):
        gb, gc, u = nxt
        if j + 1 < nchunk:
            nxt = up(j + 1)
        z_s[:, cols(j)] = (gb * _conv3(gc * u, cw_ref[:, cols(j)], cb_ref[:, cols(j)], tm)).astype(BF16)
    y = jnp.dot(z_s[...], wout_ref[...], preferred_element_type=F32)
    o_ref[...] = xm[...] + gate_m[pl.ds(b, 1), :] * y


def _mix(x2d, mod, nw, win, cw, cb, wout, *, tm, tpb):
    n_rows = x2d.shape[0]
    width = wout.shape[0]
    ext = tm + 2 * HALO
    kern = functools.partial(_mix_kernel, tm=tm, tpb=tpb)
    return pl.pallas_call(
        kern,
        grid=(n_rows // tm,),
        in_specs=(_halo_specs(tm, D_MODEL, n_rows)
                  + [_mod_spec(0), _mod_spec(1), _mod_spec(2), _const_spec((1, D_MODEL)),
                     _const_spec((D_MODEL, 3 * width)), _const_spec((3, width)),
                     _const_spec((1, width)), _const_spec((width, D_MODEL))]),
        out_specs=pl.BlockSpec((tm, D_MODEL), lambda i: (i, 0)),
        out_shape=jax.ShapeDtypeStruct((n_rows, D_MODEL), F32),
        scratch_shapes=[pltpu.VMEM((ext, D_MODEL), F32), pltpu.VMEM((ext, D_MODEL), BF16),
                        pltpu.VMEM((tm, width), BF16)],
        compiler_params=_cparams(("arbitrary",)),
        name="mix",
    )(x2d, x2d, x2d, mod, mod, mod, nw, win, cw, cb, wout)


def kernel(x, c, ctx, c_ctx, ada_w, ada_b, norm_mix_w, norm_ffn_w, ev_w_in, ev_w_out, na_rpb, hg_lb_logits,
           hg_norm_w, od_w_in, od_conv_w, od_conv_b, od_w_out, ffn_w_up, ffn_conv_w, ffn_conv_b, ffn_w_down,
           final_norm_w):
    batch, s_len, d = x.shape
    t_len = ctx.shape[1]
    n_rows = batch * s_len
    tpb = s_len // TM
    x2d = x.reshape(n_rows, d)
    ctx2d = ctx.reshape(batch * t_len, d)

    c_stack = jnp.concatenate([c, c_ctx[None], jnp.zeros((8 - batch - 1, d), F32)], axis=0)
    mods = _ada(c_stack, ada_w, ada_b, batch + 1)

    w_in0 = ev_w_in[0].at[:, :NA_WIDTH].multiply(LOG2E * NA_HEAD_DIM ** -0.5).astype(BF16)
    nw_mix = norm_mix_w.reshape(-1, 1, d)
    nw_ffn = norm_ffn_w.reshape(-1, 1, d)
    p_lat = _proj(x2d, nw_mix[0], mods[0], 0, 1, w_in0, tm=TM, tpb=tpb, fixed_row=None)
    p_ctx = _proj(ctx2d, nw_mix[0], mods[0], 0, 1, w_in0, tm=t_len, tpb=1, fixed_row=batch)
    a_lat = _na(p_lat, p_ctx, _na_band_tables(na_rpb[0]), batch=batch, s_len=s_len, t_len=t_len)
    g_lat = _hgrn(p_lat, p_ctx, hg_lb_logits, hg_norm_w[0].reshape(1, HG_WIDTH),
                  batch=batch, s_len=s_len, t_len=t_len)
    wo0 = ev_w_out[0].astype(BF16).reshape(2, NA_WIDTH, d)
    x1 = _ffn0(x2d, a_lat, g_lat, mods[0], nw_ffn[0], wo0,
               ffn_w_up[0].astype(BF16), ffn_conv_w[0], ffn_conv_b[0][None],
               ffn_w_down[0].astype(BF16), tm=TM, tpb=tpb)

    x2 = _mix(x1, mods[1], nw_mix[1], od_w_in[0].astype(BF16), od_conv_w[0],
              od_conv_b[0][None], od_w_out[0].astype(BF16), tm=TM, tpb=tpb)
    out = _ffn1(x2, mods[1], nw_ffn[1], final_norm_w.reshape(1, d),
                ffn_w_up[1].astype(BF16), ffn_conv_w[1], ffn_conv_b[1][None],
                ffn_w_down[1].astype(BF16), tm=TM, tpb=tpb)
    return out.reshape(batch, s_len, d)
```

```python
import functools

import numpy as np
import jax
import jax.numpy as jnp
from jax import lax
from jax.experimental import pallas as pl
from jax.experimental.pallas import tpu as pltpu

F32 = jnp.float32
BF16 = jnp.bfloat16
HIGHEST = lax.Precision.HIGHEST

D_MODEL = 1024
GRID_W = 64
EPS = 1e-6
NA_HEADS = 8
NA_HEAD_DIM = 64
NA_WIDTH = NA_HEADS * NA_HEAD_DIM
NA_KH = 8
NA_KW = 16
HG_HEADS = 4
HG_DIM = 128
HG_WIDTH = HG_HEADS * HG_DIM
D_FF = 2816
EV_IN = 4096

V7X_VMEM_BYTES = 64 * 1024 * 1024
VMEM_LIMIT = V7X_VMEM_BYTES - 8 * 1024 * 1024
LANES = 128
HALO = 16
NEG = -1e30
LOG2E = float(np.log2(np.e))

NA_QROWS = 4
NA_KROWS = NA_QROWS + NA_KH
NA_UNROLL = 2
HG_CHUNK = 64
HG_UNROLL = 8
FF_CHUNK = 256
TM = 1024
TM_FUSED = 512


def _cparams(sem):
    return pltpu.CompilerParams(dimension_semantics=sem, vmem_limit_bytes=VMEM_LIMIT)


def _rms_mod(x, nw, shift, scale):
    ms = jnp.mean(x * x, axis=-1, keepdims=True)
    return (x * lax.rsqrt(ms + EPS)) * (nw * (1.0 + scale)) + shift


def _ada_kernel(ct_ref, w_ref, b_ref, o_ref, *, n_cond):
    s = jax.nn.silu(ct_ref[...])
    o_ref[0] = jnp.zeros(o_ref.shape[1:], F32)
    for r in range(n_cond):
        o_ref[0, r:r + 1, :] = jnp.sum(s[:, r:r + 1] * w_ref[0], axis=0, keepdims=True) + b_ref[0]


def _ada(c_stack, ada_w, ada_b, n_cond):
    depth, d, n6 = ada_w.shape
    tn = 1536
    return pl.pallas_call(
        functools.partial(_ada_kernel, n_cond=n_cond),
        grid=(depth, n6 // tn),
        in_specs=[
            pl.BlockSpec((d, 8), lambda l, j: (0, 0)),
            pl.BlockSpec((1, d, tn), lambda l, j: (l, 0, j)),
            pl.BlockSpec((1, 1, tn), lambda l, j: (l, 0, j)),
        ],
        out_specs=pl.BlockSpec((1, 8, tn), lambda l, j: (l, 0, j)),
        out_shape=jax.ShapeDtypeStruct((depth, 8, n6), F32),
        compiler_params=_cparams(("arbitrary", "arbitrary")),
        name="ada",
    )(c_stack.T, ada_w, ada_b.reshape(depth, 1, n6))


def _proj_kernel(x_ref, nw_ref, shift_ref, scale_ref, w_ref, o_ref, h_s, *, tpb, fixed_row, ncol):
    if fixed_row is None:
        row = pl.program_id(0) // tpb
    else:
        row = fixed_row
    h = _rms_mod(x_ref[...], nw_ref[...], shift_ref[pl.ds(row, 1), :], scale_ref[pl.ds(row, 1), :])
    h_s[...] = h.astype(BF16)
    n = w_ref.shape[1]
    for c0 in range(0, n, ncol):
        o_ref[:, c0:c0 + ncol] = jnp.dot(h_s[...], w_ref[:, c0:c0 + ncol],
                                         preferred_element_type=F32).astype(o_ref.dtype)


def _proj(x2d, nw, mod, shift_blk, scale_blk, w, *, tm, tpb, fixed_row):
    n_rows, d = x2d.shape
    n = w.shape[1]
    kern = functools.partial(_proj_kernel, tpb=tpb, fixed_row=fixed_row, ncol=512)
    return pl.pallas_call(
        kern,
        grid=(n_rows // tm,),
        in_specs=[
            pl.BlockSpec((tm, d), lambda i: (i, 0)),
            pl.BlockSpec((1, d), lambda i: (0, 0)),
            pl.BlockSpec((8, d), lambda i: (0, shift_blk)),
            pl.BlockSpec((8, d), lambda i: (0, scale_blk)),
            pl.BlockSpec((d, n), lambda i: (0, 0), pipeline_mode=pl.Buffered(1)),
        ],
        out_specs=pl.BlockSpec((tm, n), lambda i: (i, 0)),
        out_shape=jax.ShapeDtypeStruct((n_rows, n), BF16),
        scratch_shapes=[pltpu.VMEM((tm, d), BF16)],
        compiler_params=_cparams(("arbitrary",)),
        name="proj",
    )(x2d, nw, mod, mod, w)


def _na_band_tables(rpb):
    cols = np.arange(GRID_W)
    cs = np.clip(cols - NA_KW // 2, 0, GRID_W - NA_KW)
    kc = np.arange(GRID_W)[None, :]
    valid_c = (kc >= cs[:, None]) & (kc < cs[:, None] + NA_KW)
    pad = GRID_W - NA_KW
    rp = jnp.pad(rpb.astype(F32) * LOG2E, ((0, 0), (0, 0), (pad, pad)))
    t = jnp.stack([rp[:, :, GRID_W - 1 - qc:2 * GRID_W - 1 - qc] for qc in range(GRID_W)], axis=2)
    t = jnp.where(valid_c[None, None], t, NEG)
    t_ext = jnp.concatenate([t, jnp.full_like(t[:, :1], NEG)], axis=1)
    return jnp.concatenate([t_ext, t_ext], axis=-1)


def _na_block_index(rows):
    masked = 2 * NA_KH - 1

    def block_idx(rb):
        kr0 = int(np.clip(NA_QROWS * rb - NA_KH // 2, 0, rows - NA_KROWS))
        idx = np.empty((NA_QROWS, NA_KROWS), np.int32)
        for qi in range(NA_QROWS):
            qr = NA_QROWS * rb + qi
            r0 = int(np.clip(qr - NA_KH // 2, 0, rows - NA_KH))
            for kj in range(NA_KROWS):
                kr = kr0 + kj
                idx[qi, kj] = kr - qr + (NA_KH - 1) if r0 <= kr < r0 + NA_KH else masked
        return idx

    nblk = rows // NA_QROWS
    interior = block_idx(1)
    assert all(np.array_equal(block_idx(rb), interior) for rb in range(1, nblk - 1))
    return np.stack([block_idx(0), interior, block_idx(nblk - 1)])


def _na_kernel(q_ref, k_ref, v_ref, kc_ref, vc_ref, band_ref, o_ref, bias_ref):
    s_len = q_ref.shape[0]
    qb = NA_QROWS * GRID_W
    kb = NA_KROWS * GRID_W
    nblk = s_len // qb
    rows = s_len // GRID_W

    idx = _na_block_index(rows)
    left = lax.broadcasted_iota(jnp.int32, (GRID_W, LANES), 1) < GRID_W
    for hh in range(2):
        for t in range(3):
            for qi in range(NA_QROWS):
                for j in range(NA_KROWS // 2):
                    a, b = int(idx[t, qi, 2 * j]), int(idx[t, qi, 2 * j + 1])
                    tile = band_ref[hh, a] if a == b else jnp.where(left, band_ref[hh, a], band_ref[hh, b])
                    bias_ref[hh, t, qi * GRID_W:(qi + 1) * GRID_W, j * LANES:(j + 1) * LANES] = tile

    lane = lax.broadcasted_iota(jnp.int32, (qb, LANES), 1)
    in_head = [(lane >= NA_HEAD_DIM * hh) & (lane < NA_HEAD_DIM * (hh + 1)) for hh in range(2)]
    dn_t = (((1,), (1,)), ((), ()))

    def head_values(v, hh):
        return jnp.concatenate([jnp.where(in_head[hh], v[r:r + qb], jnp.ones_like(v[r:r + qb]))
                                for r in range(0, v.shape[0], qb)], axis=0)

    kc = kc_ref[...]
    vc = [head_values(vc_ref[...], hh) for hh in range(2)]

    def body(it, carry):
        blocks = []
        for u in range(NA_UNROLL):
            rb = it * NA_UNROLL + u
            kr0 = jnp.clip(NA_QROWS * rb - NA_KH // 2, 0, rows - NA_KROWS)
            ks = pl.multiple_of(kr0 * GRID_W, qb)
            qs = pl.multiple_of(rb * qb, qb)
            btype = jnp.where(rb == 0, 0, jnp.where(rb == nblk - 1, 2, 1))
            blocks.append((qs, k_ref[pl.ds(ks, kb), :], v_ref[pl.ds(ks, kb), :], q_ref[pl.ds(qs, qb), :], btype))
        logits = []
        for qs, kw, vw, q, btype in blocks:
            q2 = jnp.concatenate([jnp.where(in_head[hh], q, jnp.zeros_like(q)) for hh in range(2)], axis=0)
            s_loc = lax.dot_general(q2, kw, dn_t, preferred_element_type=F32)
            s_ctx = lax.dot_general(q2, kc, dn_t, preferred_element_type=F32)
            for hh in range(2):
                logits.append((s_loc[hh * qb:(hh + 1) * qb] + bias_ref[hh, btype], s_ctx[hh * qb:(hh + 1) * qb]))
        probs = []
        for s_loc, s_ctx in logits:
            m = jnp.maximum(jnp.max(s_loc, axis=-1, keepdims=True), jnp.max(s_ctx, axis=-1, keepdims=True))
            probs.append((jnp.exp2(s_loc - m).astype(BF16), jnp.exp2(s_ctx - m).astype(BF16)))
        outs = []
        for i, (p_loc, p_ctx) in enumerate(probs):
            hh = i % 2
            o = (jnp.dot(p_loc, head_values(blocks[i // 2][2], hh), preferred_element_type=F32)
                 + jnp.dot(p_ctx, vc[hh], preferred_element_type=F32))
            outs.append(o / pltpu.roll(o, NA_HEAD_DIM, 1))
        for u, blk in enumerate(blocks):
            o_ref[pl.ds(blk[0], qb), :] = jnp.where(in_head[0], outs[2 * u], outs[2 * u + 1]).astype(o_ref.dtype)
        return carry

    lax.fori_loop(0, nblk // NA_UNROLL, body, 0)


def _na(p_lat, p_ctx, band, *, batch, s_len, t_len):
    hp = NA_HEADS // 2
    kblk = NA_WIDTH // LANES
    qb = NA_QROWS * GRID_W
    kbk = NA_KROWS * GRID_W
    return pl.pallas_call(
        _na_kernel,
        grid=(batch, hp),
        in_specs=[
            pl.BlockSpec((s_len, LANES), lambda b, h: (b, h)),
            pl.BlockSpec((s_len, LANES), lambda b, h: (b, kblk + h)),
            pl.BlockSpec((s_len, LANES), lambda b, h: (b, 2 * kblk + h)),
            pl.BlockSpec((t_len, LANES), lambda b, h: (b, kblk + h)),
            pl.BlockSpec((t_len, LANES), lambda b, h: (b, 2 * kblk + h)),
            pl.BlockSpec((2, 2 * NA_KH, GRID_W, LANES), lambda b, h: (h, 0, 0, 0)),
        ],
        out_specs=pl.BlockSpec((s_len, LANES), lambda b, h: (b, h)),
        out_shape=jax.ShapeDtypeStruct((batch * s_len, NA_WIDTH), BF16),
        scratch_shapes=[pltpu.VMEM((2, 3, qb, kbk), F32)],
        compiler_params=_cparams(("arbitrary", "arbitrary")),
        name="na",
    )(p_lat, p_lat, p_lat, p_ctx, p_ctx, band)


def _hg_kernel(q_ref, ff_ref, fb_ref, i_ref, g_ref, cff_ref, cfb_ref, ci_ref, lbl_ref, nw_ref,
               o_ref, acc_ref):
    c = HG_CHUNK
    s_len = q_ref.shape[0]
    t_len = ci_ref.shape[0]
    n = s_len // c
    dn_t = (((1,), (1,)), ((), ()))

    ll = lbl_ref[...]
    mx = jnp.maximum(ll[0], ll[1])
    e0 = jnp.exp(ll[0] - mx)
    lb = e0 / (e0 + jnp.exp(ll[1] - mx))

    f_mid = 0.5 * (1.0 + lb)
    f_amp = 0.5 * (1.0 - lb)

    def gates(pre, d):
        swing = f_amp[d:d + 1] * jnp.tanh(0.5 * pre.astype(F32))
        fg = f_mid[d:d + 1] + swing
        return (1.0 - f_mid[d:d + 1]) - swing, jnp.log(fg) * np.float32(1.0 / np.log(2.0))

    def silu(x):
        half = 0.5 * x
        return half + half * jnp.tanh(half)

    r_t = lax.broadcasted_iota(jnp.int32, (t_len, t_len), 0)
    c_t = lax.broadcasted_iota(jnp.int32, (t_len, t_len), 1)
    vct = ci_ref[...].astype(F32).T.astype(BF16)
    states = []
    for d, (fref, after) in enumerate(((cff_ref, c_t > r_t), (cfb_ref, c_t < r_t))):
        kk, lf = gates(fref[...], d)
        rest = jnp.dot(after.astype(F32), lf, precision=HIGHEST, preferred_element_type=F32)
        kd = (kk * jnp.exp2(rest)).astype(BF16)
        states.append(jnp.dot(vct, kd, preferred_element_type=F32))

    r_c = lax.broadcasted_iota(jnp.int32, (c, c), 0)
    c_c = lax.broadcasted_iota(jnp.int32, (c, c), 1)
    tri = (c_c <= r_c, c_c >= r_c)
    tri_b = tuple(t.astype(BF16) for t in tri)

    def cum(d, lf):
        hi = lf.astype(BF16)
        lo = (lf - hi.astype(F32)).astype(BF16)
        r = jnp.dot(tri_b[d], jnp.concatenate([hi, lo], axis=1), preferred_element_type=F32)
        return r[:, :LANES] + r[:, LANES:]

    acc_ref[...] = jnp.zeros_like(acc_ref)

    def body(jj, carry):
        chains = []
        for u in range(HG_UNROLL):
            j = jj * HG_UNROLL + u
            chains.append((0, pl.multiple_of(j * c, c)))
            chains.append((1, pl.multiple_of((n - 1 - j) * c, c)))
        pre = []
        for d, start in chains:
            rows = pl.ds(start, c)
            kk, lf = gates((ff_ref, fb_ref)[d][rows, :], d)
            pre.append((kk, lf, silu(q_ref[rows, :].astype(F32)), i_ref[rows, :]))
        bcs = [cum(d, p[1]) for (d, _), p in zip(chains, pre)]
        ops = []
        for (d, _), (kk, lf, qh, v), bc in zip(chains, pre, bcs):
            btot = bc[c - 1:c] if d == 0 else bc[0:1]
            mref = bc[c // 2:c // 2 + 1]
            qm = qh * jnp.exp2(bc - mref)
            km = kk * jnp.exp2(mref - bc)
            qe = (qm * jnp.exp2(mref)).astype(BF16)
            kd = (km * jnp.exp2(btot - mref)).astype(BF16)
            ops.append((qe, qm.astype(BF16), km.astype(BF16), kd, jnp.exp2(btot), v))
        atts = [jnp.where(tri[d], lax.dot_general(qm, km, dn_t, preferred_element_type=F32), 0.0).astype(BF16)
                for (d, _), (qe, qm, km, kd, eb, v) in zip(chains, ops)]
        upds = [lax.dot_general(v, kd, (((0,), (0,)), ((), ())), preferred_element_type=F32)
                for (qe, qm, km, kd, eb, v) in ops]
        intra = [jnp.dot(att, o[5], preferred_element_type=F32) for att, o in zip(atts, ops)]
        st = list(carry)
        for idx, (d, start) in enumerate(chains):
            qe, eb = ops[idx][0], ops[idx][4]
            o = intra[idx] + lax.dot_general(qe, st[d].astype(BF16), dn_t, preferred_element_type=F32)
            st[d] = eb * st[d] + upds[idx]
            acc_ref[pl.ds(start, c), :] += o
        return st[0], st[1]

    lax.fori_loop(0, n // HG_UNROLL, body, (states[0], states[1]))

    rb = 512
    nw = nw_ref[...]

    def readout(j, carry):
        rows = pl.ds(pl.multiple_of(j * rb, rb), rb)
        o = acc_ref[rows, :]
        o = o * lax.rsqrt(jnp.mean(o * o, axis=-1, keepdims=True) + EPS) * nw
        o_ref[rows, :] = (o * jax.nn.silu(g_ref[rows, :].astype(F32))).astype(o_ref.dtype)
        return carry

    lax.fori_loop(0, s_len // rb, readout, 0)


def _hgrn(p_lat, p_ctx, lb_logits, norm_w, *, batch, s_len, t_len):
    base = 3 * NA_WIDTH // LANES
    nh = HG_HEADS

    def lat(group):
        return pl.BlockSpec((s_len, LANES), lambda b, h: (b, base + group * nh + h))

    def ctx(group):
        return pl.BlockSpec((t_len, LANES), lambda b, h: (b, base + group * nh + h))

    return pl.pallas_call(
        _hg_kernel,
        grid=(batch, nh),
        in_specs=[lat(0), lat(1), lat(2), lat(3), lat(4), ctx(1), ctx(2), ctx(3),
                  pl.BlockSpec((2, 2, LANES), lambda b, h: (0, 0, h)),
                  pl.BlockSpec((1, LANES), lambda b, h: (0, h))],
        out_specs=pl.BlockSpec((s_len, LANES), lambda b, h: (b, h)),
        out_shape=jax.ShapeDtypeStruct((batch * s_len, HG_WIDTH), BF16),
        scratch_shapes=[pltpu.VMEM((s_len, LANES), F32)],
        compiler_params=_cparams(("arbitrary", "arbitrary")),
        name="hgrn",
    )(p_lat, p_lat, p_lat, p_lat, p_lat, p_ctx, p_ctx, p_ctx, lb_logits, norm_w)


def _halo_specs(tm, width, n_rows):
    per = tm // HALO
    last = n_rows // HALO - 1
    return [
        pl.BlockSpec((tm, width), lambda i: (i, 0)),
        pl.BlockSpec((HALO, width), lambda i: (jnp.maximum(i * per - 1, 0), 0)),
        pl.BlockSpec((HALO, width), lambda i: (jnp.minimum((i + 1) * per, last), 0)),
    ]


def _assemble(dst, main, prev, nxt, tm):
    dst[0:HALO, :] = prev[...]
    dst[HALO:HALO + tm, :] = main[...]
    dst[HALO + tm:, :] = nxt[...]


def _store_rows(he_s, h_prev, h_main, h_next, tm, tpb):
    t = pl.program_id(0) % tpb
    keep_prev = jnp.where(t != 0, 1.0, 0.0)
    keep_next = jnp.where(t != tpb - 1, 1.0, 0.0)
    he_s[0:HALO, :] = (h_prev * keep_prev).astype(BF16)
    he_s[HALO:HALO + tm, :] = h_main.astype(BF16)
    he_s[HALO + tm:, :] = (h_next * keep_next).astype(BF16)


def _store_mod_rows(he_s, xm, xp, xn, nw, shift, scale, tm, tpb):
    _store_rows(he_s, _rms_mod(xp[...], nw, shift, scale), _rms_mod(xm[...], nw, shift, scale),
                _rms_mod(xn[...], nw, shift, scale), tm, tpb)


def _conv3(a_ext, cw, cb, tm):
    rows = tm + 2 * HALO
    prev = pltpu.roll(a_ext, 1, 0)[HALO:HALO + tm]
    nxt = pltpu.roll(a_ext, rows - 1, 0)[HALO:HALO + tm]
    cur = a_ext[HALO:HALO + tm]
    return prev * cw[0:1] + cur * cw[1:2] + nxt * cw[2:3] + cb


def _gelu_exact(x):
    return 0.5 * x * (1.0 + lax.erf(x * np.float32(np.sqrt(0.5))))


def _convglu(he_s, act_s, wup_ref, cw_ref, cb_ref, wdown_ref, tm):
    d_ff = wdown_ref.shape[0]
    nchunk = d_ff // FF_CHUNK

    def cols(j, base=0):
        return slice(base + j * FF_CHUNK, base + (j + 1) * FF_CHUNK)

    def up(j):
        a = jnp.dot(he_s[...], wup_ref[:, cols(j)], preferred_element_type=F32)
        val = jnp.dot(he_s[HALO:HALO + tm, :], wup_ref[:, cols(j, d_ff)], preferred_element_type=F32)
        return a, val

    nxt = up(0)
    for j in range(nchunk):
        a, val = nxt
        if j + 1 < nchunk:
            nxt = up(j + 1)
        cv = _conv3(a, cw_ref[:, cols(j)], cb_ref[:, cols(j)], tm)
        act_s[:, cols(j)] = (_gelu_exact(cv) * val).astype(BF16)
    return jnp.dot(act_s[...], wdown_ref[...], preferred_element_type=F32)


def _ffn0_kernel(xm, xp, xn, am, ap, an, gm, gp, gn, gate_m, shift_f, scale_f, gate_f, nw,
                 wo_ref, wup_ref, cw_ref, cb_ref, wdown_ref, o_ref,
                 xe_s, ae_s, ge_s, he_s, act_s, *, tm, tpb):
    b = pl.program_id(0) // tpb
    _assemble(xe_s, xm, xp, xn, tm)
    _assemble(ae_s, am, ap, an, tm)
    _assemble(ge_s, gm, gp, gn, tm)
    y = (jnp.dot(ae_s[...], wo_ref[0], preferred_element_type=F32)
         + jnp.dot(ge_s[...], wo_ref[1], preferred_element_type=F32))
    x1 = xe_s[...] + gate_m[pl.ds(b, 1), :] * y
    xe_s[...] = x1
    h = _rms_mod(x1, nw[...], shift_f[pl.ds(b, 1), :], scale_f[pl.ds(b, 1), :])
    _store_rows(he_s, h[0:HALO], h[HALO:HALO + tm], h[HALO + tm:], tm, tpb)
    y2 = _convglu(he_s, act_s, wup_ref, cw_ref, cb_ref, wdown_ref, tm)
    o_ref[...] = xe_s[HALO:HALO + tm, :] + gate_f[pl.ds(b, 1), :] * y2


def _ffn1_kernel(xm, xp, xn, shift_f, scale_f, gate_f, nw, fnw,
                 wup_ref, cw_ref, cb_ref, wdown_ref, o_ref,
                 he_s, act_s, *, tm, tpb):
    b = pl.program_id(0) // tpb
    _store_mod_rows(he_s, xm, xp, xn, nw[...], shift_f[pl.ds(b, 1), :], scale_f[pl.ds(b, 1), :], tm, tpb)
    y2 = _convglu(he_s, act_s, wup_ref, cw_ref, cb_ref, wdown_ref, tm)
    x2 = xm[...] + gate_f[pl.ds(b, 1), :] * y2
    ms = jnp.mean(x2 * x2, axis=-1, keepdims=True)
    o_ref[...] = x2 * lax.rsqrt(ms + EPS) * fnw[...]


def _mod_spec(k):
    return pl.BlockSpec((8, D_MODEL), lambda i: (0, k))


def _const_spec(shape):
    nd = len(shape)
    return pl.BlockSpec(shape, lambda i: (0,) * nd, pipeline_mode=pl.Buffered(1))


def _ffn_weight_specs(d_ff):
    return [
        _const_spec((D_MODEL, 2 * d_ff)),
        _const_spec((3, d_ff)),
        _const_spec((1, d_ff)),
        _const_spec((d_ff, D_MODEL)),
    ]


def _ffn0(x2d, a, g, mod, nw, wo, wup, cw, cb, wdown, *, tm, tpb):
    n_rows = x2d.shape[0]
    d_ff = wdown.shape[0]
    ext = tm + 2 * HALO
    kern = functools.partial(_ffn0_kernel, tm=tm, tpb=tpb)
    return pl.pallas_call(
        kern,
        grid=(n_rows // tm,),
        in_specs=(_halo_specs(tm, D_MODEL, n_rows) + _halo_specs(tm, NA_WIDTH, n_rows)
                  + _halo_specs(tm, HG_WIDTH, n_rows)
                  + [_mod_spec(2), _mod_spec(3), _mod_spec(4), _mod_spec(5),
                     _const_spec((1, D_MODEL)), _const_spec((2, NA_WIDTH, D_MODEL))]
                  + _ffn_weight_specs(d_ff)),
        out_specs=pl.BlockSpec((tm, D_MODEL), lambda i: (i, 0)),
        out_shape=jax.ShapeDtypeStruct((n_rows, D_MODEL), F32),
        scratch_shapes=[pltpu.VMEM((ext, D_MODEL), F32), pltpu.VMEM((ext, NA_WIDTH), BF16),
                        pltpu.VMEM((ext, HG_WIDTH), BF16), pltpu.VMEM((ext, D_MODEL), BF16),
                        pltpu.VMEM((tm, d_ff), BF16)],
        compiler_params=_cparams(("arbitrary",)),
        name="ffn0",
    )(x2d, x2d, x2d, a, a, a, g, g, g, mod, mod, mod, mod, nw, wo, wup, cw, cb, wdown)


def _ffn1(x2d, mod, nw, fnw, wup, cw, cb, wdown, *, tm, tpb):
    n_rows = x2d.shape[0]
    d_ff = wdown.shape[0]
    ext = tm + 2 * HALO
    kern = functools.partial(_ffn1_kernel, tm=tm, tpb=tpb)
    return pl.pallas_call(
        kern,
        grid=(n_rows // tm,),
        in_specs=(_halo_specs(tm, D_MODEL, n_rows)
                  + [_mod_spec(3), _mod_spec(4), _mod_spec(5),
                     _const_spec((1, D_MODEL)), _const_spec((1, D_MODEL))]
                  + _ffn_weight_specs(d_ff)),
        out_specs=pl.BlockSpec((tm, D_MODEL), lambda i: (i, 0)),
        out_shape=jax.ShapeDtypeStruct((n_rows, D_MODEL), F32),
        scratch_shapes=[pltpu.VMEM((ext, D_MODEL), BF16), pltpu.VMEM((tm, d_ff), BF16)],
        compiler_params=_cparams(("arbitrary",)),
        name="ffn1",
    )(x2d, x2d, x2d, mod, mod, mod, nw, fnw, wup, cw, cb, wdown)


def _mix_kernel(xm, xp, xn, shift_m, scale_m, gate_m, nw, win_ref, cw_ref, cb_ref, wout_ref, o_ref,
                he_s, z_s, *, tm, tpb):
    b = pl.program_id(0) // tpb
    width = wout_ref.shape[0]
    nchunk = width // FF_CHUNK
    _store_mod_rows(he_s, xm, xp, xn, nw[...], shift_m[pl.ds(b, 1), :], scale_m[pl.ds(b, 1), :], tm, tpb)

    def cols(j, base=0):
        return slice(base + j * FF_CHUNK, base + (j + 1) * FF_CHUNK)

    def up(j):
        gb = jnp.dot(he_s[HALO:HALO + tm, :], win_ref[:, cols(j)], preferred_element_type=F32)
        gc = jnp.dot(he_s[...], win_ref[:, cols(j, width)], preferred_element_type=F32)
        u = jnp.dot(he_s[...], win_ref[:, cols(j, 2 * width)], preferred_element_type=F32)
        return gb, gc, u

    nxt = up(0)
    for j in range(nchunk):
        gb, gc, u = nxt
        if j + 1 < nchunk:
            nxt = up(j + 1)
        z_s[:, cols(j)] = (gb * _conv3(gc * u, cw_ref[:, cols(j)], cb_ref[:, cols(j)], tm)).astype(BF16)
    y = jnp.dot(z_s[...], wout_ref[...], preferred_element_type=F32)
    o_ref[...] = xm[...] + gate_m[pl.ds(b, 1), :] * y


def _mix(x2d, mod, nw, win, cw, cb, wout, *, tm, tpb):
    n_rows = x2d.shape[0]
    width = wout.shape[0]
    ext = tm + 2 * HALO
    kern = functools.partial(_mix_kernel, tm=tm, tpb=tpb)
    return pl.pallas_call(
        kern,
        grid=(n_rows // tm,),
        in_specs=(_halo_specs(tm, D_MODEL, n_rows)
                  + [_mod_spec(0), _mod_spec(1), _mod_spec(2), _const_spec((1, D_MODEL)),
                     _const_spec((D_MODEL, 3 * width)), _const_spec((3, width)),
                     _const_spec((1, width)), _const_spec((width, D_MODEL))]),
        out_specs=pl.BlockSpec((tm, D_MODEL), lambda i: (i, 0)),
        out_shape=jax.ShapeDtypeStruct((n_rows, D_MODEL), F32),
        scratch_shapes=[pltpu.VMEM((ext, D_MODEL), BF16), pltpu.VMEM((tm, width), BF16)],
        compiler_params=_cparams(("arbitrary",)),
        name="mix",
    )(x2d, x2d, x2d, mod, mod, mod, nw, win, cw, cb, wout)


def kernel(x, c, ctx, c_ctx, ada_w, ada_b, norm_mix_w, norm_ffn_w, ev_w_in, ev_w_out, na_rpb, hg_lb_logits,
           hg_norm_w, od_w_in, od_conv_w, od_conv_b, od_w_out, ffn_w_up, ffn_conv_w, ffn_conv_b, ffn_w_down,
           final_norm_w):
    batch, s_len, d = x.shape
    t_len = ctx.shape[1]
    n_rows = batch * s_len
    tpb = s_len // TM
    x2d = x.reshape(n_rows, d)
    ctx2d = ctx.reshape(batch * t_len, d)

    c_stack = jnp.concatenate([c, c_ctx[None], jnp.zeros((8 - batch - 1, d), F32)], axis=0)
    mods = _ada(c_stack, ada_w, ada_b, batch + 1)

    w_in0 = ev_w_in[0].at[:, :NA_WIDTH].multiply(LOG2E * NA_HEAD_DIM ** -0.5).astype(BF16)
    nw_mix = norm_mix_w.reshape(-1, 1, d)
    nw_ffn = norm_ffn_w.reshape(-1, 1, d)
    p_lat = _proj(x2d, nw_mix[0], mods[0], 0, 1, w_in0, tm=TM, tpb=tpb, fixed_row=None)
    p_ctx = _proj(ctx2d, nw_mix[0], mods[0], 0, 1, w_in0, tm=t_len, tpb=1, fixed_row=batch)
    a_lat = _na(p_lat, p_ctx, _na_band_tables(na_rpb[0]), batch=batch, s_len=s_len, t_len=t_len)
    g_lat = _hgrn(p_lat, p_ctx, hg_lb_logits, hg_norm_w[0].reshape(1, HG_WIDTH),
                  batch=batch, s_len=s_len, t_len=t_len)
    wo0 = ev_w_out[0].astype(BF16).reshape(2, NA_WIDTH, d)
    x1 = _ffn0(x2d, a_lat, g_lat, mods[0], nw_ffn[0], wo0,
               ffn_w_up[0].astype(BF16), ffn_conv_w[0], ffn_conv_b[0][None],
               ffn_w_down[0].astype(BF16), tm=TM_FUSED, tpb=s_len // TM_FUSED)

    x2 = _mix(x1, mods[1], nw_mix[1], od_w_in[0].astype(BF16), od_conv_w[0],
              od_conv_b[0][None], od_w_out[0].astype(BF16), tm=TM, tpb=tpb)
    out = _ffn1(x2, mods[1], nw_ffn[1], final_norm_w.reshape(1, d),
                ffn_w_up[1].astype(BF16), ffn_conv_w[1], ffn_conv_b[1][None],
                ffn_w_down[1].astype(BF16), tm=TM, tpb=tpb)
    return out.reshape(batch, s_len, d)
```

```python
import functools

import numpy as np
import jax
import jax.numpy as jnp
from jax import lax
from jax.experimental import pallas as pl
from jax.experimental.pallas import tpu as pltpu

F32 = jnp.float32
BF16 = jnp.bfloat16
HIGHEST = lax.Precision.HIGHEST

D_MODEL = 1024
GRID_W = 64
EPS = 1e-6
NA_HEADS = 8
NA_HEAD_DIM = 64
NA_WIDTH = NA_HEADS * NA_HEAD_DIM
NA_KH = 8
NA_KW = 16
HG_HEADS = 4
HG_DIM = 128
HG_WIDTH = HG_HEADS * HG_DIM
D_FF = 2816
EV_IN = 4096

V7X_VMEM_BYTES = 64 * 1024 * 1024
VMEM_LIMIT = V7X_VMEM_BYTES - 8 * 1024 * 1024
LANES = 128
HALO = 16
NEG = -1e30
LOG2E = float(np.log2(np.e))

NA_QROWS = 4
NA_KROWS = NA_QROWS + NA_KH
NA_UNROLL = 2
HG_CHUNK = 64
HG_UNROLL = 8
FF_CHUNK = 256
PROJ_HQ = 3
PROJ_FF = 6
TM = 1024
TM_FUSED = 512


def _cparams(sem):
    return pltpu.CompilerParams(dimension_semantics=sem, vmem_limit_bytes=VMEM_LIMIT)


def _rms_mod(x, nw, shift, scale):
    ms = jnp.mean(x * x, axis=-1, keepdims=True)
    return (x * lax.rsqrt(ms + EPS)) * (nw * (1.0 + scale)) + shift


def _ada_kernel(ct_ref, w_ref, b_ref, o_ref, *, n_cond):
    s = jax.nn.silu(ct_ref[...])
    o_ref[0] = jnp.zeros(o_ref.shape[1:], F32)
    for r in range(n_cond):
        o_ref[0, r:r + 1, :] = jnp.sum(s[:, r:r + 1] * w_ref[0], axis=0, keepdims=True) + b_ref[0]


def _ada(c_stack, ada_w, ada_b, n_cond):
    depth, d, n6 = ada_w.shape
    tn = 1536
    return pl.pallas_call(
        functools.partial(_ada_kernel, n_cond=n_cond),
        grid=(depth, n6 // tn),
        in_specs=[
            pl.BlockSpec((d, 8), lambda l, j: (0, 0)),
            pl.BlockSpec((1, d, tn), lambda l, j: (l, 0, j)),
            pl.BlockSpec((1, 1, tn), lambda l, j: (l, 0, j)),
        ],
        out_specs=pl.BlockSpec((1, 8, tn), lambda l, j: (l, 0, j)),
        out_shape=jax.ShapeDtypeStruct((depth, 8, n6), F32),
        compiler_params=_cparams(("arbitrary", "arbitrary")),
        name="ada",
    )(c_stack.T, ada_w, ada_b.reshape(depth, 1, n6))


def _forget_bounds(lbl_ref):
    ll = lbl_ref[...]
    mx = jnp.maximum(ll[0], ll[1])
    e0 = jnp.exp(ll[0] - mx)
    lb = e0 / (e0 + jnp.exp(ll[1] - mx))
    return 0.5 * (1.0 + lb), 0.5 * (1.0 - lb)


def _proj_kernel(x_ref, nw_ref, shift_ref, scale_ref, w_ref, lbl_ref, o_ref, lf_ref, h_s, *, tpb, fixed_row):
    if fixed_row is None:
        row = pl.program_id(0) // tpb
    else:
        row = fixed_row
    h = _rms_mod(x_ref[...], nw_ref[...], shift_ref[pl.ds(row, 1), :], scale_ref[pl.ds(row, 1), :])
    h_s[...] = h.astype(BF16)
    f_mid, f_amp = _forget_bounds(lbl_ref)
    ncol = HG_WIDTH
    nblk = w_ref.shape[1] // ncol
    for blk in sorted(range(nblk), key=lambda k: (k < PROJ_FF and k != PROJ_HQ, k)):
        cols = slice(blk * ncol, (blk + 1) * ncol)
        y = jnp.dot(h_s[...], w_ref[:, cols], preferred_element_type=F32)
        if blk == PROJ_HQ:
            half = 0.5 * y
            y = half + half * jnp.tanh(half)
        if blk < PROJ_FF:
            o_ref[:, cols] = y.astype(o_ref.dtype)
        else:
            d = blk - PROJ_FF
            fg = f_mid[d:d + 1] + f_amp[d:d + 1] * jnp.tanh(0.5 * y)
            lf_ref[:, d * ncol:(d + 1) * ncol] = jnp.log(fg) * np.float32(1.0 / np.log(2.0))


def _proj(x2d, nw, mod, shift_blk, scale_blk, w, lb_logits, *, tm, tpb, fixed_row):
    n_rows, d = x2d.shape
    n = w.shape[1]
    n_bf = PROJ_FF * HG_WIDTH
    kern = functools.partial(_proj_kernel, tpb=tpb, fixed_row=fixed_row)
    return pl.pallas_call(
        kern,
        grid=(n_rows // tm,),
        in_specs=[
            pl.BlockSpec((tm, d), lambda i: (i, 0)),
            pl.BlockSpec((1, d), lambda i: (0, 0)),
            pl.BlockSpec((8, d), lambda i: (0, shift_blk)),
            pl.BlockSpec((8, d), lambda i: (0, scale_blk)),
            pl.BlockSpec((d, n), lambda i: (0, 0), pipeline_mode=pl.Buffered(1)),
            pl.BlockSpec(lb_logits.shape, lambda i: (0, 0, 0)),
        ],
        out_specs=[pl.BlockSpec((tm, n_bf), lambda i: (i, 0)),
                   pl.BlockSpec((tm, n - n_bf), lambda i: (i, 0))],
        out_shape=[jax.ShapeDtypeStruct((n_rows, n_bf), BF16),
                   jax.ShapeDtypeStruct((n_rows, n - n_bf), F32)],
        scratch_shapes=[pltpu.VMEM((tm, d), BF16)],
        compiler_params=_cparams(("arbitrary",)),
        name="proj",
    )(x2d, nw, mod, mod, w, lb_logits)


def _na_bias_rows(rpb):
    h, nr, nc = rpb.shape
    return jnp.pad(rpb.astype(F32) * LOG2E, ((0, 0), (0, 2 * NA_KH - nr), (0, LANES - nc)))


def _na_build_bands(rows_ref, band_ref):
    qc = lax.broadcasted_iota(jnp.int32, (GRID_W, LANES), 0)
    lane = lax.broadcasted_iota(jnp.int32, (GRID_W, LANES), 1)
    kc = jnp.where(lane < GRID_W, lane, lane - GRID_W)
    start = jnp.clip(qc - NA_KW // 2, 0, GRID_W - NA_KW)
    valid = (kc >= start) & (kc < start + NA_KW)
    masked = jnp.full((GRID_W, LANES), NEG, F32)
    for hh in range(band_ref.shape[0]):
        for a in range(2 * NA_KH - 1):
            x = jnp.broadcast_to(rows_ref[hh, a:a + 1, :], (GRID_W, LANES))
            lo = pltpu.roll(x, LANES - (NA_KW - 1), 1, stride=1, stride_axis=0)
            hi = pltpu.roll(x, LANES - (NA_KW - 1) - GRID_W, 1, stride=1, stride_axis=0)
            band_ref[hh, a] = jnp.where(valid, jnp.where(lane < GRID_W, lo, hi), masked)
        band_ref[hh, 2 * NA_KH - 1] = masked


def _na_block_index(rows):
    masked = 2 * NA_KH - 1

    def block_idx(rb):
        kr0 = int(np.clip(NA_QROWS * rb - NA_KH // 2, 0, rows - NA_KROWS))
        idx = np.empty((NA_QROWS, NA_KROWS), np.int32)
        for qi in range(NA_QROWS):
            qr = NA_QROWS * rb + qi
            r0 = int(np.clip(qr - NA_KH // 2, 0, rows - NA_KH))
            for kj in range(NA_KROWS):
                kr = kr0 + kj
                idx[qi, kj] = kr - qr + (NA_KH - 1) if r0 <= kr < r0 + NA_KH else masked
        return idx

    nblk = rows // NA_QROWS
    interior = block_idx(1)
    assert all(np.array_equal(block_idx(rb), interior) for rb in range(1, nblk - 1))
    return np.stack([block_idx(0), interior, block_idx(nblk - 1)])


def _na_kernel(q_ref, k_ref, v_ref, kc_ref, vc_ref, rows_ref, o_ref, bias_ref, band_ref):
    s_len = q_ref.shape[0]
    qb = NA_QROWS * GRID_W
    kb = NA_KROWS * GRID_W
    nblk = s_len // qb
    rows = s_len // GRID_W
    _na_build_bands(rows_ref, band_ref)

    idx = _na_block_index(rows)
    left = lax.broadcasted_iota(jnp.int32, (GRID_W, LANES), 1) < GRID_W
    for hh in range(2):
        for t in range(3):
            for qi in range(NA_QROWS):
                for j in range(NA_KROWS // 2):
                    a, b = int(idx[t, qi, 2 * j]), int(idx[t, qi, 2 * j + 1])
                    tile = band_ref[hh, a] if a == b else jnp.where(left, band_ref[hh, a], band_ref[hh, b])
                    bias_ref[hh, t, qi * GRID_W:(qi + 1) * GRID_W, j * LANES:(j + 1) * LANES] = tile

    lane = lax.broadcasted_iota(jnp.int32, (qb, LANES), 1)
    in_head = [(lane >= NA_HEAD_DIM * hh) & (lane < NA_HEAD_DIM * (hh + 1)) for hh in range(2)]
    dn_t = (((1,), (1,)), ((), ()))

    def head_values(v, hh):
        return jnp.concatenate([jnp.where(in_head[hh], v[r:r + qb], jnp.ones_like(v[r:r + qb]))
                                for r in range(0, v.shape[0], qb)], axis=0)

    kc = kc_ref[...]
    vc = [head_values(vc_ref[...], hh) for hh in range(2)]

    def body(it, carry):
        blocks = []
        for u in range(NA_UNROLL):
            rb = it * NA_UNROLL + u
            kr0 = jnp.clip(NA_QROWS * rb - NA_KH // 2, 0, rows - NA_KROWS)
            ks = pl.multiple_of(kr0 * GRID_W, qb)
            qs = pl.multiple_of(rb * qb, qb)
            btype = jnp.where(rb == 0, 0, jnp.where(rb == nblk - 1, 2, 1))
            blocks.append((qs, k_ref[pl.ds(ks, kb), :], v_ref[pl.ds(ks, kb), :], q_ref[pl.ds(qs, qb), :], btype))
        logits = []
        for qs, kw, vw, q, btype in blocks:
            q2 = jnp.concatenate([jnp.where(in_head[hh], q, jnp.zeros_like(q)) for hh in range(2)], axis=0)
            s_loc = lax.dot_general(q2, kw, dn_t, preferred_element_type=F32)
            s_ctx = lax.dot_general(q2, kc, dn_t, preferred_element_type=F32)
            for hh in range(2):
                logits.append((s_loc[hh * qb:(hh + 1) * qb] + bias_ref[hh, btype], s_ctx[hh * qb:(hh + 1) * qb]))
        probs = []
        for s_loc, s_ctx in logits:
            m = jnp.maximum(jnp.max(s_loc, axis=-1, keepdims=True), jnp.max(s_ctx, axis=-1, keepdims=True))
            probs.append((jnp.exp2(s_loc - m).astype(BF16), jnp.exp2(s_ctx - m).astype(BF16)))
        outs = []
        for i, (p_loc, p_ctx) in enumerate(probs):
            hh = i % 2
            o = (jnp.dot(p_loc, head_values(blocks[i // 2][2], hh), preferred_element_type=F32)
                 + jnp.dot(p_ctx, vc[hh], preferred_element_type=F32))
            outs.append(o / pltpu.roll(o, NA_HEAD_DIM, 1))
        for u, blk in enumerate(blocks):
            o_ref[pl.ds(blk[0], qb), :] = jnp.where(in_head[0], outs[2 * u], outs[2 * u + 1]).astype(o_ref.dtype)
        return carry

    lax.fori_loop(0, nblk // NA_UNROLL, body, 0)


def _na(p_lat, p_ctx, band, *, batch, s_len, t_len):
    hp = NA_HEADS // 2
    kblk = NA_WIDTH // LANES
    qb = NA_QROWS * GRID_W
    kbk = NA_KROWS * GRID_W
    return pl.pallas_call(
        _na_kernel,
        grid=(batch, hp),
        in_specs=[
            pl.BlockSpec((s_len, LANES), lambda b, h: (b, h)),
            pl.BlockSpec((s_len, LANES), lambda b, h: (b, kblk + h)),
            pl.BlockSpec((s_len, LANES), lambda b, h: (b, 2 * kblk + h)),
            pl.BlockSpec((t_len, LANES), lambda b, h: (b, kblk + h)),
            pl.BlockSpec((t_len, LANES), lambda b, h: (b, 2 * kblk + h)),
            pl.BlockSpec((2, 2 * NA_KH, LANES), lambda b, h: (h, 0, 0)),
        ],
        out_specs=pl.BlockSpec((s_len, LANES), lambda b, h: (b, h)),
        out_shape=jax.ShapeDtypeStruct((batch * s_len, NA_WIDTH), BF16),
        scratch_shapes=[pltpu.VMEM((2, 3, qb, kbk), F32), pltpu.VMEM((2, 2 * NA_KH, GRID_W, LANES), F32)],
        compiler_params=_cparams(("arbitrary", "arbitrary")),
        name="na",
    )(p_lat, p_lat, p_lat, p_ctx, p_ctx, band)


def _hg_kernel(q_ref, ff_ref, fb_ref, i_ref, g_ref, cff_ref, cfb_ref, ci_ref, nw_ref,
               o_ref, acc_ref):
    c = HG_CHUNK
    s_len = q_ref.shape[0]
    t_len = ci_ref.shape[0]
    n = s_len // c
    dn_t = (((1,), (1,)), ((), ()))

    def gates(lf):
        return 1.0 - jnp.exp2(lf), lf

    r_t = lax.broadcasted_iota(jnp.int32, (t_len, t_len), 0)
    c_t = lax.broadcasted_iota(jnp.int32, (t_len, t_len), 1)
    vct = ci_ref[...].astype(F32).T.astype(BF16)
    states = []
    for d, (fref, after) in enumerate(((cff_ref, c_t > r_t), (cfb_ref, c_t < r_t))):
        kk, lf = gates(fref[...])
        rest = jnp.dot(after.astype(F32), lf, precision=HIGHEST, preferred_element_type=F32)
        kd = (kk * jnp.exp2(rest)).astype(BF16)
        states.append(jnp.dot(vct, kd, preferred_element_type=F32))

    r_c = lax.broadcasted_iota(jnp.int32, (c, c), 0)
    c_c = lax.broadcasted_iota(jnp.int32, (c, c), 1)
    tri = (c_c <= r_c, c_c >= r_c)
    tri_b = tuple(t.astype(BF16) for t in tri)

    def cum(d, lf):
        hi = lf.astype(BF16)
        lo = (lf - hi.astype(F32)).astype(BF16)
        r = jnp.dot(tri_b[d], jnp.concatenate([hi, lo], axis=1), preferred_element_type=F32)
        return r[:, :LANES] + r[:, LANES:]

    acc_ref[...] = jnp.zeros_like(acc_ref)

    def body(jj, carry):
        chains = []
        for u in range(HG_UNROLL):
            j = jj * HG_UNROLL + u
            chains.append((0, pl.multiple_of(j * c, c)))
            chains.append((1, pl.multiple_of((n - 1 - j) * c, c)))
        pre = []
        for d, start in chains:
            rows = pl.ds(start, c)
            kk, lf = gates((ff_ref, fb_ref)[d][rows, :])
            pre.append((kk, lf, q_ref[rows, :].astype(F32), i_ref[rows, :]))
        bcs = [cum(d, p[1]) for (d, _), p in zip(chains, pre)]
        ops = []
        for (d, _), (kk, lf, qh, v), bc in zip(chains, pre, bcs):
            btot = bc[c - 1:c] if d == 0 else bc[0:1]
            mref = bc[c // 2:c // 2 + 1]
            qm = qh * jnp.exp2(bc - mref)
            km = kk * jnp.exp2(mref - bc)
            qe = (qm * jnp.exp2(mref)).astype(BF16)
            kd = (km * jnp.exp2(btot - mref)).astype(BF16)
            ops.append((qe, qm.astype(BF16), km.astype(BF16), kd, jnp.exp2(btot), v))
        atts = [jnp.where(tri[d], lax.dot_general(qm, km, dn_t, preferred_element_type=F32), 0.0).astype(BF16)
                for (d, _), (qe, qm, km, kd, eb, v) in zip(chains, ops)]
        upds = [lax.dot_general(v, kd, (((0,), (0,)), ((), ())), preferred_element_type=F32)
                for (qe, qm, km, kd, eb, v) in ops]
        intra = [jnp.dot(att, o[5], preferred_element_type=F32) for att, o in zip(atts, ops)]
        st = list(carry)
        for idx, (d, start) in enumerate(chains):
            qe, eb = ops[idx][0], ops[idx][4]
            o = intra[idx] + lax.dot_general(qe, st[d].astype(BF16), dn_t, preferred_element_type=F32)
            st[d] = eb * st[d] + upds[idx]
            acc_ref[pl.ds(start, c), :] += o
        return st[0], st[1]

    lax.fori_loop(0, n // HG_UNROLL, body, (states[0], states[1]))

    rb = 512
    nw = nw_ref[...]

    def readout(j, carry):
        rows = pl.ds(pl.multiple_of(j * rb, rb), rb)
        o = acc_ref[rows, :]
        o = o * lax.rsqrt(jnp.mean(o * o, axis=-1, keepdims=True) + EPS) * nw
        o_ref[rows, :] = (o * jax.nn.silu(g_ref[rows, :].astype(F32))).astype(o_ref.dtype)
        return carry

    lax.fori_loop(0, s_len // rb, readout, 0)


def _hgrn(p_lat, lf_lat, p_ctx, lf_ctx, norm_w, *, batch, s_len, t_len):
    nh = HG_HEADS
    per = HG_WIDTH // LANES

    def blk(rows, group):
        return pl.BlockSpec((rows, LANES), lambda b, h: (b, group * per + h))

    return pl.pallas_call(
        _hg_kernel,
        grid=(batch, nh),
        in_specs=[blk(s_len, PROJ_HQ), blk(s_len, 0), blk(s_len, 1), blk(s_len, PROJ_HQ + 1),
                  blk(s_len, PROJ_HQ + 2), blk(t_len, 0), blk(t_len, 1), blk(t_len, PROJ_HQ + 1),
                  pl.BlockSpec((1, LANES), lambda b, h: (0, h))],
        out_specs=pl.BlockSpec((s_len, LANES), lambda b, h: (b, h)),
        out_shape=jax.ShapeDtypeStruct((batch * s_len, HG_WIDTH), BF16),
        scratch_shapes=[pltpu.VMEM((s_len, LANES), F32)],
        compiler_params=_cparams(("arbitrary", "arbitrary")),
        name="hgrn",
    )(p_lat, lf_lat, lf_lat, p_lat, p_lat, lf_ctx, lf_ctx, p_ctx, norm_w)


def _halo_specs(tm, width, n_rows):
    per = tm // HALO
    last = n_rows // HALO - 1
    return [
        pl.BlockSpec((tm, width), lambda i: (i, 0)),
        pl.BlockSpec((HALO, width), lambda i: (jnp.maximum(i * per - 1, 0), 0)),
        pl.BlockSpec((HALO, width), lambda i: (jnp.minimum((i + 1) * per, last), 0)),
    ]


def _assemble(dst, main, prev, nxt, tm):
    dst[0:HALO, :] = prev[...]
    dst[HALO:HALO + tm, :] = main[...]
    dst[HALO + tm:, :] = nxt[...]


def _store_rows(he_s, h_prev, h_main, h_next, tm, tpb):
    t = pl.program_id(0) % tpb
    keep_prev = jnp.where(t != 0, 1.0, 0.0)
    keep_next = jnp.where(t != tpb - 1, 1.0, 0.0)
    he_s[0:HALO, :] = (h_prev * keep_prev).astype(BF16)
    he_s[HALO:HALO + tm, :] = h_main.astype(BF16)
    he_s[HALO + tm:, :] = (h_next * keep_next).astype(BF16)


def _store_mod_rows(he_s, xm, xp, xn, nw, shift, scale, tm, tpb):
    _store_rows(he_s, _rms_mod(xp[...], nw, shift, scale), _rms_mod(xm[...], nw, shift, scale),
                _rms_mod(xn[...], nw, shift, scale), tm, tpb)


def _conv3(a_ext, cw, cb, tm):
    rows = tm + 2 * HALO
    prev = pltpu.roll(a_ext, 1, 0)[HALO:HALO + tm]
    nxt = pltpu.roll(a_ext, rows - 1, 0)[HALO:HALO + tm]
    cur = a_ext[HALO:HALO + tm]
    return prev * cw[0:1] + cur * cw[1:2] + nxt * cw[2:3] + cb


def _gelu_exact(x):
    return 0.5 * x * (1.0 + lax.erf(x * np.float32(np.sqrt(0.5))))


def _convglu(he_s, act_s, wup_ref, cw_ref, cb_ref, wdown_ref, tm):
    d_ff = wdown_ref.shape[0]
    nchunk = d_ff // FF_CHUNK

    def cols(j, base=0):
        return slice(base + j * FF_CHUNK, base + (j + 1) * FF_CHUNK)

    def up(j):
        a = jnp.dot(he_s[...], wup_ref[:, cols(j)], preferred_element_type=F32)
        val = jnp.dot(he_s[HALO:HALO + tm, :], wup_ref[:, cols(j, d_ff)], preferred_element_type=F32)
        return a, val

    nxt = up(0)
    for j in range(nchunk):
        a, val = nxt
        if j + 1 < nchunk:
            nxt = up(j + 1)
        cv = _conv3(a, cw_ref[:, cols(j)], cb_ref[:, cols(j)], tm)
        act_s[:, cols(j)] = (_gelu_exact(cv) * val).astype(BF16)
    return jnp.dot(act_s[...], wdown_ref[...], preferred_element_type=F32)


def _ffn0_kernel(xm, xp, xn, am, ap, an, gm, gp, gn, gate_m, shift_f, scale_f, gate_f, nw,
                 wo_ref, wup_ref, cw_ref, cb_ref, wdown_ref, o_ref,
                 xe_s, ae_s, ge_s, he_s, act_s, *, tm, tpb):
    b = pl.program_id(0) // tpb
    _assemble(xe_s, xm, xp, xn, tm)
    _assemble(ae_s, am, ap, an, tm)
    _assemble(ge_s, gm, gp, gn, tm)
    y = (jnp.dot(ae_s[...], wo_ref[0], preferred_element_type=F32)
         + jnp.dot(ge_s[...], wo_ref[1], preferred_element_type=F32))
    x1 = xe_s[...] + gate_m[pl.ds(b, 1), :] * y
    xe_s[...] = x1
    h = _rms_mod(x1, nw[...], shift_f[pl.ds(b, 1), :], scale_f[pl.ds(b, 1), :])
    _store_rows(he_s, h[0:HALO], h[HALO:HALO + tm], h[HALO + tm:], tm, tpb)
    y2 = _convglu(he_s, act_s, wup_ref, cw_ref, cb_ref, wdown_ref, tm)
    o_ref[...] = xe_s[HALO:HALO + tm, :] + gate_f[pl.ds(b, 1), :] * y2


def _ffn1_kernel(xm, xp, xn, shift_f, scale_f, gate_f, nw, fnw,
                 wup_ref, cw_ref, cb_ref, wdown_ref, o_ref,
                 he_s, act_s, *, tm, tpb):
    b = pl.program_id(0) // tpb
    _store_mod_rows(he_s, xm, xp, xn, nw[...], shift_f[pl.ds(b, 1), :], scale_f[pl.ds(b, 1), :], tm, tpb)
    y2 = _convglu(he_s, act_s, wup_ref, cw_ref, cb_ref, wdown_ref, tm)
    x2 = xm[...] + gate_f[pl.ds(b, 1), :] * y2
    ms = jnp.mean(x2 * x2, axis=-1, keepdims=True)
    o_ref[...] = x2 * lax.rsqrt(ms + EPS) * fnw[...]


def _mod_spec(k):
    return pl.BlockSpec((8, D_MODEL), lambda i: (0, k))


def _const_spec(shape):
    nd = len(shape)
    return pl.BlockSpec(shape, lambda i: (0,) * nd, pipeline_mode=pl.Buffered(1))


def _ffn_weight_specs(d_ff):
    return [
        _const_spec((D_MODEL, 2 * d_ff)),
        _const_spec((3, d_ff)),
        _const_spec((1, d_ff)),
        _const_spec((d_ff, D_MODEL)),
    ]


def _ffn0(x2d, a, g, mod, nw, wo, wup, cw, cb, wdown, *, tm, tpb):
    n_rows = x2d.shape[0]
    d_ff = wdown.shape[0]
    ext = tm + 2 * HALO
    kern = functools.partial(_ffn0_kernel, tm=tm, tpb=tpb)
    return pl.pallas_call(
        kern,
        grid=(n_rows // tm,),
        in_specs=(_halo_specs(tm, D_MODEL, n_rows) + _halo_specs(tm, NA_WIDTH, n_rows)
                  + _halo_specs(tm, HG_WIDTH, n_rows)
                  + [_mod_spec(2), _mod_spec(3), _mod_spec(4), _mod_spec(5),
                     _const_spec((1, D_MODEL)), _const_spec((2, NA_WIDTH, D_MODEL))]
                  + _ffn_weight_specs(d_ff)),
        out_specs=pl.BlockSpec((tm, D_MODEL), lambda i: (i, 0)),
        out_shape=jax.ShapeDtypeStruct((n_rows, D_MODEL), F32),
        scratch_shapes=[pltpu.VMEM((ext, D_MODEL), F32), pltpu.VMEM((ext, NA_WIDTH), BF16),
                        pltpu.VMEM((ext, HG_WIDTH), BF16), pltpu.VMEM((ext, D_MODEL), BF16),
                        pltpu.VMEM((tm, d_ff), BF16)],
        compiler_params=_cparams(("arbitrary",)),
        name="ffn0",
    )(x2d, x2d, x2d, a, a, a, g, g, g, mod, mod, mod, mod, nw, wo, wup, cw, cb, wdown)


def _ffn1(x2d, mod, nw, fnw, wup, cw, cb, wdown, *, tm, tpb):
    n_rows = x2d.shape[0]
    d_ff = wdown.shape[0]
    ext = tm + 2 * HALO
    kern = functools.partial(_ffn1_kernel, tm=tm, tpb=tpb)
    return pl.pallas_call(
        kern,
        grid=(n_rows // tm,),
        in_specs=(_halo_specs(tm, D_MODEL, n_rows)
                  + [_mod_spec(3), _mod_spec(4), _mod_spec(5),
                     _const_spec((1, D_MODEL)), _const_spec((1, D_MODEL))]
                  + _ffn_weight_specs(d_ff)),
        out_specs=pl.BlockSpec((tm, D_MODEL), lambda i: (i, 0)),
        out_shape=jax.ShapeDtypeStruct((n_rows, D_MODEL), F32),
        scratch_shapes=[pltpu.VMEM((ext, D_MODEL), BF16), pltpu.VMEM((tm, d_ff), BF16)],
        compiler_params=_cparams(("arbitrary",)),
        name="ffn1",
    )(x2d, x2d, x2d, mod, mod, mod, nw, fnw, wup, cw, cb, wdown)


def _mix_kernel(xm, xp, xn, shift_m, scale_m, gate_m, nw, win_ref, cw_ref, cb_ref, wout_ref, o_ref,
                he_s, z_s, *, tm, tpb):
    b = pl.program_id(0) // tpb
    width = wout_ref.shape[0]
    nchunk = width // FF_CHUNK
    _store_mod_rows(he_s, xm, xp, xn, nw[...], shift_m[pl.ds(b, 1), :], scale_m[pl.ds(b, 1), :], tm, tpb)

    def cols(j, base=0):
        return slice(base + j * FF_CHUNK, base + (j + 1) * FF_CHUNK)

    def up(j):
        gb = jnp.dot(he_s[HALO:HALO + tm, :], win_ref[:, cols(j)], preferred_element_type=F32)
        gc = jnp.dot(he_s[...], win_ref[:, cols(j, width)], preferred_element_type=F32)
        u = jnp.dot(he_s[...], win_ref[:, cols(j, 2 * width)], preferred_element_type=F32)
        return gb, gc, u

    nxt = up(0)
    for j in range(nchunk):
        gb, gc, u = nxt
        if j + 1 < nchunk:
            nxt = up(j + 1)
        z_s[:, cols(j)] = (gb * _conv3(gc * u, cw_ref[:, cols(j)], cb_ref[:, cols(j)], tm)).astype(BF16)
    y = jnp.dot(z_s[...], wout_ref[...], preferred_element_type=F32)
    o_ref[...] = xm[...] + gate_m[pl.ds(b, 1), :] * y


def _mix(x2d, mod, nw, win, cw, cb, wout, *, tm, tpb):
    n_rows = x2d.shape[0]
    width = wout.shape[0]
    ext = tm + 2 * HALO
    kern = functools.partial(_mix_kernel, tm=tm, tpb=tpb)
    return pl.pallas_call(
        kern,
        grid=(n_rows // tm,),
        in_specs=(_halo_specs(tm, D_MODEL, n_rows)
                  + [_mod_spec(0), _mod_spec(1), _mod_spec(2), _const_spec((1, D_MODEL)),
                     _const_spec((D_MODEL, 3 * width)), _const_spec((3, width)),
                     _const_spec((1, width)), _const_spec((width, D_MODEL))]),
        out_specs=pl.BlockSpec((tm, D_MODEL), lambda i: (i, 0)),
        out_shape=jax.ShapeDtypeStruct((n_rows, D_MODEL), F32),
        scratch_shapes=[pltpu.VMEM((ext, D_MODEL), BF16), pltpu.VMEM((tm, width), BF16)],
        compiler_params=_cparams(("arbitrary",)),
        name="mix",
    )(x2d, x2d, x2d, mod, mod, mod, nw, win, cw, cb, wout)


def kernel(x, c, ctx, c_ctx, ada_w, ada_b, norm_mix_w, norm_ffn_w, ev_w_in, ev_w_out, na_rpb, hg_lb_logits,
           hg_norm_w, od_w_in, od_conv_w, od_conv_b, od_w_out, ffn_w_up, ffn_conv_w, ffn_conv_b, ffn_w_down,
           final_norm_w):
    batch, s_len, d = x.shape
    t_len = ctx.shape[1]
    n_rows = batch * s_len
    tpb = s_len // TM
    x2d = x.reshape(n_rows, d)
    ctx2d = ctx.reshape(batch * t_len, d)

    c_stack = jnp.concatenate([c, c_ctx[None], jnp.zeros((8 - batch - 1, d), F32)], axis=0)
    mods = _ada(c_stack, ada_w, ada_b, batch + 1)

    w = ev_w_in[0]
    hg0 = 3 * NA_WIDTH
    w_in0 = jnp.concatenate([w[:, :NA_WIDTH] * (LOG2E * NA_HEAD_DIM ** -0.5), w[:, NA_WIDTH:hg0 + HG_WIDTH],
                             w[:, hg0 + 3 * HG_WIDTH:], w[:, hg0 + HG_WIDTH:hg0 + 3 * HG_WIDTH]],
                            axis=1).astype(BF16)
    nw_mix = norm_mix_w.reshape(-1, 1, d)
    nw_ffn = norm_ffn_w.reshape(-1, 1, d)
    p_lat, lf_lat = _proj(x2d, nw_mix[0], mods[0], 0, 1, w_in0, hg_lb_logits, tm=TM, tpb=tpb, fixed_row=None)
    p_ctx, lf_ctx = _proj(ctx2d, nw_mix[0], mods[0], 0, 1, w_in0, hg_lb_logits, tm=t_len, tpb=1,
                          fixed_row=batch)
    a_lat = _na(p_lat, p_ctx, _na_bias_rows(na_rpb[0]), batch=batch, s_len=s_len, t_len=t_len)
    g_lat = _hgrn(p_lat, lf_lat, p_ctx, lf_ctx, hg_norm_w[0].reshape(1, HG_WIDTH),
                  batch=batch, s_len=s_len, t_len=t_len)
    wo0 = ev_w_out[0].astype(BF16).reshape(2, NA_WIDTH, d)
    x1 = _ffn0(x2d, a_lat, g_lat, mods[0], nw_ffn[0], wo0,
               ffn_w_up[0].astype(BF16), ffn_conv_w[0], ffn_conv_b[0][None],
               ffn_w_down[0].astype(BF16), tm=TM_FUSED, tpb=s_len // TM_FUSED)

    x2 = _mix(x1, mods[1], nw_mix[1], od_w_in[0].astype(BF16), od_conv_w[0],
              od_conv_b[0][None], od_w_out[0].astype(BF16), tm=TM, tpb=tpb)
    out = _ffn1(x2, mods[1], nw_ffn[1], final_norm_w.reshape(1, d),
                ffn_w_up[1].astype(BF16), ffn_conv_w[1], ffn_conv_b[1][None],
                ffn_w_down[1].astype(BF16), tm=TM, tpb=tpb)
    return out.reshape(batch, s_len, d)
```

```python
import functools

import numpy as np
import jax
import jax.numpy as jnp
from jax import lax
from jax.experimental import pallas as pl
from jax.experimental.pallas import tpu as pltpu

F32 = jnp.float32
BF16 = jnp.bfloat16
HIGHEST = lax.Precision.HIGHEST

D_MODEL = 1024
GRID_W = 64
EPS = 1e-6
NA_HEADS = 8
NA_HEAD_DIM = 64
NA_WIDTH = NA_HEADS * NA_HEAD_DIM
NA_KH = 8
NA_KW = 16
HG_HEADS = 4
HG_DIM = 128
HG_WIDTH = HG_HEADS * HG_DIM
D_FF = 2816
EV_IN = 4096

V7X_VMEM_BYTES = 64 * 1024 * 1024
VMEM_LIMIT = V7X_VMEM_BYTES - 8 * 1024 * 1024
LANES = 128
HALO = 16
NEG = -1e30
LOG2E = float(np.log2(np.e))

NA_QROWS = 4
NA_KROWS = NA_QROWS + NA_KH
NA_UNROLL = 4
HG_CHUNK = 64
HG_UNROLL = 16
FF_CHUNK = 256
PROJ_HQ = 3
PROJ_FF = 6
TM = 1024
TM_FUSED = 1024


def _cparams(sem):
    return pltpu.CompilerParams(dimension_semantics=sem, vmem_limit_bytes=VMEM_LIMIT)


def _rms_mod(x, nw, shift, scale):
    ms = jnp.mean(x * x, axis=-1, keepdims=True)
    return (x * lax.rsqrt(ms + EPS)) * (nw * (1.0 + scale)) + shift


def _ada_kernel(ct_ref, w_ref, b_ref, o_ref, *, n_cond):
    s = jax.nn.silu(ct_ref[...])
    o_ref[0] = jnp.zeros(o_ref.shape[1:], F32)
    for r in range(n_cond):
        o_ref[0, r:r + 1, :] = jnp.sum(s[:, r:r + 1] * w_ref[0], axis=0, keepdims=True) + b_ref[0]


def _ada(c_stack, ada_w, ada_b, n_cond):
    depth, d, n6 = ada_w.shape
    tn = 1536
    return pl.pallas_call(
        functools.partial(_ada_kernel, n_cond=n_cond),
        grid=(depth, n6 // tn),
        in_specs=[
            pl.BlockSpec((d, 8), lambda l, j: (0, 0)),
            pl.BlockSpec((1, d, tn), lambda l, j: (l, 0, j)),
            pl.BlockSpec((1, 1, tn), lambda l, j: (l, 0, j)),
        ],
        out_specs=pl.BlockSpec((1, 8, tn), lambda l, j: (l, 0, j)),
        out_shape=jax.ShapeDtypeStruct((depth, 8, n6), F32),
        compiler_params=_cparams(("arbitrary", "arbitrary")),
        name="ada",
    )(c_stack.T, ada_w, ada_b.reshape(depth, 1, n6))


def _forget_bounds(lbl_ref):
    ll = lbl_ref[...]
    mx = jnp.maximum(ll[0], ll[1])
    e0 = jnp.exp(ll[0] - mx)
    lb = e0 / (e0 + jnp.exp(ll[1] - mx))
    return 0.5 * (1.0 + lb), 0.5 * (1.0 - lb)


def _proj_kernel(x_ref, nw_ref, shift_ref, scale_ref, w_ref, lbl_ref, o_ref, lf_ref, h_s, *, tpb, fixed_row):
    if fixed_row is None:
        row = pl.program_id(0) // tpb
    else:
        row = fixed_row
    h = _rms_mod(x_ref[...], nw_ref[...], shift_ref[pl.ds(row, 1), :], scale_ref[pl.ds(row, 1), :])
    h_s[...] = h.astype(BF16)
    f_mid, f_amp = _forget_bounds(lbl_ref)
    ncol = HG_WIDTH
    nblk = w_ref.shape[1] // ncol
    for blk in sorted(range(nblk), key=lambda k: (k < PROJ_FF and k != PROJ_HQ, k)):
        cols = slice(blk * ncol, (blk + 1) * ncol)
        y = jnp.dot(h_s[...], w_ref[:, cols], preferred_element_type=F32)
        if blk == PROJ_HQ:
            half = 0.5 * y
            y = half + half * jnp.tanh(half)
        if blk < PROJ_FF:
            o_ref[:, cols] = y.astype(o_ref.dtype)
        else:
            d = blk - PROJ_FF
            fg = f_mid[d:d + 1] + f_amp[d:d + 1] * jnp.tanh(0.5 * y)
            lf_ref[:, d * ncol:(d + 1) * ncol] = jnp.log(fg) * np.float32(1.0 / np.log(2.0))


def _proj(x2d, nw, mod, shift_blk, scale_blk, w, lb_logits, *, tm, tpb, fixed_row):
    n_rows, d = x2d.shape
    n = w.shape[1]
    n_bf = PROJ_FF * HG_WIDTH
    kern = functools.partial(_proj_kernel, tpb=tpb, fixed_row=fixed_row)
    return pl.pallas_call(
        kern,
        grid=(n_rows // tm,),
        in_specs=[
            pl.BlockSpec((tm, d), lambda i: (i, 0)),
            pl.BlockSpec((1, d), lambda i: (0, 0)),
            pl.BlockSpec((8, d), lambda i: (0, shift_blk)),
            pl.BlockSpec((8, d), lambda i: (0, scale_blk)),
            pl.BlockSpec((d, n), lambda i: (0, 0), pipeline_mode=pl.Buffered(1)),
            pl.BlockSpec(lb_logits.shape, lambda i: (0, 0, 0)),
        ],
        out_specs=[pl.BlockSpec((tm, n_bf), lambda i: (i, 0)),
                   pl.BlockSpec((tm, n - n_bf), lambda i: (i, 0))],
        out_shape=[jax.ShapeDtypeStruct((n_rows, n_bf), BF16),
                   jax.ShapeDtypeStruct((n_rows, n - n_bf), F32)],
        scratch_shapes=[pltpu.VMEM((tm, d), BF16)],
        compiler_params=_cparams(("arbitrary",)),
        name="proj",
    )(x2d, nw, mod, mod, w, lb_logits)


def _na_bias_rows(rpb):
    h, nr, nc = rpb.shape
    return jnp.pad(rpb.astype(F32) * LOG2E, ((0, 0), (0, 2 * NA_KH - nr), (0, LANES - nc)))


def _na_build_bands(rows_ref, band_ref):
    qc = lax.broadcasted_iota(jnp.int32, (GRID_W, LANES), 0)
    lane = lax.broadcasted_iota(jnp.int32, (GRID_W, LANES), 1)
    kc = jnp.where(lane < GRID_W, lane, lane - GRID_W)
    start = jnp.clip(qc - NA_KW // 2, 0, GRID_W - NA_KW)
    valid = (kc >= start) & (kc < start + NA_KW)
    masked = jnp.full((GRID_W, LANES), NEG, F32)
    for hh in range(band_ref.shape[0]):
        for a in range(2 * NA_KH - 1):
            x = jnp.broadcast_to(rows_ref[hh, a:a + 1, :], (GRID_W, LANES))
            lo = pltpu.roll(x, LANES - (NA_KW - 1), 1, stride=1, stride_axis=0)
            hi = pltpu.roll(x, LANES - (NA_KW - 1) - GRID_W, 1, stride=1, stride_axis=0)
            band_ref[hh, a] = jnp.where(valid, jnp.where(lane < GRID_W, lo, hi), masked)
        band_ref[hh, 2 * NA_KH - 1] = masked


def _na_block_index(rows):
    masked = 2 * NA_KH - 1

    def block_idx(rb):
        kr0 = int(np.clip(NA_QROWS * rb - NA_KH // 2, 0, rows - NA_KROWS))
        idx = np.empty((NA_QROWS, NA_KROWS), np.int32)
        for qi in range(NA_QROWS):
            qr = NA_QROWS * rb + qi
            r0 = int(np.clip(qr - NA_KH // 2, 0, rows - NA_KH))
            for kj in range(NA_KROWS):
                kr = kr0 + kj
                idx[qi, kj] = kr - qr + (NA_KH - 1) if r0 <= kr < r0 + NA_KH else masked
        return idx

    nblk = rows // NA_QROWS
    interior = block_idx(1)
    assert all(np.array_equal(block_idx(rb), interior) for rb in range(1, nblk - 1))
    return np.stack([block_idx(0), interior, block_idx(nblk - 1)])


def _na_kernel(q_ref, k_ref, v_ref, kc_ref, vc_ref, rows_ref, o_ref, bias_ref, band_ref):
    s_len = q_ref.shape[0]
    qb = NA_QROWS * GRID_W
    kb = NA_KROWS * GRID_W
    nblk = s_len // qb
    rows = s_len // GRID_W
    _na_build_bands(rows_ref, band_ref)

    idx = _na_block_index(rows)
    left = lax.broadcasted_iota(jnp.int32, (GRID_W, LANES), 1) < GRID_W
    for hh in range(2):
        for t in range(3):
            for qi in range(NA_QROWS):
                for j in range(NA_KROWS // 2):
                    a, b = int(idx[t, qi, 2 * j]), int(idx[t, qi, 2 * j + 1])
                    tile = band_ref[hh, a] if a == b else jnp.where(left, band_ref[hh, a], band_ref[hh, b])
                    bias_ref[hh, t, qi * GRID_W:(qi + 1) * GRID_W, j * LANES:(j + 1) * LANES] = tile

    lane = lax.broadcasted_iota(jnp.int32, (qb, LANES), 1)
    in_head = [(lane >= NA_HEAD_DIM * hh) & (lane < NA_HEAD_DIM * (hh + 1)) for hh in range(2)]
    dn_t = (((1,), (1,)), ((), ()))

    def head_values(v, hh):
        return jnp.concatenate([jnp.where(in_head[hh], v[r:r + qb], jnp.ones_like(v[r:r + qb]))
                                for r in range(0, v.shape[0], qb)], axis=0)

    kc = kc_ref[...]
    vc = [head_values(vc_ref[...], hh) for hh in range(2)]

    def body(it, carry):
        blocks = []
        for u in range(NA_UNROLL):
            rb = it * NA_UNROLL + u
            kr0 = jnp.clip(NA_QROWS * rb - NA_KH // 2, 0, rows - NA_KROWS)
            ks = pl.multiple_of(kr0 * GRID_W, qb)
            qs = pl.multiple_of(rb * qb, qb)
            btype = jnp.where(rb == 0, 0, jnp.where(rb == nblk - 1, 2, 1))
            blocks.append((qs, k_ref[pl.ds(ks, kb), :], v_ref[pl.ds(ks, kb), :], q_ref[pl.ds(qs, qb), :], btype))
        logits = []
        for qs, kw, vw, q, btype in blocks:
            q2 = jnp.concatenate([jnp.where(in_head[hh], q, jnp.zeros_like(q)) for hh in range(2)], axis=0)
            s_loc = lax.dot_general(q2, kw, dn_t, preferred_element_type=F32)
            s_ctx = lax.dot_general(q2, kc, dn_t, preferred_element_type=F32)
            for hh in range(2):
                logits.append((s_loc[hh * qb:(hh + 1) * qb] + bias_ref[hh, btype], s_ctx[hh * qb:(hh + 1) * qb]))
        probs = []
        for s_loc, s_ctx in logits:
            m = jnp.maximum(jnp.max(s_loc, axis=-1, keepdims=True), jnp.max(s_ctx, axis=-1, keepdims=True))
            probs.append((jnp.exp2(s_loc - m).astype(BF16), jnp.exp2(s_ctx - m).astype(BF16)))
        outs = []
        for i, (p_loc, p_ctx) in enumerate(probs):
            hh = i % 2
            o = (jnp.dot(p_loc, head_values(blocks[i // 2][2], hh), preferred_element_type=F32)
                 + jnp.dot(p_ctx, vc[hh], preferred_element_type=F32))
            outs.append(o / pltpu.roll(o, NA_HEAD_DIM, 1))
        for u, blk in enumerate(blocks):
            o_ref[pl.ds(blk[0], qb), :] = jnp.where(in_head[0], outs[2 * u], outs[2 * u + 1]).astype(o_ref.dtype)
        return carry

    lax.fori_loop(0, nblk // NA_UNROLL, body, 0)


def _na(p_lat, p_ctx, band, *, batch, s_len, t_len):
    hp = NA_HEADS // 2
    kblk = NA_WIDTH // LANES
    qb = NA_QROWS * GRID_W
    kbk = NA_KROWS * GRID_W
    return pl.pallas_call(
        _na_kernel,
        grid=(batch, hp),
        in_specs=[
            pl.BlockSpec((s_len, LANES), lambda b, h: (b, h)),
            pl.BlockSpec((s_len, LANES), lambda b, h: (b, kblk + h)),
            pl.BlockSpec((s_len, LANES), lambda b, h: (b, 2 * kblk + h)),
            pl.BlockSpec((t_len, LANES), lambda b, h: (b, kblk + h)),
            pl.BlockSpec((t_len, LANES), lambda b, h: (b, 2 * kblk + h)),
            pl.BlockSpec((2, 2 * NA_KH, LANES), lambda b, h: (h, 0, 0)),
        ],
        out_specs=pl.BlockSpec((s_len, LANES), lambda b, h: (b, h)),
        out_shape=jax.ShapeDtypeStruct((batch * s_len, NA_WIDTH), BF16),
        scratch_shapes=[pltpu.VMEM((2, 3, qb, kbk), F32), pltpu.VMEM((2, 2 * NA_KH, GRID_W, LANES), F32)],
        compiler_params=_cparams(("arbitrary", "arbitrary")),
        name="na",
    )(p_lat, p_lat, p_lat, p_ctx, p_ctx, band)


def _hg_kernel(q_ref, ff_ref, fb_ref, i_ref, g_ref, cff_ref, cfb_ref, ci_ref, nw_ref,
               o_ref, acc_ref):
    c = HG_CHUNK
    s_len = q_ref.shape[0]
    t_len = ci_ref.shape[0]
    n = s_len // c
    dn_t = (((1,), (1,)), ((), ()))

    def gates(lf):
        return 1.0 - jnp.exp2(lf), lf

    def masked_sum(mask, lf):
        hi = lf.astype(BF16)
        lo = (lf - hi.astype(F32)).astype(BF16)
        r = jnp.dot(mask, jnp.concatenate([hi, lo], axis=1), preferred_element_type=F32)
        return r[:, :LANES] + r[:, LANES:]

    r_t = lax.broadcasted_iota(jnp.int32, (t_len, t_len), 0)
    c_t = lax.broadcasted_iota(jnp.int32, (t_len, t_len), 1)
    vct = ci_ref[...].astype(F32).T.astype(BF16)
    states = []
    for fref, after in ((cff_ref, c_t > r_t), (cfb_ref, c_t < r_t)):
        kk, lf = gates(fref[...])
        kd = (kk * jnp.exp2(masked_sum(after.astype(BF16), lf))).astype(BF16)
        states.append(jnp.dot(vct, kd, preferred_element_type=F32))

    r_c = lax.broadcasted_iota(jnp.int32, (c, c), 0)
    c_c = lax.broadcasted_iota(jnp.int32, (c, c), 1)
    tri = (c_c <= r_c, c_c >= r_c)
    tri_b = tuple(t.astype(BF16) for t in tri)

    def cum(d, lf):
        return masked_sum(tri_b[d], lf)

    acc_ref[...] = jnp.zeros_like(acc_ref)

    def body(jj, carry):
        chains = []
        for u in range(HG_UNROLL):
            j = jj * HG_UNROLL + u
            chains.append((0, pl.multiple_of(j * c, c)))
            chains.append((1, pl.multiple_of((n - 1 - j) * c, c)))
        pre = []
        for d, start in chains:
            rows = pl.ds(start, c)
            kk, lf = gates((ff_ref, fb_ref)[d][rows, :])
            pre.append((kk, lf, q_ref[rows, :].astype(F32), i_ref[rows, :]))
        bcs = [cum(d, p[1]) for (d, _), p in zip(chains, pre)]
        ops = []
        for (d, _), (kk, lf, qh, v), bc in zip(chains, pre, bcs):
            btot = bc[c - 1:c] if d == 0 else bc[0:1]
            mref = bc[c // 2:c // 2 + 1]
            qm = qh * jnp.exp2(bc - mref)
            km = kk * jnp.exp2(mref - bc)
            qe = (qm * jnp.exp2(mref)).astype(BF16)
            kd = (km * jnp.exp2(btot - mref)).astype(BF16)
            ops.append((qe, qm.astype(BF16), km.astype(BF16), kd, jnp.exp2(btot), v))
        atts = [jnp.where(tri[d], lax.dot_general(qm, km, dn_t, preferred_element_type=F32), 0.0).astype(BF16)
                for (d, _), (qe, qm, km, kd, eb, v) in zip(chains, ops)]
        upds = [lax.dot_general(v, kd, (((0,), (0,)), ((), ())), preferred_element_type=F32)
                for (qe, qm, km, kd, eb, v) in ops]
        intra = [jnp.dot(att, o[5], preferred_element_type=F32) for att, o in zip(atts, ops)]
        st = list(carry)
        for idx, (d, start) in enumerate(chains):
            qe, eb = ops[idx][0], ops[idx][4]
            o = intra[idx] + lax.dot_general(qe, st[d].astype(BF16), dn_t, preferred_element_type=F32)
            st[d] = eb * st[d] + upds[idx]
            acc_ref[pl.ds(start, c), :] += o
        return st[0], st[1]

    lax.fori_loop(0, n // HG_UNROLL, body, (states[0], states[1]))

    rb = 512
    nw = nw_ref[...]

    def readout(j, carry):
        rows = pl.ds(pl.multiple_of(j * rb, rb), rb)
        o = acc_ref[rows, :]
        o = o * lax.rsqrt(jnp.mean(o * o, axis=-1, keepdims=True) + EPS) * nw
        o_ref[rows, :] = (o * jax.nn.silu(g_ref[rows, :].astype(F32))).astype(o_ref.dtype)
        return carry

    lax.fori_loop(0, s_len // rb, readout, 0)


def _hgrn(p_lat, lf_lat, p_ctx, lf_ctx, norm_w, *, batch, s_len, t_len):
    nh = HG_HEADS
    per = HG_WIDTH // LANES

    def blk(rows, group):
        return pl.BlockSpec((rows, LANES), lambda b, h: (b, group * per + h))

    return pl.pallas_call(
        _hg_kernel,
        grid=(batch, nh),
        in_specs=[blk(s_len, PROJ_HQ), blk(s_len, 0), blk(s_len, 1), blk(s_len, PROJ_HQ + 1),
                  blk(s_len, PROJ_HQ + 2), blk(t_len, 0), blk(t_len, 1), blk(t_len, PROJ_HQ + 1),
                  pl.BlockSpec((1, LANES), lambda b, h: (0, h))],
        out_specs=pl.BlockSpec((s_len, LANES), lambda b, h: (b, h)),
        out_shape=jax.ShapeDtypeStruct((batch * s_len, HG_WIDTH), BF16),
        scratch_shapes=[pltpu.VMEM((s_len, LANES), F32)],
        compiler_params=_cparams(("arbitrary", "arbitrary")),
        name="hgrn",
    )(p_lat, lf_lat, lf_lat, p_lat, p_lat, lf_ctx, lf_ctx, p_ctx, norm_w)


def _halo_specs(tm, width, n_rows):
    per = tm // HALO
    last = n_rows // HALO - 1
    return [
        pl.BlockSpec((tm, width), lambda i: (i, 0)),
        pl.BlockSpec((HALO, width), lambda i: (jnp.maximum(i * per - 1, 0), 0)),
        pl.BlockSpec((HALO, width), lambda i: (jnp.minimum((i + 1) * per, last), 0)),
    ]


def _assemble(dst, main, prev, nxt, tm):
    dst[0:HALO, :] = prev[...]
    dst[HALO:HALO + tm, :] = main[...]
    dst[HALO + tm:, :] = nxt[...]


def _store_rows(he_s, h_prev, h_main, h_next, tm, tpb):
    t = pl.program_id(0) % tpb
    keep_prev = jnp.where(t != 0, 1.0, 0.0)
    keep_next = jnp.where(t != tpb - 1, 1.0, 0.0)
    he_s[0:HALO, :] = (h_prev * keep_prev).astype(BF16)
    he_s[HALO:HALO + tm, :] = h_main.astype(BF16)
    he_s[HALO + tm:, :] = (h_next * keep_next).astype(BF16)


def _store_mod_rows(he_s, xm, xp, xn, nw, shift, scale, tm, tpb):
    _store_rows(he_s, _rms_mod(xp[...], nw, shift, scale), _rms_mod(xm[...], nw, shift, scale),
                _rms_mod(xn[...], nw, shift, scale), tm, tpb)


def _conv3(a_ext, cw, cb, tm):
    rows = tm + 2 * HALO
    prev = pltpu.roll(a_ext, 1, 0)[HALO:HALO + tm]
    nxt = pltpu.roll(a_ext, rows - 1, 0)[HALO:HALO + tm]
    cur = a_ext[HALO:HALO + tm]
    return prev * cw[0:1] + cur * cw[1:2] + nxt * cw[2:3] + cb


def _gelu_exact(x):
    return 0.5 * x * (1.0 + lax.erf(x * np.float32(np.sqrt(0.5))))


def _convglu(he_s, act_s, wup_ref, cw_ref, cb_ref, wdown_ref, tm):
    d_ff = wdown_ref.shape[0]
    nchunk = d_ff // FF_CHUNK

    def cols(j, base=0):
        return slice(base + j * FF_CHUNK, base + (j + 1) * FF_CHUNK)

    def up(j):
        a = jnp.dot(he_s[...], wup_ref[:, cols(j)], preferred_element_type=F32)
        val = jnp.dot(he_s[HALO:HALO + tm, :], wup_ref[:, cols(j, d_ff)], preferred_element_type=F32)
        return a, val

    nxt = up(0)
    for j in range(nchunk):
        a, val = nxt
        if j + 1 < nchunk:
            nxt = up(j + 1)
        cv = _conv3(a, cw_ref[:, cols(j)], cb_ref[:, cols(j)], tm)
        act_s[:, cols(j)] = (_gelu_exact(cv) * val).astype(BF16)
    return jnp.dot(act_s[...], wdown_ref[...], preferred_element_type=F32)


def _ffn0_kernel(xm, xp, xn, am, ap, an, gm, gp, gn, gate_m, shift_f, scale_f, gate_f, nw,
                 wo_ref, wup_ref, cw_ref, cb_ref, wdown_ref, o_ref,
                 xe_s, ae_s, ge_s, he_s, act_s, *, tm, tpb):
    b = pl.program_id(0) // tpb
    _assemble(xe_s, xm, xp, xn, tm)
    _assemble(ae_s, am, ap, an, tm)
    _assemble(ge_s, gm, gp, gn, tm)
    y = (jnp.dot(ae_s[...], wo_ref[0], preferred_element_type=F32)
         + jnp.dot(ge_s[...], wo_ref[1], preferred_element_type=F32))
    x1 = xe_s[...] + gate_m[pl.ds(b, 1), :] * y
    xe_s[...] = x1
    h = _rms_mod(x1, nw[...], shift_f[pl.ds(b, 1), :], scale_f[pl.ds(b, 1), :])
    _store_rows(he_s, h[0:HALO], h[HALO:HALO + tm], h[HALO + tm:], tm, tpb)
    y2 = _convglu(he_s, act_s, wup_ref, cw_ref, cb_ref, wdown_ref, tm)
    o_ref[...] = xe_s[HALO:HALO + tm, :] + gate_f[pl.ds(b, 1), :] * y2


def _ffn1_kernel(xm, xp, xn, shift_f, scale_f, gate_f, nw, fnw,
                 wup_ref, cw_ref, cb_ref, wdown_ref, o_ref,
                 he_s, act_s, *, tm, tpb):
    b = pl.program_id(0) // tpb
    _store_mod_rows(he_s, xm, xp, xn, nw[...], shift_f[pl.ds(b, 1), :], scale_f[pl.ds(b, 1), :], tm, tpb)
    y2 = _convglu(he_s, act_s, wup_ref, cw_ref, cb_ref, wdown_ref, tm)
    x2 = xm[...] + gate_f[pl.ds(b, 1), :] * y2
    ms = jnp.mean(x2 * x2, axis=-1, keepdims=True)
    o_ref[...] = x2 * lax.rsqrt(ms + EPS) * fnw[...]


def _mod_spec(k):
    return pl.BlockSpec((8, D_MODEL), lambda i: (0, k))


def _const_spec(shape):
    nd = len(shape)
    return pl.BlockSpec(shape, lambda i: (0,) * nd, pipeline_mode=pl.Buffered(1))


def _ffn_weight_specs(d_ff):
    return [
        _const_spec((D_MODEL, 2 * d_ff)),
        _const_spec((3, d_ff)),
        _const_spec((1, d_ff)),
        _const_spec((d_ff, D_MODEL)),
    ]


def _ffn0(x2d, a, g, mod, nw, wo, wup, cw, cb, wdown, *, tm, tpb):
    n_rows = x2d.shape[0]
    d_ff = wdown.shape[0]
    ext = tm + 2 * HALO
    kern = functools.partial(_ffn0_kernel, tm=tm, tpb=tpb)
    return pl.pallas_call(
        kern,
        grid=(n_rows // tm,),
        in_specs=(_halo_specs(tm, D_MODEL, n_rows) + _halo_specs(tm, NA_WIDTH, n_rows)
                  + _halo_specs(tm, HG_WIDTH, n_rows)
                  + [_mod_spec(2), _mod_spec(3), _mod_spec(4), _mod_spec(5),
                     _const_spec((1, D_MODEL)), _const_spec((2, NA_WIDTH, D_MODEL))]
                  + _ffn_weight_specs(d_ff)),
        out_specs=pl.BlockSpec((tm, D_MODEL), lambda i: (i, 0)),
        out_shape=jax.ShapeDtypeStruct((n_rows, D_MODEL), F32),
        scratch_shapes=[pltpu.VMEM((ext, D_MODEL), F32), pltpu.VMEM((ext, NA_WIDTH), BF16),
                        pltpu.VMEM((ext, HG_WIDTH), BF16), pltpu.VMEM((ext, D_MODEL), BF16),
                        pltpu.VMEM((tm, d_ff), BF16)],
        compiler_params=_cparams(("arbitrary",)),
        name="ffn0",
    )(x2d, x2d, x2d, a, a, a, g, g, g, mod, mod, mod, mod, nw, wo, wup, cw, cb, wdown)


def _ffn1(x2d, mod, nw, fnw, wup, cw, cb, wdown, *, tm, tpb):
    n_rows = x2d.shape[0]
    d_ff = wdown.shape[0]
    ext = tm + 2 * HALO
    kern = functools.partial(_ffn1_kernel, tm=tm, tpb=tpb)
    return pl.pallas_call(
        kern,
        grid=(n_rows // tm,),
        in_specs=(_halo_specs(tm, D_MODEL, n_rows)
                  + [_mod_spec(3), _mod_spec(4), _mod_spec(5),
                     _const_spec((1, D_MODEL)), _const_spec((1, D_MODEL))]
                  + _ffn_weight_specs(d_ff)),
        out_specs=pl.BlockSpec((tm, D_MODEL), lambda i: (i, 0)),
        out_shape=jax.ShapeDtypeStruct((n_rows, D_MODEL), F32),
        scratch_shapes=[pltpu.VMEM((ext, D_MODEL), BF16), pltpu.VMEM((tm, d_ff), BF16)],
        compiler_params=_cparams(("arbitrary",)),
        name="ffn1",
    )(x2d, x2d, x2d, mod, mod, mod, nw, fnw, wup, cw, cb, wdown)


def _mix_kernel(xm, xp, xn, shift_m, scale_m, gate_m, nw, win_ref, cw_ref, cb_ref, wout_ref, o_ref,
                he_s, z_s, *, tm, tpb):
    b = pl.program_id(0) // tpb
    width = wout_ref.shape[0]
    nchunk = width // FF_CHUNK
    _store_mod_rows(he_s, xm, xp, xn, nw[...], shift_m[pl.ds(b, 1), :], scale_m[pl.ds(b, 1), :], tm, tpb)

    def cols(j, base=0):
        return slice(base + j * FF_CHUNK, base + (j + 1) * FF_CHUNK)

    def up(j):
        gb = jnp.dot(he_s[HALO:HALO + tm, :], win_ref[:, cols(j)], preferred_element_type=F32)
        gc = jnp.dot(he_s[...], win_ref[:, cols(j, width)], preferred_element_type=F32)
        u = jnp.dot(he_s[...], win_ref[:, cols(j, 2 * width)], preferred_element_type=F32)
        return gb, gc, u

    nxt = up(0)
    for j in range(nchunk):
        gb, gc, u = nxt
        if j + 1 < nchunk:
            nxt = up(j + 1)
        z_s[:, cols(j)] = (gb * _conv3(gc * u, cw_ref[:, cols(j)], cb_ref[:, cols(j)], tm)).astype(BF16)
    y = jnp.dot(z_s[...], wout_ref[...], preferred_element_type=F32)
    o_ref[...] = xm[...] + gate_m[pl.ds(b, 1), :] * y


def _mix(x2d, mod, nw, win, cw, cb, wout, *, tm, tpb):
    n_rows = x2d.shape[0]
    width = wout.shape[0]
    ext = tm + 2 * HALO
    kern = functools.partial(_mix_kernel, tm=tm, tpb=tpb)
    return pl.pallas_call(
        kern,
        grid=(n_rows // tm,),
        in_specs=(_halo_specs(tm, D_MODEL, n_rows)
                  + [_mod_spec(0), _mod_spec(1), _mod_spec(2), _const_spec((1, D_MODEL)),
                     _const_spec((D_MODEL, 3 * width)), _const_spec((3, width)),
                     _const_spec((1, width)), _const_spec((width, D_MODEL))]),
        out_specs=pl.BlockSpec((tm, D_MODEL), lambda i: (i, 0)),
        out_shape=jax.ShapeDtypeStruct((n_rows, D_MODEL), F32),
        scratch_shapes=[pltpu.VMEM((ext, D_MODEL), BF16), pltpu.VMEM((tm, width), BF16)],
        compiler_params=_cparams(("arbitrary",)),
        name="mix",
    )(x2d, x2d, x2d, mod, mod, mod, nw, win, cw, cb, wout)


def kernel(x, c, ctx, c_ctx, ada_w, ada_b, norm_mix_w, norm_ffn_w, ev_w_in, ev_w_out, na_rpb, hg_lb_logits,
           hg_norm_w, od_w_in, od_conv_w, od_conv_b, od_w_out, ffn_w_up, ffn_conv_w, ffn_conv_b, ffn_w_down,
           final_norm_w):
    batch, s_len, d = x.shape
    t_len = ctx.shape[1]
    n_rows = batch * s_len
    tpb = s_len // TM
    x2d = x.reshape(n_rows, d)
    ctx2d = ctx.reshape(batch * t_len, d)

    c_stack = jnp.concatenate([c, c_ctx[None], jnp.zeros((8 - batch - 1, d), F32)], axis=0)
    mods = _ada(c_stack, ada_w, ada_b, batch + 1)

    w = ev_w_in[0]
    hg0 = 3 * NA_WIDTH
    w_in0 = jnp.concatenate([w[:, :NA_WIDTH] * (LOG2E * NA_HEAD_DIM ** -0.5), w[:, NA_WIDTH:hg0 + HG_WIDTH],
                             w[:, hg0 + 3 * HG_WIDTH:], w[:, hg0 + HG_WIDTH:hg0 + 3 * HG_WIDTH]],
                            axis=1).astype(BF16)
    nw_mix = norm_mix_w.reshape(-1, 1, d)
    nw_ffn = norm_ffn_w.reshape(-1, 1, d)
    p_lat, lf_lat = _proj(x2d, nw_mix[0], mods[0], 0, 1, w_in0, hg_lb_logits, tm=TM, tpb=tpb, fixed_row=None)
    p_ctx, lf_ctx = _proj(ctx2d, nw_mix[0], mods[0], 0, 1, w_in0, hg_lb_logits, tm=t_len, tpb=1,
                          fixed_row=batch)
    a_lat = _na(p_lat, p_ctx, _na_bias_rows(na_rpb[0]), batch=batch, s_len=s_len, t_len=t_len)
    g_lat = _hgrn(p_lat, lf_lat, p_ctx, lf_ctx, hg_norm_w[0].reshape(1, HG_WIDTH),
                  batch=batch, s_len=s_len, t_len=t_len)
    wo0 = ev_w_out[0].astype(BF16).reshape(2, NA_WIDTH, d)
    x1 = _ffn0(x2d, a_lat, g_lat, mods[0], nw_ffn[0], wo0,
               ffn_w_up[0].astype(BF16), ffn_conv_w[0], ffn_conv_b[0][None],
               ffn_w_down[0].astype(BF16), tm=TM_FUSED, tpb=s_len // TM_FUSED)

    x2 = _mix(x1, mods[1], nw_mix[1], od_w_in[0].astype(BF16), od_conv_w[0],
              od_conv_b[0][None], od_w_out[0].astype(BF16), tm=TM, tpb=tpb)
    out = _ffn1(x2, mods[1], nw_ffn[1], final_norm_w.reshape(1, d),
                ffn_w_up[1].astype(BF16), ffn_conv_w[1], ffn_conv_b[1][None],
                ffn_w_down[1].astype(BF16), tm=TM, tpb=tpb)
    return out.reshape(batch, s_len, d)
```

```python
import functools

import numpy as np
import jax
import jax.numpy as jnp
from jax import lax
from jax.experimental import pallas as pl
from jax.experimental.pallas import tpu as pltpu

F32 = jnp.float32
BF16 = jnp.bfloat16
HIGHEST = lax.Precision.HIGHEST

D_MODEL = 1024
GRID_W = 64
EPS = 1e-6
NA_HEADS = 8
NA_HEAD_DIM = 64
NA_WIDTH = NA_HEADS * NA_HEAD_DIM
NA_KH = 8
NA_KW = 16
HG_HEADS = 4
HG_DIM = 128
HG_WIDTH = HG_HEADS * HG_DIM
D_FF = 2816
EV_IN = 4096

V7X_VMEM_BYTES = 64 * 1024 * 1024
VMEM_LIMIT = V7X_VMEM_BYTES - 8 * 1024 * 1024
LANES = 128
HALO = 16
NEG = -1e30
LOG2E = float(np.log2(np.e))

NA_QROWS = 4
NA_KROWS = NA_QROWS + NA_KH
NA_UNROLL = 8
HG_CHUNK = 64
HG_UNROLL = 32
FF_CHUNK = 256
PROJ_SRC = (0, 1, 2, 3, 6, 7, 4, 5)
PROJ_HQ = 3
PROJ_FF = 6
TM = 1024
TM_FUSED = 1024


def _cparams(sem):
    return pltpu.CompilerParams(dimension_semantics=sem, vmem_limit_bytes=VMEM_LIMIT)


def _rms_mod(x, nw, shift, scale):
    ms = jnp.mean(x * x, axis=-1, keepdims=True)
    return (x * lax.rsqrt(ms + EPS)) * (nw * (1.0 + scale)) + shift


def _ada_kernel(ct_ref, w_ref, b_ref, o_ref, *, n_cond):
    s = jax.nn.silu(ct_ref[...])
    o_ref[0] = jnp.zeros(o_ref.shape[1:], F32)
    for r in range(n_cond):
        o_ref[0, r:r + 1, :] = jnp.sum(s[:, r:r + 1] * w_ref[0], axis=0, keepdims=True) + b_ref[0]


def _ada(c_stack, ada_w, ada_b, n_cond):
    depth, d, n6 = ada_w.shape
    tn = 1536
    return pl.pallas_call(
        functools.partial(_ada_kernel, n_cond=n_cond),
        grid=(depth, n6 // tn),
        in_specs=[
            pl.BlockSpec((d, 8), lambda l, j: (0, 0)),
            pl.BlockSpec((1, d, tn), lambda l, j: (l, 0, j)),
            pl.BlockSpec((1, 1, tn), lambda l, j: (l, 0, j)),
        ],
        out_specs=pl.BlockSpec((1, 8, tn), lambda l, j: (l, 0, j)),
        out_shape=jax.ShapeDtypeStruct((depth, 8, n6), F32),
        compiler_params=_cparams(("arbitrary", "arbitrary")),
        name="ada",
    )(c_stack.T, ada_w, ada_b.reshape(depth, 1, n6))


def _forget_bounds(lbl_ref):
    ll = lbl_ref[...]
    mx = jnp.maximum(ll[0], ll[1])
    e0 = jnp.exp(ll[0] - mx)
    lb = e0 / (e0 + jnp.exp(ll[1] - mx))
    return 0.5 * (1.0 + lb), 0.5 * (1.0 - lb)


def _proj_kernel(x_ref, nw_ref, shift_ref, scale_ref, w_ref, lbl_ref, o_ref, lf_ref, h_s, *, tpb, fixed_row):
    if fixed_row is None:
        row = pl.program_id(0) // tpb
    else:
        row = fixed_row
    h = _rms_mod(x_ref[...], nw_ref[...], shift_ref[pl.ds(row, 1), :], scale_ref[pl.ds(row, 1), :])
    h_s[...] = h.astype(BF16)
    f_mid, f_amp = _forget_bounds(lbl_ref)
    ncol = HG_WIDTH
    nblk = w_ref.shape[1] // ncol
    for blk in sorted(range(nblk), key=lambda k: (k < PROJ_FF and k != PROJ_HQ, k)):
        cols = slice(blk * ncol, (blk + 1) * ncol)
        src = PROJ_SRC[blk]
        y = jnp.dot(h_s[...], w_ref[:, src * ncol:(src + 1) * ncol].astype(BF16), preferred_element_type=F32)
        if blk == 0:
            y = y * np.float32(LOG2E * NA_HEAD_DIM ** -0.5)
        if blk == PROJ_HQ:
            half = 0.5 * y
            y = half + half * jnp.tanh(half)
        if blk < PROJ_FF:
            o_ref[:, cols] = y.astype(o_ref.dtype)
        else:
            d = blk - PROJ_FF
            fg = f_mid[d:d + 1] + f_amp[d:d + 1] * jnp.tanh(0.5 * y)
            lf_ref[:, d * ncol:(d + 1) * ncol] = jnp.log(fg) * np.float32(1.0 / np.log(2.0))


def _proj(x2d, nw, mod, shift_blk, scale_blk, w, lb_logits, *, tm, tpb, fixed_row):
    n_rows, d = x2d.shape
    n = w.shape[1]
    n_bf = PROJ_FF * HG_WIDTH
    kern = functools.partial(_proj_kernel, tpb=tpb, fixed_row=fixed_row)
    return pl.pallas_call(
        kern,
        grid=(n_rows // tm,),
        in_specs=[
            pl.BlockSpec((tm, d), lambda i: (i, 0)),
            pl.BlockSpec((1, d), lambda i: (0, 0)),
            pl.BlockSpec((8, d), lambda i: (0, shift_blk)),
            pl.BlockSpec((8, d), lambda i: (0, scale_blk)),
            pl.BlockSpec((d, n), lambda i: (0, 0), pipeline_mode=pl.Buffered(1)),
            pl.BlockSpec(lb_logits.shape, lambda i: (0, 0, 0)),
        ],
        out_specs=[pl.BlockSpec((tm, n_bf), lambda i: (i, 0)),
                   pl.BlockSpec((tm, n - n_bf), lambda i: (i, 0))],
        out_shape=[jax.ShapeDtypeStruct((n_rows, n_bf), BF16),
                   jax.ShapeDtypeStruct((n_rows, n - n_bf), F32)],
        scratch_shapes=[pltpu.VMEM((tm, d), BF16)],
        compiler_params=_cparams(("arbitrary",)),
        name="proj",
    )(x2d, nw, mod, mod, w, lb_logits)


def _na_bias_rows(rpb):
    h, nr, nc = rpb.shape
    return jnp.pad(rpb.astype(F32) * LOG2E, ((0, 0), (0, 2 * NA_KH - nr), (0, LANES - nc)))


def _na_build_bands(rows_ref, band_ref):
    qc = lax.broadcasted_iota(jnp.int32, (GRID_W, LANES), 0)
    lane = lax.broadcasted_iota(jnp.int32, (GRID_W, LANES), 1)
    kc = jnp.where(lane < GRID_W, lane, lane - GRID_W)
    start = jnp.clip(qc - NA_KW // 2, 0, GRID_W - NA_KW)
    valid = (kc >= start) & (kc < start + NA_KW)
    masked = jnp.full((GRID_W, LANES), NEG, F32)
    for hh in range(band_ref.shape[0]):
        for a in range(2 * NA_KH - 1):
            x = jnp.broadcast_to(rows_ref[hh, a:a + 1, :], (GRID_W, LANES))
            lo = pltpu.roll(x, LANES - (NA_KW - 1), 1, stride=1, stride_axis=0)
            hi = pltpu.roll(x, LANES - (NA_KW - 1) - GRID_W, 1, stride=1, stride_axis=0)
            band_ref[hh, a] = jnp.where(valid, jnp.where(lane < GRID_W, lo, hi), masked)
        band_ref[hh, 2 * NA_KH - 1] = masked


def _na_block_index(rows):
    masked = 2 * NA_KH - 1

    def block_idx(rb):
        kr0 = int(np.clip(NA_QROWS * rb - NA_KH // 2, 0, rows - NA_KROWS))
        idx = np.empty((NA_QROWS, NA_KROWS), np.int32)
        for qi in range(NA_QROWS):
            qr = NA_QROWS * rb + qi
            r0 = int(np.clip(qr - NA_KH // 2, 0, rows - NA_KH))
            for kj in range(NA_KROWS):
                kr = kr0 + kj
                idx[qi, kj] = kr - qr + (NA_KH - 1) if r0 <= kr < r0 + NA_KH else masked
        return idx

    nblk = rows // NA_QROWS
    interior = block_idx(1)
    assert all(np.array_equal(block_idx(rb), interior) for rb in range(1, nblk - 1))
    return np.stack([block_idx(0), interior, block_idx(nblk - 1)])


def _na_kernel(q_ref, k_ref, v_ref, kc_ref, vc_ref, rows_ref, o_ref, bias_ref, band_ref):
    s_len = q_ref.shape[0]
    qb = NA_QROWS * GRID_W
    kb = NA_KROWS * GRID_W
    nblk = s_len // qb
    rows = s_len // GRID_W
    _na_build_bands(rows_ref, band_ref)

    idx = _na_block_index(rows)
    left = lax.broadcasted_iota(jnp.int32, (GRID_W, LANES), 1) < GRID_W
    for hh in range(2):
        for t in range(3):
            for qi in range(NA_QROWS):
                for j in range(NA_KROWS // 2):
                    a, b = int(idx[t, qi, 2 * j]), int(idx[t, qi, 2 * j + 1])
                    tile = band_ref[hh, a] if a == b else jnp.where(left, band_ref[hh, a], band_ref[hh, b])
                    bias_ref[hh, t, qi * GRID_W:(qi + 1) * GRID_W, j * LANES:(j + 1) * LANES] = tile

    lane = lax.broadcasted_iota(jnp.int32, (qb, LANES), 1)
    in_head = [(lane >= NA_HEAD_DIM * hh) & (lane < NA_HEAD_DIM * (hh + 1)) for hh in range(2)]
    dn_t = (((1,), (1,)), ((), ()))

    def head_values(v, hh):
        return jnp.concatenate([jnp.where(in_head[hh], v[r:r + qb], jnp.ones_like(v[r:r + qb]))
                                for r in range(0, v.shape[0], qb)], axis=0)

    kc = kc_ref[...]
    vc = [head_values(vc_ref[...], hh) for hh in range(2)]

    def body(it, carry):
        blocks = []
        for u in range(NA_UNROLL):
            rb = it * NA_UNROLL + u
            kr0 = jnp.clip(NA_QROWS * rb - NA_KH // 2, 0, rows - NA_KROWS)
            ks = pl.multiple_of(kr0 * GRID_W, qb)
            qs = pl.multiple_of(rb * qb, qb)
            btype = jnp.where(rb == 0, 0, jnp.where(rb == nblk - 1, 2, 1))
            blocks.append((qs, k_ref[pl.ds(ks, kb), :], v_ref[pl.ds(ks, kb), :], q_ref[pl.ds(qs, qb), :], btype))
        logits = []
        for qs, kw, vw, q, btype in blocks:
            q2 = jnp.concatenate([jnp.where(in_head[hh], q, jnp.zeros_like(q)) for hh in range(2)], axis=0)
            s_loc = lax.dot_general(q2, kw, dn_t, preferred_element_type=F32)
            s_ctx = lax.dot_general(q2, kc, dn_t, preferred_element_type=F32)
            for hh in range(2):
                logits.append((s_loc[hh * qb:(hh + 1) * qb] + bias_ref[hh, btype], s_ctx[hh * qb:(hh + 1) * qb]))
        probs = []
        for s_loc, s_ctx in logits:
            m = jnp.maximum(jnp.max(s_loc, axis=-1, keepdims=True), jnp.max(s_ctx, axis=-1, keepdims=True))
            probs.append((jnp.exp2(s_loc - m).astype(BF16), jnp.exp2(s_ctx - m).astype(BF16)))
        outs = []
        for i, (p_loc, p_ctx) in enumerate(probs):
            hh = i % 2
            o = (jnp.dot(p_loc, head_values(blocks[i // 2][2], hh), preferred_element_type=F32)
                 + jnp.dot(p_ctx, vc[hh], preferred_element_type=F32))
            outs.append(o / pltpu.roll(o, NA_HEAD_DIM, 1))
        for u, blk in enumerate(blocks):
            o_ref[pl.ds(blk[0], qb), :] = jnp.where(in_head[0], outs[2 * u], outs[2 * u + 1]).astype(o_ref.dtype)
        return carry

    lax.fori_loop(0, nblk // NA_UNROLL, body, 0)


def _na(p_lat, p_ctx, band, *, batch, s_len, t_len):
    hp = NA_HEADS // 2
    kblk = NA_WIDTH // LANES
    qb = NA_QROWS * GRID_W
    kbk = NA_KROWS * GRID_W
    return pl.pallas_call(
        _na_kernel,
        grid=(batch, hp),
        in_specs=[
            pl.BlockSpec((s_len, LANES), lambda b, h: (b, h)),
            pl.BlockSpec((s_len, LANES), lambda b, h: (b, kblk + h)),
            pl.BlockSpec((s_len, LANES), lambda b, h: (b, 2 * kblk + h)),
            pl.BlockSpec((t_len, LANES), lambda b, h: (b, kblk + h)),
            pl.BlockSpec((t_len, LANES), lambda b, h: (b, 2 * kblk + h)),
            pl.BlockSpec((2, 2 * NA_KH, LANES), lambda b, h: (h, 0, 0)),
        ],
        out_specs=pl.BlockSpec((s_len, LANES), lambda b, h: (b, h)),
        out_shape=jax.ShapeDtypeStruct((batch * s_len, NA_WIDTH), BF16),
        scratch_shapes=[pltpu.VMEM((2, 3, qb, kbk), F32), pltpu.VMEM((2, 2 * NA_KH, GRID_W, LANES), F32)],
        compiler_params=_cparams(("arbitrary", "arbitrary")),
        name="na",
    )(p_lat, p_lat, p_lat, p_ctx, p_ctx, band)


def _hg_kernel(q_ref, ff_ref, fb_ref, i_ref, g_ref, cff_ref, cfb_ref, ci_ref, nw_ref,
               o_ref, acc_ref):
    c = HG_CHUNK
    s_len = q_ref.shape[0]
    t_len = ci_ref.shape[0]
    n = s_len // c
    dn_t = (((1,), (1,)), ((), ()))

    def gates(lf):
        return 1.0 - jnp.exp2(lf), lf

    def masked_sum(mask, lf):
        hi = lf.astype(BF16)
        lo = (lf - hi.astype(F32)).astype(BF16)
        r = jnp.dot(mask, jnp.concatenate([hi, lo], axis=1), preferred_element_type=F32)
        return r[:, :LANES] + r[:, LANES:]

    r_t = lax.broadcasted_iota(jnp.int32, (t_len, t_len), 0)
    c_t = lax.broadcasted_iota(jnp.int32, (t_len, t_len), 1)
    vct = ci_ref[...].astype(F32).T.astype(BF16)
    states = []
    for fref, after in ((cff_ref, c_t > r_t), (cfb_ref, c_t < r_t)):
        kk, lf = gates(fref[...])
        kd = (kk * jnp.exp2(masked_sum(after.astype(BF16), lf))).astype(BF16)
        states.append(jnp.dot(vct, kd, preferred_element_type=F32))

    r_c = lax.broadcasted_iota(jnp.int32, (c, c), 0)
    c_c = lax.broadcasted_iota(jnp.int32, (c, c), 1)
    tri = (c_c <= r_c, c_c >= r_c)
    tri_b = tuple(t.astype(BF16) for t in tri)

    def cum(d, lf):
        return masked_sum(tri_b[d], lf)

    acc_ref[...] = jnp.zeros_like(acc_ref)

    def body(jj, carry):
        chains = []
        for u in range(HG_UNROLL):
            j = jj * HG_UNROLL + u
            chains.append((0, pl.multiple_of(j * c, c)))
            chains.append((1, pl.multiple_of((n - 1 - j) * c, c)))
        pre = []
        for d, start in chains:
            rows = pl.ds(start, c)
            kk, lf = gates((ff_ref, fb_ref)[d][rows, :])
            pre.append((kk, lf, q_ref[rows, :].astype(F32), i_ref[rows, :]))
        bcs = [cum(d, p[1]) for (d, _), p in zip(chains, pre)]
        ops = []
        for (d, _), (kk, lf, qh, v), bc in zip(chains, pre, bcs):
            btot = bc[c - 1:c] if d == 0 else bc[0:1]
            mref = bc[c // 2:c // 2 + 1]
            qm = qh * jnp.exp2(bc - mref)
            km = kk * jnp.exp2(mref - bc)
            qe = (qm * jnp.exp2(mref)).astype(BF16)
            kd = (km * jnp.exp2(btot - mref)).astype(BF16)
            ops.append((qe, qm.astype(BF16), km.astype(BF16), kd, jnp.exp2(btot), v))
        atts = [jnp.where(tri[d], lax.dot_general(qm, km, dn_t, preferred_element_type=F32), 0.0).astype(BF16)
                for (d, _), (qe, qm, km, kd, eb, v) in zip(chains, ops)]
        upds = [lax.dot_general(v, kd, (((0,), (0,)), ((), ())), preferred_element_type=F32)
                for (qe, qm, km, kd, eb, v) in ops]
        intra = [jnp.dot(att, o[5], preferred_element_type=F32) for att, o in zip(atts, ops)]
        st = list(carry)
        for idx, (d, start) in enumerate(chains):
            qe, eb = ops[idx][0], ops[idx][4]
            o = intra[idx] + lax.dot_general(qe, st[d].astype(BF16), dn_t, preferred_element_type=F32)
            st[d] = eb * st[d] + upds[idx]
            acc_ref[pl.ds(start, c), :] += o
        return st[0], st[1]

    lax.fori_loop(0, n // HG_UNROLL, body, (states[0], states[1]))

    rb = 512
    nw = nw_ref[...]

    def readout(j, carry):
        rows = pl.ds(pl.multiple_of(j * rb, rb), rb)
        o = acc_ref[rows, :]
        o = o * lax.rsqrt(jnp.mean(o * o, axis=-1, keepdims=True) + EPS) * nw
        o_ref[rows, :] = (o * jax.nn.silu(g_ref[rows, :].astype(F32))).astype(o_ref.dtype)
        return carry

    lax.fori_loop(0, s_len // rb, readout, 0)


def _hgrn(p_lat, lf_lat, p_ctx, lf_ctx, norm_w, *, batch, s_len, t_len):
    nh = HG_HEADS
    per = HG_WIDTH // LANES

    def blk(rows, group):
        return pl.BlockSpec((rows, LANES), lambda b, h: (b, group * per + h))

    return pl.pallas_call(
        _hg_kernel,
        grid=(batch, nh),
        in_specs=[blk(s_len, PROJ_HQ), blk(s_len, 0), blk(s_len, 1), blk(s_len, PROJ_HQ + 1),
                  blk(s_len, PROJ_HQ + 2), blk(t_len, 0), blk(t_len, 1), blk(t_len, PROJ_HQ + 1),
                  pl.BlockSpec((1, LANES), lambda b, h: (0, h))],
        out_specs=pl.BlockSpec((s_len, LANES), lambda b, h: (b, h)),
        out_shape=jax.ShapeDtypeStruct((batch * s_len, HG_WIDTH), BF16),
        scratch_shapes=[pltpu.VMEM((s_len, LANES), F32)],
        compiler_params=_cparams(("arbitrary", "arbitrary")),
        name="hgrn",
    )(p_lat, lf_lat, lf_lat, p_lat, p_lat, lf_ctx, lf_ctx, p_ctx, norm_w)


def _halo_specs(tm, width, n_rows):
    per = tm // HALO
    last = n_rows // HALO - 1
    return [
        pl.BlockSpec((tm, width), lambda i: (i, 0)),
        pl.BlockSpec((HALO, width), lambda i: (jnp.maximum(i * per - 1, 0), 0)),
        pl.BlockSpec((HALO, width), lambda i: (jnp.minimum((i + 1) * per, last), 0)),
    ]


def _assemble(dst, main, prev, nxt, tm):
    dst[0:HALO, :] = prev[...]
    dst[HALO:HALO + tm, :] = main[...]
    dst[HALO + tm:, :] = nxt[...]


def _store_rows(he_s, h_prev, h_main, h_next, tm, tpb):
    t = pl.program_id(0) % tpb
    keep_prev = jnp.where(t != 0, 1.0, 0.0)
    keep_next = jnp.where(t != tpb - 1, 1.0, 0.0)
    he_s[0:HALO, :] = (h_prev * keep_prev).astype(BF16)
    he_s[HALO:HALO + tm, :] = h_main.astype(BF16)
    he_s[HALO + tm:, :] = (h_next * keep_next).astype(BF16)


def _store_mod_rows(he_s, xm, xp, xn, nw, shift, scale, tm, tpb):
    _store_rows(he_s, _rms_mod(xp[...], nw, shift, scale), _rms_mod(xm[...], nw, shift, scale),
                _rms_mod(xn[...], nw, shift, scale), tm, tpb)


def _conv3(a_ext, cw, cb, tm):
    rows = tm + 2 * HALO
    prev = pltpu.roll(a_ext, 1, 0)[HALO:HALO + tm]
    nxt = pltpu.roll(a_ext, rows - 1, 0)[HALO:HALO + tm]
    cur = a_ext[HALO:HALO + tm]
    return prev * cw[0:1] + cur * cw[1:2] + nxt * cw[2:3] + cb


def _gelu_exact(x):
    return 0.5 * x * (1.0 + lax.erf(x * np.float32(np.sqrt(0.5))))


def _convglu(he_s, act_s, wup_ref, cw_ref, cb_ref, wdown_ref, tm):
    d_ff = wdown_ref.shape[0]
    nchunk = d_ff // FF_CHUNK

    def cols(j, base=0):
        return slice(base + j * FF_CHUNK, base + (j + 1) * FF_CHUNK)

    def up(j):
        a = jnp.dot(he_s[...], wup_ref[:, cols(j)].astype(BF16), preferred_element_type=F32)
        val = jnp.dot(he_s[HALO:HALO + tm, :], wup_ref[:, cols(j, d_ff)].astype(BF16),
                      preferred_element_type=F32)
        return a, val

    nxt = up(0)
    for j in range(nchunk):
        a, val = nxt
        if j + 1 < nchunk:
            nxt = up(j + 1)
        cv = _conv3(a, cw_ref[:, cols(j)], cb_ref[:, cols(j)], tm)
        act_s[:, cols(j)] = (_gelu_exact(cv) * val).astype(BF16)
    return jnp.dot(act_s[...], wdown_ref[...], preferred_element_type=F32)


def _ffn0_kernel(xm, xp, xn, am, ap, an, gm, gp, gn, gate_m, shift_f, scale_f, gate_f, nw,
                 wo_ref, wup_ref, cw_ref, cb_ref, wdown_ref, o_ref,
                 xe_s, ae_s, ge_s, he_s, act_s, *, tm, tpb):
    b = pl.program_id(0) // tpb
    _assemble(xe_s, xm, xp, xn, tm)
    _assemble(ae_s, am, ap, an, tm)
    _assemble(ge_s, gm, gp, gn, tm)
    y = (jnp.dot(ae_s[...], wo_ref[0], preferred_element_type=F32)
         + jnp.dot(ge_s[...], wo_ref[1], preferred_element_type=F32))
    x1 = xe_s[...] + gate_m[pl.ds(b, 1), :] * y
    xe_s[...] = x1
    h = _rms_mod(x1, nw[...], shift_f[pl.ds(b, 1), :], scale_f[pl.ds(b, 1), :])
    _store_rows(he_s, h[0:HALO], h[HALO:HALO + tm], h[HALO + tm:], tm, tpb)
    y2 = _convglu(he_s, act_s, wup_ref, cw_ref, cb_ref, wdown_ref, tm)
    o_ref[...] = xe_s[HALO:HALO + tm, :] + gate_f[pl.ds(b, 1), :] * y2


def _ffn1_kernel(xm, xp, xn, shift_f, scale_f, gate_f, nw, fnw,
                 wup_ref, cw_ref, cb_ref, wdown_ref, o_ref,
                 he_s, act_s, *, tm, tpb):
    b = pl.program_id(0) // tpb
    _store_mod_rows(he_s, xm, xp, xn, nw[...], shift_f[pl.ds(b, 1), :], scale_f[pl.ds(b, 1), :], tm, tpb)
    y2 = _convglu(he_s, act_s, wup_ref, cw_ref, cb_ref, wdown_ref, tm)
    x2 = xm[...] + gate_f[pl.ds(b, 1), :] * y2
    ms = jnp.mean(x2 * x2, axis=-1, keepdims=True)
    o_ref[...] = x2 * lax.rsqrt(ms + EPS) * fnw[...]


def _mod_spec(k):
    return pl.BlockSpec((8, D_MODEL), lambda i: (0, k))


def _const_spec(shape):
    nd = len(shape)
    return pl.BlockSpec(shape, lambda i: (0,) * nd, pipeline_mode=pl.Buffered(1))


def _ffn_weight_specs(d_ff):
    return [
        _const_spec((D_MODEL, 2 * d_ff)),
        _const_spec((3, d_ff)),
        _const_spec((1, d_ff)),
        _const_spec((d_ff, D_MODEL)),
    ]


def _ffn0(x2d, a, g, mod, nw, wo, wup, cw, cb, wdown, *, tm, tpb):
    n_rows = x2d.shape[0]
    d_ff = wdown.shape[0]
    ext = tm + 2 * HALO
    kern = functools.partial(_ffn0_kernel, tm=tm, tpb=tpb)
    return pl.pallas_call(
        kern,
        grid=(n_rows // tm,),
        in_specs=(_halo_specs(tm, D_MODEL, n_rows) + _halo_specs(tm, NA_WIDTH, n_rows)
                  + _halo_specs(tm, HG_WIDTH, n_rows)
                  + [_mod_spec(2), _mod_spec(3), _mod_spec(4), _mod_spec(5),
                     _const_spec((1, D_MODEL)), _const_spec((2, NA_WIDTH, D_MODEL))]
                  + _ffn_weight_specs(d_ff)),
        out_specs=pl.BlockSpec((tm, D_MODEL), lambda i: (i, 0)),
        out_shape=jax.ShapeDtypeStruct((n_rows, D_MODEL), F32),
        scratch_shapes=[pltpu.VMEM((ext, D_MODEL), F32), pltpu.VMEM((ext, NA_WIDTH), BF16),
                        pltpu.VMEM((ext, HG_WIDTH), BF16), pltpu.VMEM((ext, D_MODEL), BF16),
                        pltpu.VMEM((tm, d_ff), BF16)],
        compiler_params=_cparams(("arbitrary",)),
        name="ffn0",
    )(x2d, x2d, x2d, a, a, a, g, g, g, mod, mod, mod, mod, nw, wo, wup, cw, cb, wdown)


def _ffn1(x2d, mod, nw, fnw, wup, cw, cb, wdown, *, tm, tpb):
    n_rows = x2d.shape[0]
    d_ff = wdown.shape[0]
    ext = tm + 2 * HALO
    kern = functools.partial(_ffn1_kernel, tm=tm, tpb=tpb)
    return pl.pallas_call(
        kern,
        grid=(n_rows // tm,),
        in_specs=(_halo_specs(tm, D_MODEL, n_rows)
                  + [_mod_spec(3), _mod_spec(4), _mod_spec(5),
                     _const_spec((1, D_MODEL)), _const_spec((1, D_MODEL))]
                  + _ffn_weight_specs(d_ff)),
        out_specs=pl.BlockSpec((tm, D_MODEL), lambda i: (i, 0)),
        out_shape=jax.ShapeDtypeStruct((n_rows, D_MODEL), F32),
        scratch_shapes=[pltpu.VMEM((ext, D_MODEL), BF16), pltpu.VMEM((tm, d_ff), BF16)],
        compiler_params=_cparams(("arbitrary",)),
        name="ffn1",
    )(x2d, x2d, x2d, mod, mod, mod, nw, fnw, wup, cw, cb, wdown)


def _mix_kernel(xm, xp, xn, shift_m, scale_m, gate_m, nw, win_ref, cw_ref, cb_ref, wout_ref, o_ref,
                he_s, z_s, *, tm, tpb):
    b = pl.program_id(0) // tpb
    width = wout_ref.shape[0]
    nchunk = width // FF_CHUNK
    _store_mod_rows(he_s, xm, xp, xn, nw[...], shift_m[pl.ds(b, 1), :], scale_m[pl.ds(b, 1), :], tm, tpb)

    def cols(j, base=0):
        return slice(base + j * FF_CHUNK, base + (j + 1) * FF_CHUNK)

    def w_in(j, base):
        return win_ref[:, cols(j, base)].astype(BF16)

    def up(j):
        gb = jnp.dot(he_s[HALO:HALO + tm, :], w_in(j, 0), preferred_element_type=F32)
        gc = jnp.dot(he_s[...], w_in(j, width), preferred_element_type=F32)
        u = jnp.dot(he_s[...], w_in(j, 2 * width), preferred_element_type=F32)
        return gb, gc, u

    nxt = up(0)
    for j in range(nchunk):
        gb, gc, u = nxt
        if j + 1 < nchunk:
            nxt = up(j + 1)
        z_s[:, cols(j)] = (gb * _conv3(gc * u, cw_ref[:, cols(j)], cb_ref[:, cols(j)], tm)).astype(BF16)
    y = jnp.dot(z_s[...], wout_ref[...].astype(BF16), preferred_element_type=F32)
    o_ref[...] = xm[...] + gate_m[pl.ds(b, 1), :] * y


def _mix(x2d, mod, nw, win, cw, cb, wout, *, tm, tpb):
    n_rows = x2d.shape[0]
    width = wout.shape[0]
    ext = tm + 2 * HALO
    kern = functools.partial(_mix_kernel, tm=tm, tpb=tpb)
    return pl.pallas_call(
        kern,
        grid=(n_rows // tm,),
        in_specs=(_halo_specs(tm, D_MODEL, n_rows)
                  + [_mod_spec(0), _mod_spec(1), _mod_spec(2), _const_spec((1, D_MODEL)),
                     _const_spec((D_MODEL, 3 * width)), _const_spec((3, width)),
                     _const_spec((1, width)), _const_spec((width, D_MODEL))]),
        out_specs=pl.BlockSpec((tm, D_MODEL), lambda i: (i, 0)),
        out_shape=jax.ShapeDtypeStruct((n_rows, D_MODEL), F32),
        scratch_shapes=[pltpu.VMEM((ext, D_MODEL), BF16), pltpu.VMEM((tm, width), BF16)],
        compiler_params=_cparams(("arbitrary",)),
        name="mix",
    )(x2d, x2d, x2d, mod, mod, mod, nw, win, cw, cb, wout)


def kernel(x, c, ctx, c_ctx, ada_w, ada_b, norm_mix_w, norm_ffn_w, ev_w_in, ev_w_out, na_rpb, hg_lb_logits,
           hg_norm_w, od_w_in, od_conv_w, od_conv_b, od_w_out, ffn_w_up, ffn_conv_w, ffn_conv_b, ffn_w_down,
           final_norm_w):
    batch, s_len, d = x.shape
    t_len = ctx.shape[1]
    n_rows = batch * s_len
    tpb = s_len // TM
    x2d = x.reshape(n_rows, d)
    ctx2d = ctx.reshape(batch * t_len, d)

    c_stack = jnp.concatenate([c, c_ctx[None], jnp.zeros((8 - batch - 1, d), F32)], axis=0)
    mods = _ada(c_stack, ada_w, ada_b, batch + 1)

    w_in0 = ev_w_in[0]
    nw_mix = norm_mix_w.reshape(-1, 1, d)
    nw_ffn = norm_ffn_w.reshape(-1, 1, d)
    p_lat, lf_lat = _proj(x2d, nw_mix[0], mods[0], 0, 1, w_in0, hg_lb_logits, tm=TM, tpb=tpb, fixed_row=None)
    p_ctx, lf_ctx = _proj(ctx2d, nw_mix[0], mods[0], 0, 1, w_in0, hg_lb_logits, tm=t_len, tpb=1,
                          fixed_row=batch)
    a_lat = _na(p_lat, p_ctx, _na_bias_rows(na_rpb[0]), batch=batch, s_len=s_len, t_len=t_len)
    g_lat = _hgrn(p_lat, lf_lat, p_ctx, lf_ctx, hg_norm_w[0].reshape(1, HG_WIDTH),
                  batch=batch, s_len=s_len, t_len=t_len)
    wo0 = ev_w_out[0].astype(BF16).reshape(2, NA_WIDTH, d)
    x1 = _ffn0(x2d, a_lat, g_lat, mods[0], nw_ffn[0], wo0,
               ffn_w_up[0].astype(BF16), ffn_conv_w[0], ffn_conv_b[0][None],
               ffn_w_down[0].astype(BF16), tm=TM_FUSED, tpb=s_len // TM_FUSED)

    x2 = _mix(x1, mods[1], nw_mix[1], od_w_in[0], od_conv_w[0], od_conv_b[0][None], od_w_out[0], tm=TM, tpb=tpb)
    out = _ffn1(x2, mods[1], nw_ffn[1], final_norm_w.reshape(1, d),
                ffn_w_up[1], ffn_conv_w[1], ffn_conv_b[1][None],
                ffn_w_down[1].astype(BF16), tm=TM, tpb=tpb)
    return out.reshape(batch, s_len, d)
```

```python
import functools

import numpy as np
import jax
import jax.numpy as jnp
from jax import lax
from jax.experimental import pallas as pl
from jax.experimental.pallas import tpu as pltpu

F32 = jnp.float32
BF16 = jnp.bfloat16
HIGHEST = lax.Precision.HIGHEST

D_MODEL = 1024
GRID_W = 64
EPS = 1e-6
NA_HEADS = 8
NA_HEAD_DIM = 64
NA_WIDTH = NA_HEADS * NA_HEAD_DIM
NA_KH = 8
NA_KW = 16
HG_HEADS = 4
HG_DIM = 128
HG_WIDTH = HG_HEADS * HG_DIM
D_FF = 2816
EV_IN = 4096

V7X_VMEM_BYTES = 64 * 1024 * 1024
VMEM_LIMIT = V7X_VMEM_BYTES - 8 * 1024 * 1024
LANES = 128
HALO = 16
NEG = -1e30
LOG2E = float(np.log2(np.e))

NA_QROWS = 4
NA_KROWS = NA_QROWS + NA_KH
NA_UNROLL = 8
HG_CHUNK = 64
HG_UNROLL = 32
FF_CHUNK = 256
PROJ_SRC = (0, 1, 2, 3, 6, 7, 4, 5)
PROJ_HQ = 3
PROJ_FF = 6
TM = 1024
TM_FUSED = 1024


def _cparams(sem):
    return pltpu.CompilerParams(dimension_semantics=sem, vmem_limit_bytes=VMEM_LIMIT)


def _rms_mod(x, nw, shift, scale):
    ms = jnp.mean(x * x, axis=-1, keepdims=True)
    return (x * lax.rsqrt(ms + EPS)) * (nw * (1.0 + scale)) + shift


def _ada_kernel(ct_ref, w_ref, b_ref, o_ref, *, n_cond):
    s = jax.nn.silu(ct_ref[...])
    o_ref[0] = jnp.zeros(o_ref.shape[1:], F32)
    for r in range(n_cond):
        o_ref[0, r:r + 1, :] = jnp.sum(s[:, r:r + 1] * w_ref[0], axis=0, keepdims=True) + b_ref[0]


def _ada(c_stack, ada_w, ada_b, n_cond):
    depth, d, n6 = ada_w.shape
    tn = 3072
    return pl.pallas_call(
        functools.partial(_ada_kernel, n_cond=n_cond),
        grid=(depth, n6 // tn),
        in_specs=[
            pl.BlockSpec((d, 8), lambda l, j: (0, 0)),
            pl.BlockSpec((1, d, tn), lambda l, j: (l, 0, j)),
            pl.BlockSpec((1, 1, tn), lambda l, j: (l, 0, j)),
        ],
        out_specs=pl.BlockSpec((1, 8, tn), lambda l, j: (l, 0, j)),
        out_shape=jax.ShapeDtypeStruct((depth, 8, n6), F32),
        compiler_params=_cparams(("arbitrary", "arbitrary")),
        name="ada",
    )(c_stack.T, ada_w, ada_b.reshape(depth, 1, n6))


def _forget_bounds(lbl_ref):
    ll = lbl_ref[...]
    mx = jnp.maximum(ll[0], ll[1])
    e0 = jnp.exp(ll[0] - mx)
    lb = e0 / (e0 + jnp.exp(ll[1] - mx))
    return 0.5 * (1.0 + lb), 0.5 * (1.0 - lb)


def _proj_kernel(x_ref, nw_ref, shift_ref, scale_ref, w_ref, lbl_ref, o_ref, lf_ref, h_s, *, tpb, fixed_row):
    if fixed_row is None:
        row = pl.program_id(0) // tpb
    else:
        row = fixed_row
    h = _rms_mod(x_ref[...], nw_ref[...], shift_ref[pl.ds(row, 1), :], scale_ref[pl.ds(row, 1), :])
    h_s[...] = h.astype(BF16)
    f_mid, f_amp = _forget_bounds(lbl_ref)
    ncol = HG_WIDTH
    nblk = w_ref.shape[1] // ncol
    for blk in sorted(range(nblk), key=lambda k: (k < PROJ_FF and k != PROJ_HQ, k)):
        cols = slice(blk * ncol, (blk + 1) * ncol)
        src = PROJ_SRC[blk]
        y = jnp.dot(h_s[...], w_ref[:, src * ncol:(src + 1) * ncol].astype(BF16), preferred_element_type=F32)
        if blk == 0:
            y = y * np.float32(LOG2E * NA_HEAD_DIM ** -0.5)
        if blk == PROJ_HQ:
            half = 0.5 * y
            y = half + half * jnp.tanh(half)
        if blk < PROJ_FF:
            o_ref[:, cols] = y.astype(o_ref.dtype)
        else:
            d = blk - PROJ_FF
            fg = f_mid[d:d + 1] + f_amp[d:d + 1] * jnp.tanh(0.5 * y)
            lf_ref[:, d * ncol:(d + 1) * ncol] = jnp.log(fg) * np.float32(1.0 / np.log(2.0))


def _proj(x2d, nw, mod, shift_blk, scale_blk, w, lb_logits, *, tm, tpb, fixed_row):
    n_rows, d = x2d.shape
    n = w.shape[1]
    n_bf = PROJ_FF * HG_WIDTH
    kern = functools.partial(_proj_kernel, tpb=tpb, fixed_row=fixed_row)
    return pl.pallas_call(
        kern,
        grid=(n_rows // tm,),
        in_specs=[
            pl.BlockSpec((tm, d), lambda i: (i, 0)),
            pl.BlockSpec((1, d), lambda i: (0, 0)),
            pl.BlockSpec((8, d), lambda i: (0, shift_blk)),
            pl.BlockSpec((8, d), lambda i: (0, scale_blk)),
            pl.BlockSpec((d, n), lambda i: (0, 0), pipeline_mode=pl.Buffered(1)),
            pl.BlockSpec(lb_logits.shape, lambda i: (0, 0, 0)),
        ],
        out_specs=[pl.BlockSpec((tm, n_bf), lambda i: (i, 0)),
                   pl.BlockSpec((tm, n - n_bf), lambda i: (i, 0))],
        out_shape=[jax.ShapeDtypeStruct((n_rows, n_bf), BF16),
                   jax.ShapeDtypeStruct((n_rows, n - n_bf), F32)],
        scratch_shapes=[pltpu.VMEM((tm, d), BF16)],
        compiler_params=_cparams(("arbitrary",)),
        name="proj",
    )(x2d, nw, mod, mod, w, lb_logits)


def _na_bias_rows(rpb):
    h, nr, nc = rpb.shape
    return jnp.pad(rpb.astype(F32) * LOG2E, ((0, 0), (0, 2 * NA_KH - nr), (0, LANES - nc)))


def _na_build_bands(rows_ref, band_ref):
    qc = lax.broadcasted_iota(jnp.int32, (GRID_W, LANES), 0)
    lane = lax.broadcasted_iota(jnp.int32, (GRID_W, LANES), 1)
    kc = jnp.where(lane < GRID_W, lane, lane - GRID_W)
    start = jnp.clip(qc - NA_KW // 2, 0, GRID_W - NA_KW)
    valid = (kc >= start) & (kc < start + NA_KW)
    masked = jnp.full((GRID_W, LANES), NEG, F32)
    for hh in range(band_ref.shape[0]):
        for a in range(2 * NA_KH - 1):
            x = jnp.broadcast_to(rows_ref[hh, a:a + 1, :], (GRID_W, LANES))
            lo = pltpu.roll(x, LANES - (NA_KW - 1), 1, stride=1, stride_axis=0)
            hi = pltpu.roll(x, LANES - (NA_KW - 1) - GRID_W, 1, stride=1, stride_axis=0)
            band_ref[hh, a] = jnp.where(valid, jnp.where(lane < GRID_W, lo, hi), masked)
        band_ref[hh, 2 * NA_KH - 1] = masked


def _na_block_index(rows):
    masked = 2 * NA_KH - 1

    def block_idx(rb):
        kr0 = int(np.clip(NA_QROWS * rb - NA_KH // 2, 0, rows - NA_KROWS))
        idx = np.empty((NA_QROWS, NA_KROWS), np.int32)
        for qi in range(NA_QROWS):
            qr = NA_QROWS * rb + qi
            r0 = int(np.clip(qr - NA_KH // 2, 0, rows - NA_KH))
            for kj in range(NA_KROWS):
                kr = kr0 + kj
                idx[qi, kj] = kr - qr + (NA_KH - 1) if r0 <= kr < r0 + NA_KH else masked
        return idx

    nblk = rows // NA_QROWS
    interior = block_idx(1)
    assert all(np.array_equal(block_idx(rb), interior) for rb in range(1, nblk - 1))
    return np.stack([block_idx(0), interior, block_idx(nblk - 1)])


def _na_kernel(q_ref, k_ref, v_ref, kc_ref, vc_ref, rows_ref, o_ref, bias_ref, band_ref):
    s_len = q_ref.shape[0]
    qb = NA_QROWS * GRID_W
    kb = NA_KROWS * GRID_W
    nblk = s_len // qb
    rows = s_len // GRID_W
    _na_build_bands(rows_ref, band_ref)

    idx = _na_block_index(rows)
    left = lax.broadcasted_iota(jnp.int32, (GRID_W, LANES), 1) < GRID_W
    for hh in range(2):
        for t in range(3):
            for qi in range(NA_QROWS):
                for j in range(NA_KROWS // 2):
                    a, b = int(idx[t, qi, 2 * j]), int(idx[t, qi, 2 * j + 1])
                    tile = band_ref[hh, a] if a == b else jnp.where(left, band_ref[hh, a], band_ref[hh, b])
                    bias_ref[hh, t, qi * GRID_W:(qi + 1) * GRID_W, j * LANES:(j + 1) * LANES] = tile

    lane = lax.broadcasted_iota(jnp.int32, (qb, LANES), 1)
    in_head = [(lane >= NA_HEAD_DIM * hh) & (lane < NA_HEAD_DIM * (hh + 1)) for hh in range(2)]
    dn_t = (((1,), (1,)), ((), ()))

    def head_values(v, hh):
        return jnp.concatenate([jnp.where(in_head[hh], v[r:r + qb], jnp.ones_like(v[r:r + qb]))
                                for r in range(0, v.shape[0], qb)], axis=0)

    kc = kc_ref[...]
    vc = [head_values(vc_ref[...], hh) for hh in range(2)]

    def body(it, carry):
        blocks = []
        for u in range(NA_UNROLL):
            rb = it * NA_UNROLL + u
            kr0 = jnp.clip(NA_QROWS * rb - NA_KH // 2, 0, rows - NA_KROWS)
            ks = pl.multiple_of(kr0 * GRID_W, qb)
            qs = pl.multiple_of(rb * qb, qb)
            btype = jnp.where(rb == 0, 0, jnp.where(rb == nblk - 1, 2, 1))
            blocks.append((qs, k_ref[pl.ds(ks, kb), :], v_ref[pl.ds(ks, kb), :], q_ref[pl.ds(qs, qb), :], btype))
        logits = []
        for qs, kw, vw, q, btype in blocks:
            q2 = jnp.concatenate([jnp.where(in_head[hh], q, jnp.zeros_like(q)) for hh in range(2)], axis=0)
            s_loc = lax.dot_general(q2, kw, dn_t, preferred_element_type=F32)
            s_ctx = lax.dot_general(q2, kc, dn_t, preferred_element_type=F32)
            for hh in range(2):
                logits.append((s_loc[hh * qb:(hh + 1) * qb] + bias_ref[hh, btype], s_ctx[hh * qb:(hh + 1) * qb]))
        probs = []
        for s_loc, s_ctx in logits:
            m = jnp.maximum(jnp.max(s_loc, axis=-1, keepdims=True), jnp.max(s_ctx, axis=-1, keepdims=True))
            probs.append((jnp.exp2(s_loc - m).astype(BF16), jnp.exp2(s_ctx - m).astype(BF16)))
        outs = []
        for i, (p_loc, p_ctx) in enumerate(probs):
            hh = i % 2
            o = (jnp.dot(p_loc, head_values(blocks[i // 2][2], hh), preferred_element_type=F32)
                 + jnp.dot(p_ctx, vc[hh], preferred_element_type=F32))
            outs.append(o / pltpu.roll(o, NA_HEAD_DIM, 1))
        for u, blk in enumerate(blocks):
            o_ref[pl.ds(blk[0], qb), :] = jnp.where(in_head[0], outs[2 * u], outs[2 * u + 1]).astype(o_ref.dtype)
        return carry

    lax.fori_loop(0, nblk // NA_UNROLL, body, 0)


def _na(p_lat, p_ctx, band, *, batch, s_len, t_len):
    hp = NA_HEADS // 2
    kblk = NA_WIDTH // LANES
    qb = NA_QROWS * GRID_W
    kbk = NA_KROWS * GRID_W
    return pl.pallas_call(
        _na_kernel,
        grid=(batch, hp),
        in_specs=[
            pl.BlockSpec((s_len, LANES), lambda b, h: (b, h)),
            pl.BlockSpec((s_len, LANES), lambda b, h: (b, kblk + h)),
            pl.BlockSpec((s_len, LANES), lambda b, h: (b, 2 * kblk + h)),
            pl.BlockSpec((t_len, LANES), lambda b, h: (b, kblk + h)),
            pl.BlockSpec((t_len, LANES), lambda b, h: (b, 2 * kblk + h)),
            pl.BlockSpec((2, 2 * NA_KH, LANES), lambda b, h: (h, 0, 0)),
        ],
        out_specs=pl.BlockSpec((s_len, LANES), lambda b, h: (b, h)),
        out_shape=jax.ShapeDtypeStruct((batch * s_len, NA_WIDTH), BF16),
        scratch_shapes=[pltpu.VMEM((2, 3, qb, kbk), F32), pltpu.VMEM((2, 2 * NA_KH, GRID_W, LANES), F32)],
        compiler_params=_cparams(("arbitrary", "arbitrary")),
        name="na",
    )(p_lat, p_lat, p_lat, p_ctx, p_ctx, band)


def _hg_kernel(q_ref, ff_ref, fb_ref, i_ref, g_ref, cff_ref, cfb_ref, ci_ref, nw_ref,
               o_ref, acc_ref):
    c = HG_CHUNK
    s_len = q_ref.shape[0]
    t_len = ci_ref.shape[0]
    n = s_len // c
    dn_t = (((1,), (1,)), ((), ()))

    def gates(lf):
        return 1.0 - jnp.exp2(lf), lf

    def masked_sum(mask, lf):
        hi = lf.astype(BF16)
        lo = (lf - hi.astype(F32)).astype(BF16)
        r = jnp.dot(mask, jnp.concatenate([hi, lo], axis=1), preferred_element_type=F32)
        return r[:, :LANES] + r[:, LANES:]

    r_t = lax.broadcasted_iota(jnp.int32, (t_len, t_len), 0)
    c_t = lax.broadcasted_iota(jnp.int32, (t_len, t_len), 1)
    vct = ci_ref[...].astype(F32).T.astype(BF16)
    states = []
    for fref, after in ((cff_ref, c_t > r_t), (cfb_ref, c_t < r_t)):
        kk, lf = gates(fref[...])
        kd = (kk * jnp.exp2(masked_sum(after.astype(BF16), lf))).astype(BF16)
        states.append(jnp.dot(vct, kd, preferred_element_type=F32))

    r_c = lax.broadcasted_iota(jnp.int32, (c, c), 0)
    c_c = lax.broadcasted_iota(jnp.int32, (c, c), 1)
    tri = (c_c <= r_c, c_c >= r_c)
    tri_b = tuple(t.astype(BF16) for t in tri)

    def cum(d, lf):
        return masked_sum(tri_b[d], lf)

    acc_ref[...] = jnp.zeros_like(acc_ref)

    def body(jj, carry):
        chains = []
        for u in range(HG_UNROLL):
            j = jj * HG_UNROLL + u
            chains.append((0, pl.multiple_of(j * c, c)))
            chains.append((1, pl.multiple_of((n - 1 - j) * c, c)))
        pre = []
        for d, start in chains:
            rows = pl.ds(start, c)
            kk, lf = gates((ff_ref, fb_ref)[d][rows, :])
            pre.append((kk, lf, q_ref[rows, :].astype(F32), i_ref[rows, :]))
        bcs = [cum(d, p[1]) for (d, _), p in zip(chains, pre)]
        ops = []
        for (d, _), (kk, lf, qh, v), bc in zip(chains, pre, bcs):
            btot = bc[c - 1:c] if d == 0 else bc[0:1]
            mref = bc[c // 2:c // 2 + 1]
            qm = qh * jnp.exp2(bc - mref)
            km = kk * jnp.exp2(mref - bc)
            qe = (qm * jnp.exp2(mref)).astype(BF16)
            kd = (km * jnp.exp2(btot - mref)).astype(BF16)
            ops.append((qe, qm.astype(BF16), km.astype(BF16), kd, jnp.exp2(btot), v))
        atts = [jnp.where(tri[d], lax.dot_general(qm, km, dn_t, preferred_element_type=F32), 0.0).astype(BF16)
                for (d, _), (qe, qm, km, kd, eb, v) in zip(chains, ops)]
        upds = [lax.dot_general(v, kd, (((0,), (0,)), ((), ())), preferred_element_type=F32)
                for (qe, qm, km, kd, eb, v) in ops]
        intra = [jnp.dot(att, o[5], preferred_element_type=F32) for att, o in zip(atts, ops)]
        st = list(carry)
        for idx, (d, start) in enumerate(chains):
            qe, eb = ops[idx][0], ops[idx][4]
            o = intra[idx] + lax.dot_general(qe, st[d].astype(BF16), dn_t, preferred_element_type=F32)
            st[d] = eb * st[d] + upds[idx]
            acc_ref[pl.ds(start, c), :] += o
        return st[0], st[1]

    lax.fori_loop(0, n // HG_UNROLL, body, (states[0], states[1]))

    rb = 512
    nw = nw_ref[...]

    def readout(j, carry):
        rows = pl.ds(pl.multiple_of(j * rb, rb), rb)
        o = acc_ref[rows, :]
        o = o * lax.rsqrt(jnp.mean(o * o, axis=-1, keepdims=True) + EPS) * nw
        half = 0.5 * g_ref[rows, :].astype(F32)
        o_ref[rows, :] = (o * (half + half * jnp.tanh(half))).astype(o_ref.dtype)
        return carry

    lax.fori_loop(0, s_len // rb, readout, 0)


def _hgrn(p_lat, lf_lat, p_ctx, lf_ctx, norm_w, *, batch, s_len, t_len):
    nh = HG_HEADS
    per = HG_WIDTH // LANES

    def blk(rows, group):
        return pl.BlockSpec((rows, LANES), lambda b, h: (b, group * per + h))

    return pl.pallas_call(
        _hg_kernel,
        grid=(batch, nh),
        in_specs=[blk(s_len, PROJ_HQ), blk(s_len, 0), blk(s_len, 1), blk(s_len, PROJ_HQ + 1),
                  blk(s_len, PROJ_HQ + 2), blk(t_len, 0), blk(t_len, 1), blk(t_len, PROJ_HQ + 1),
                  pl.BlockSpec((1, LANES), lambda b, h: (0, h))],
        out_specs=pl.BlockSpec((s_len, LANES), lambda b, h: (b, h)),
        out_shape=jax.ShapeDtypeStruct((batch * s_len, HG_WIDTH), BF16),
        scratch_shapes=[pltpu.VMEM((s_len, LANES), F32)],
        compiler_params=_cparams(("arbitrary", "arbitrary")),
        name="hgrn",
    )(p_lat, lf_lat, lf_lat, p_lat, p_lat, lf_ctx, lf_ctx, p_ctx, norm_w)


def _halo_specs(tm, width, n_rows):
    per = tm // HALO
    last = n_rows // HALO - 1
    return [
        pl.BlockSpec((tm, width), lambda i: (i, 0)),
        pl.BlockSpec((HALO, width), lambda i: (jnp.maximum(i * per - 1, 0), 0)),
        pl.BlockSpec((HALO, width), lambda i: (jnp.minimum((i + 1) * per, last), 0)),
    ]


def _assemble(dst, main, prev, nxt, tm):
    dst[0:HALO, :] = prev[...]
    dst[HALO:HALO + tm, :] = main[...]
    dst[HALO + tm:, :] = nxt[...]


def _store_rows(he_s, h_prev, h_main, h_next, tm, tpb):
    t = pl.program_id(0) % tpb
    keep_prev = jnp.where(t != 0, 1.0, 0.0)
    keep_next = jnp.where(t != tpb - 1, 1.0, 0.0)
    he_s[0:HALO, :] = (h_prev * keep_prev).astype(BF16)
    he_s[HALO:HALO + tm, :] = h_main.astype(BF16)
    he_s[HALO + tm:, :] = (h_next * keep_next).astype(BF16)


def _store_mod_rows(he_s, xm, xp, xn, nw, shift, scale, tm, tpb):
    _store_rows(he_s, _rms_mod(xp[...], nw, shift, scale), _rms_mod(xm[...], nw, shift, scale),
                _rms_mod(xn[...], nw, shift, scale), tm, tpb)


def _conv3(a_ext, cw, cb, tm):
    rows = tm + 2 * HALO
    prev = pltpu.roll(a_ext, 1, 0)[HALO:HALO + tm]
    nxt = pltpu.roll(a_ext, rows - 1, 0)[HALO:HALO + tm]
    cur = a_ext[HALO:HALO + tm]
    return prev * cw[0:1] + cur * cw[1:2] + nxt * cw[2:3] + cb


def _gelu_exact(x):
    return 0.5 * x * (1.0 + lax.erf(x * np.float32(np.sqrt(0.5))))


def _convglu(he_s, act_s, wup_ref, cw_ref, cb_ref, wdown_ref, tm):
    d_ff = wdown_ref.shape[0]
    nchunk = d_ff // FF_CHUNK

    def cols(j, base=0):
        return slice(base + j * FF_CHUNK, base + (j + 1) * FF_CHUNK)

    def up(j):
        a = jnp.dot(he_s[...], wup_ref[:, cols(j)].astype(BF16), preferred_element_type=F32)
        val = jnp.dot(he_s[HALO:HALO + tm, :], wup_ref[:, cols(j, d_ff)].astype(BF16),
                      preferred_element_type=F32)
        return a, val

    nxt = up(0)
    for j in range(nchunk):
        a, val = nxt
        if j + 1 < nchunk:
            nxt = up(j + 1)
        cv = _conv3(a, cw_ref[:, cols(j)], cb_ref[:, cols(j)], tm)
        act_s[:, cols(j)] = (_gelu_exact(cv) * val).astype(BF16)
    return jnp.dot(act_s[...], wdown_ref[...], preferred_element_type=F32)


def _ffn0_kernel(xm, xp, xn, am, ap, an, gm, gp, gn, gate_m, shift_f, scale_f, gate_f, nw,
                 wo_ref, wup_ref, cw_ref, cb_ref, wdown_ref, o_ref,
                 xe_s, ae_s, ge_s, he_s, act_s, *, tm, tpb):
    b = pl.program_id(0) // tpb
    _assemble(xe_s, xm, xp, xn, tm)
    _assemble(ae_s, am, ap, an, tm)
    _assemble(ge_s, gm, gp, gn, tm)
    y = (jnp.dot(ae_s[...], wo_ref[0], preferred_element_type=F32)
         + jnp.dot(ge_s[...], wo_ref[1], preferred_element_type=F32))
    x1 = xe_s[...] + gate_m[pl.ds(b, 1), :] * y
    xe_s[...] = x1
    h = _rms_mod(x1, nw[...], shift_f[pl.ds(b, 1), :], scale_f[pl.ds(b, 1), :])
    _store_rows(he_s, h[0:HALO], h[HALO:HALO + tm], h[HALO + tm:], tm, tpb)
    y2 = _convglu(he_s, act_s, wup_ref, cw_ref, cb_ref, wdown_ref, tm)
    o_ref[...] = xe_s[HALO:HALO + tm, :] + gate_f[pl.ds(b, 1), :] * y2


def _ffn1_kernel(xm, xp, xn, shift_f, scale_f, gate_f, nw, fnw,
                 wup_ref, cw_ref, cb_ref, wdown_ref, o_ref,
                 he_s, act_s, *, tm, tpb):
    b = pl.program_id(0) // tpb
    _store_mod_rows(he_s, xm, xp, xn, nw[...], shift_f[pl.ds(b, 1), :], scale_f[pl.ds(b, 1), :], tm, tpb)
    y2 = _convglu(he_s, act_s, wup_ref, cw_ref, cb_ref, wdown_ref, tm)
    x2 = xm[...] + gate_f[pl.ds(b, 1), :] * y2
    ms = jnp.mean(x2 * x2, axis=-1, keepdims=True)
    o_ref[...] = x2 * lax.rsqrt(ms + EPS) * fnw[...]


def _mod_spec(k):
    return pl.BlockSpec((8, D_MODEL), lambda i: (0, k))


def _const_spec(shape):
    nd = len(shape)
    return pl.BlockSpec(shape, lambda i: (0,) * nd, pipeline_mode=pl.Buffered(1))


def _ffn_weight_specs(d_ff):
    return [
        _const_spec((D_MODEL, 2 * d_ff)),
        _const_spec((3, d_ff)),
        _const_spec((1, d_ff)),
        _const_spec((d_ff, D_MODEL)),
    ]


def _ffn0(x2d, a, g, mod, nw, wo, wup, cw, cb, wdown, *, tm, tpb):
    n_rows = x2d.shape[0]
    d_ff = wdown.shape[0]
    ext = tm + 2 * HALO
    kern = functools.partial(_ffn0_kernel, tm=tm, tpb=tpb)
    return pl.pallas_call(
        kern,
        grid=(n_rows // tm,),
        in_specs=(_halo_specs(tm, D_MODEL, n_rows) + _halo_specs(tm, NA_WIDTH, n_rows)
                  + _halo_specs(tm, HG_WIDTH, n_rows)
                  + [_mod_spec(2), _mod_spec(3), _mod_spec(4), _mod_spec(5),
                     _const_spec((1, D_MODEL)), _const_spec((2, NA_WIDTH, D_MODEL))]
                  + _ffn_weight_specs(d_ff)),
        out_specs=pl.BlockSpec((tm, D_MODEL), lambda i: (i, 0)),
        out_shape=jax.ShapeDtypeStruct((n_rows, D_MODEL), F32),
        scratch_shapes=[pltpu.VMEM((ext, D_MODEL), F32), pltpu.VMEM((ext, NA_WIDTH), BF16),
                        pltpu.VMEM((ext, HG_WIDTH), BF16), pltpu.VMEM((ext, D_MODEL), BF16),
                        pltpu.VMEM((tm, d_ff), BF16)],
        compiler_params=_cparams(("arbitrary",)),
        name="ffn0",
    )(x2d, x2d, x2d, a, a, a, g, g, g, mod, mod, mod, mod, nw, wo, wup, cw, cb, wdown)


def _ffn1(x2d, mod, nw, fnw, wup_layers, layer, cw, cb, wdown, *, tm, tpb):
    n_rows = x2d.shape[0]
    d_ff = wdown.shape[0]
    ext = tm + 2 * HALO
    kern = functools.partial(_ffn1_kernel, tm=tm, tpb=tpb)
    weight_specs = _ffn_weight_specs(d_ff)
    weight_specs[0] = pl.BlockSpec((None, D_MODEL, 2 * d_ff), lambda i: (layer, 0, 0),
                                   pipeline_mode=pl.Buffered(1))
    return pl.pallas_call(
        kern,
        grid=(n_rows // tm,),
        in_specs=(_halo_specs(tm, D_MODEL, n_rows)
                  + [_mod_spec(3), _mod_spec(4), _mod_spec(5),
                     _const_spec((1, D_MODEL)), _const_spec((1, D_MODEL))]
                  + weight_specs),
        out_specs=pl.BlockSpec((tm, D_MODEL), lambda i: (i, 0)),
        out_shape=jax.ShapeDtypeStruct((n_rows, D_MODEL), F32),
        scratch_shapes=[pltpu.VMEM((ext, D_MODEL), BF16), pltpu.VMEM((tm, d_ff), BF16)],
        compiler_params=_cparams(("arbitrary",)),
        name="ffn1",
    )(x2d, x2d, x2d, mod, mod, mod, nw, fnw, wup_layers, cw, cb, wdown)


def _mix_kernel(xm, xp, xn, shift_m, scale_m, gate_m, nw, win_ref, cw_ref, cb_ref, wout_ref, o_ref,
                he_s, z_s, *, tm, tpb):
    b = pl.program_id(0) // tpb
    width = wout_ref.shape[0]
    nchunk = width // FF_CHUNK
    _store_mod_rows(he_s, xm, xp, xn, nw[...], shift_m[pl.ds(b, 1), :], scale_m[pl.ds(b, 1), :], tm, tpb)

    def cols(j, base=0):
        return slice(base + j * FF_CHUNK, base + (j + 1) * FF_CHUNK)

    def w_in(j, base):
        return win_ref[:, cols(j, base)].astype(BF16)

    def up(j):
        gb = jnp.dot(he_s[HALO:HALO + tm, :], w_in(j, 0), preferred_element_type=F32)
        gc = jnp.dot(he_s[...], w_in(j, width), preferred_element_type=F32)
        u = jnp.dot(he_s[...], w_in(j, 2 * width), preferred_element_type=F32)
        return gb, gc, u

    nxt = up(0)
    for j in range(nchunk):
        gb, gc, u = nxt
        if j + 1 < nchunk:
            nxt = up(j + 1)
        z_s[:, cols(j)] = (gb * _conv3(gc * u, cw_ref[:, cols(j)], cb_ref[:, cols(j)], tm)).astype(BF16)
    y = jnp.dot(z_s[...], wout_ref[...].astype(BF16), preferred_element_type=F32)
    o_ref[...] = xm[...] + gate_m[pl.ds(b, 1), :] * y


def _mix(x2d, mod, nw, win, cw, cb, wout, *, tm, tpb):
    n_rows = x2d.shape[0]
    width = wout.shape[0]
    ext = tm + 2 * HALO
    kern = functools.partial(_mix_kernel, tm=tm, tpb=tpb)
    return pl.pallas_call(
        kern,
        grid=(n_rows // tm,),
        in_specs=(_halo_specs(tm, D_MODEL, n_rows)
                  + [_mod_spec(0), _mod_spec(1), _mod_spec(2), _const_spec((1, D_MODEL)),
                     _const_spec((D_MODEL, 3 * width)), _const_spec((3, width)),
                     _const_spec((1, width)), _const_spec((width, D_MODEL))]),
        out_specs=pl.BlockSpec((tm, D_MODEL), lambda i: (i, 0)),
        out_shape=jax.ShapeDtypeStruct((n_rows, D_MODEL), F32),
        scratch_shapes=[pltpu.VMEM((ext, D_MODEL), BF16), pltpu.VMEM((tm, width), BF16)],
        compiler_params=_cparams(("arbitrary",)),
        name="mix",
    )(x2d, x2d, x2d, mod, mod, mod, nw, win, cw, cb, wout)


def kernel(x, c, ctx, c_ctx, ada_w, ada_b, norm_mix_w, norm_ffn_w, ev_w_in, ev_w_out, na_rpb, hg_lb_logits,
           hg_norm_w, od_w_in, od_conv_w, od_conv_b, od_w_out, ffn_w_up, ffn_conv_w, ffn_conv_b, ffn_w_down,
           final_norm_w):
    batch, s_len, d = x.shape
    t_len = ctx.shape[1]
    n_rows = batch * s_len
    tpb = s_len // TM
    x2d = x.reshape(n_rows, d)
    ctx2d = ctx.reshape(batch * t_len, d)

    c_stack = jnp.concatenate([c, c_ctx[None], jnp.zeros((8 - batch - 1, d), F32)], axis=0)
    mods = _ada(c_stack, ada_w, ada_b, batch + 1)

    w_in0 = ev_w_in[0]
    nw_mix = norm_mix_w.reshape(-1, 1, d)
    nw_ffn = norm_ffn_w.reshape(-1, 1, d)
    p_lat, lf_lat = _proj(x2d, nw_mix[0], mods[0], 0, 1, w_in0, hg_lb_logits, tm=TM, tpb=tpb, fixed_row=None)
    p_ctx, lf_ctx = _proj(ctx2d, nw_mix[0], mods[0], 0, 1, w_in0, hg_lb_logits, tm=batch * t_len, tpb=1,
                          fixed_row=batch)
    a_lat = _na(p_lat, p_ctx, _na_bias_rows(na_rpb[0]), batch=batch, s_len=s_len, t_len=t_len)
    g_lat = _hgrn(p_lat, lf_lat, p_ctx, lf_ctx, hg_norm_w[0].reshape(1, HG_WIDTH),
                  batch=batch, s_len=s_len, t_len=t_len)
    wo0 = ev_w_out[0].astype(BF16).reshape(2, NA_WIDTH, d)
    x1 = _ffn0(x2d, a_lat, g_lat, mods[0], nw_ffn[0], wo0,
               ffn_w_up[0].astype(BF16), ffn_conv_w[0], ffn_conv_b[0][None],
               ffn_w_down[0].astype(BF16), tm=TM_FUSED, tpb=s_len // TM_FUSED)

    x2 = _mix(x1, mods[1], nw_mix[1], od_w_in[0], od_conv_w[0], od_conv_b[0][None], od_w_out[0], tm=TM, tpb=tpb)
    out = _ffn1(x2, mods[1], nw_ffn[1], final_norm_w.reshape(1, d),
                ffn_w_up, 1, ffn_conv_w[1], ffn_conv_b[1][None],
                ffn_w_down[1].astype(BF16), tm=TM, tpb=tpb)
    return out.reshape(batch, s_len, d)
```

```python
import functools

import numpy as np
import jax
import jax.numpy as jnp
from jax import lax
from jax.experimental import pallas as pl
from jax.experimental.pallas import tpu as pltpu

F32 = jnp.float32
BF16 = jnp.bfloat16
HIGHEST = lax.Precision.HIGHEST

D_MODEL = 1024
GRID_W = 64
EPS = 1e-6
NA_HEADS = 8
NA_HEAD_DIM = 64
NA_WIDTH = NA_HEADS * NA_HEAD_DIM
NA_KH = 8
NA_KW = 16
HG_HEADS = 4
HG_DIM = 128
HG_WIDTH = HG_HEADS * HG_DIM
D_FF = 2816
EV_IN = 4096

V7X_VMEM_BYTES = 64 * 1024 * 1024
VMEM_LIMIT = V7X_VMEM_BYTES - 8 * 1024 * 1024
LANES = 128
HALO = 16
NEG = -1e30
LOG2E = float(np.log2(np.e))

NA_QROWS = 4
NA_KROWS = NA_QROWS + NA_KH
NA_UNROLL = 8
HG_CHUNK = 64
HG_UNROLL = 32
FF_CHUNK = 256
PROJ_SRC = (0, 1, 2, 3, 6, 7, 4, 5)
PROJ_HQ = 3
PROJ_FF = 6
TM = 1024
TM_FUSED = 1024


def _cparams(sem):
    return pltpu.CompilerParams(dimension_semantics=sem, vmem_limit_bytes=VMEM_LIMIT)


def _rms_mod(x, nw, shift, scale):
    ms = jnp.mean(x * x, axis=-1, keepdims=True)
    return (x * lax.rsqrt(ms + EPS)) * (nw * (1.0 + scale)) + shift


def _ada_kernel(ct_ref, w_ref, b_ref, o_ref, *, n_cond):
    s = jax.nn.silu(ct_ref[...])
    o_ref[0] = jnp.zeros(o_ref.shape[1:], F32)
    for r in range(n_cond):
        o_ref[0, r:r + 1, :] = jnp.sum(s[:, r:r + 1] * w_ref[0], axis=0, keepdims=True) + b_ref[0]


def _ada(c_stack, ada_w, ada_b, n_cond):
    depth, d, n6 = ada_w.shape
    tn = 1536
    return pl.pallas_call(
        functools.partial(_ada_kernel, n_cond=n_cond),
        grid=(depth, n6 // tn),
        in_specs=[
            pl.BlockSpec((d, 8), lambda l, j: (0, 0)),
            pl.BlockSpec((1, d, tn), lambda l, j: (l, 0, j)),
            pl.BlockSpec((1, 1, tn), lambda l, j: (l, 0, j)),
        ],
        out_specs=pl.BlockSpec((1, 8, tn), lambda l, j: (l, 0, j)),
        out_shape=jax.ShapeDtypeStruct((depth, 8, n6), F32),
        compiler_params=_cparams(("arbitrary", "arbitrary")),
        name="ada",
    )(c_stack.T, ada_w, ada_b.reshape(depth, 1, n6))


def _forget_bounds(lbl_ref):
    ll = lbl_ref[...]
    mx = jnp.maximum(ll[0], ll[1])
    e0 = jnp.exp(ll[0] - mx)
    lb = e0 / (e0 + jnp.exp(ll[1] - mx))
    return 0.5 * (1.0 + lb), 0.5 * (1.0 - lb)


def _proj_kernel(x_ref, nw_ref, shift_ref, scale_ref, w_ref, lbl_ref, o_ref, lf_ref, h_s, *, tpb, fixed_row):
    if fixed_row is None:
        row = pl.program_id(0) // tpb
    else:
        row = fixed_row
    h = _rms_mod(x_ref[...], nw_ref[...], shift_ref[pl.ds(row, 1), :], scale_ref[pl.ds(row, 1), :])
    h_s[...] = h.astype(BF16)
    f_mid, f_amp = _forget_bounds(lbl_ref)
    ncol = HG_WIDTH
    nblk = w_ref.shape[1] // ncol
    for blk in sorted(range(nblk), key=lambda k: (k < PROJ_FF and k != PROJ_HQ, k)):
        cols = slice(blk * ncol, (blk + 1) * ncol)
        src = PROJ_SRC[blk]
        y = jnp.dot(h_s[...], w_ref[:, src * ncol:(src + 1) * ncol].astype(BF16), preferred_element_type=F32)
        if blk == 0:
            y = y * np.float32(LOG2E * NA_HEAD_DIM ** -0.5)
        if blk == PROJ_HQ:
            half = 0.5 * y
            y = half + half * jnp.tanh(half)
        if blk < PROJ_FF:
            o_ref[:, cols] = y.astype(o_ref.dtype)
        else:
            d = blk - PROJ_FF
            fg = f_mid[d:d + 1] + f_amp[d:d + 1] * jnp.tanh(0.5 * y)
            lf_ref[:, d * ncol:(d + 1) * ncol] = jnp.log(fg) * np.float32(1.0 / np.log(2.0))


def _proj(x2d, nw, mod, shift_blk, scale_blk, w, lb_logits, *, tm, tpb, fixed_row):
    n_rows, d = x2d.shape
    n = w.shape[1]
    n_bf = PROJ_FF * HG_WIDTH
    kern = functools.partial(_proj_kernel, tpb=tpb, fixed_row=fixed_row)
    return pl.pallas_call(
        kern,
        grid=(n_rows // tm,),
        in_specs=[
            pl.BlockSpec((tm, d), lambda i: (i, 0)),
            pl.BlockSpec((1, d), lambda i: (0, 0)),
            pl.BlockSpec((8, d), lambda i: (0, shift_blk)),
            pl.BlockSpec((8, d), lambda i: (0, scale_blk)),
            pl.BlockSpec((d, n), lambda i: (0, 0), pipeline_mode=pl.Buffered(1)),
            pl.BlockSpec(lb_logits.shape, lambda i: (0, 0, 0)),
        ],
        out_specs=[pl.BlockSpec((tm, n_bf), lambda i: (i, 0)),
                   pl.BlockSpec((tm, n - n_bf), lambda i: (i, 0))],
        out_shape=[jax.ShapeDtypeStruct((n_rows, n_bf), BF16),
                   jax.ShapeDtypeStruct((n_rows, n - n_bf), F32)],
        scratch_shapes=[pltpu.VMEM((tm, d), BF16)],
        compiler_params=_cparams(("arbitrary",)),
        name="proj",
    )(x2d, nw, mod, mod, w, lb_logits)


def _na_bias_rows(rpb):
    h, nr, nc = rpb.shape
    return jnp.pad(rpb.astype(F32) * LOG2E, ((0, 0), (0, 2 * NA_KH - nr), (0, LANES - nc)))


def _na_build_bands(rows_ref, band_ref):
    qc = lax.broadcasted_iota(jnp.int32, (GRID_W, LANES), 0)
    lane = lax.broadcasted_iota(jnp.int32, (GRID_W, LANES), 1)
    kc = jnp.where(lane < GRID_W, lane, lane - GRID_W)
    start = jnp.clip(qc - NA_KW // 2, 0, GRID_W - NA_KW)
    valid = (kc >= start) & (kc < start + NA_KW)
    masked = jnp.full((GRID_W, LANES), NEG, F32)
    for hh in range(band_ref.shape[0]):
        for a in range(2 * NA_KH - 1):
            x = jnp.broadcast_to(rows_ref[hh, a:a + 1, :], (GRID_W, LANES))
            lo = pltpu.roll(x, LANES - (NA_KW - 1), 1, stride=1, stride_axis=0)
            hi = pltpu.roll(x, LANES - (NA_KW - 1) - GRID_W, 1, stride=1, stride_axis=0)
            band_ref[hh, a] = jnp.where(valid, jnp.where(lane < GRID_W, lo, hi), masked)
        band_ref[hh, 2 * NA_KH - 1] = masked


def _na_block_index(rows):
    masked = 2 * NA_KH - 1

    def block_idx(rb):
        kr0 = int(np.clip(NA_QROWS * rb - NA_KH // 2, 0, rows - NA_KROWS))
        idx = np.empty((NA_QROWS, NA_KROWS), np.int32)
        for qi in range(NA_QROWS):
            qr = NA_QROWS * rb + qi
            r0 = int(np.clip(qr - NA_KH // 2, 0, rows - NA_KH))
            for kj in range(NA_KROWS):
                kr = kr0 + kj
                idx[qi, kj] = kr - qr + (NA_KH - 1) if r0 <= kr < r0 + NA_KH else masked
        return idx

    nblk = rows // NA_QROWS
    interior = block_idx(1)
    assert all(np.array_equal(block_idx(rb), interior) for rb in range(1, nblk - 1))
    return np.stack([block_idx(0), interior, block_idx(nblk - 1)])


def _na_kernel(q_ref, k_ref, v_ref, kc_ref, vc_ref, rows_ref, *rest, n_cast):
    o_ref = _cast_riders(rest, n_cast)
    bias_ref, band_ref = rest[2 * n_cast + 1:]
    s_len = q_ref.shape[0]
    qb = NA_QROWS * GRID_W
    kb = NA_KROWS * GRID_W
    nblk = s_len // qb
    rows = s_len // GRID_W
    _na_build_bands(rows_ref, band_ref)

    idx = _na_block_index(rows)
    left = lax.broadcasted_iota(jnp.int32, (GRID_W, LANES), 1) < GRID_W
    for hh in range(2):
        for t in range(3):
            for qi in range(NA_QROWS):
                for j in range(NA_KROWS // 2):
                    a, b = int(idx[t, qi, 2 * j]), int(idx[t, qi, 2 * j + 1])
                    tile = band_ref[hh, a] if a == b else jnp.where(left, band_ref[hh, a], band_ref[hh, b])
                    bias_ref[hh, t, qi * GRID_W:(qi + 1) * GRID_W, j * LANES:(j + 1) * LANES] = tile

    lane = lax.broadcasted_iota(jnp.int32, (qb, LANES), 1)
    in_head = [(lane >= NA_HEAD_DIM * hh) & (lane < NA_HEAD_DIM * (hh + 1)) for hh in range(2)]
    dn_t = (((1,), (1,)), ((), ()))

    def head_values(v, hh):
        return jnp.concatenate([jnp.where(in_head[hh], v[r:r + qb], jnp.ones_like(v[r:r + qb]))
                                for r in range(0, v.shape[0], qb)], axis=0)

    kc = kc_ref[...]
    vc = [head_values(vc_ref[...], hh) for hh in range(2)]

    def body(it, carry):
        blocks = []
        for u in range(NA_UNROLL):
            rb = it * NA_UNROLL + u
            kr0 = jnp.clip(NA_QROWS * rb - NA_KH // 2, 0, rows - NA_KROWS)
            ks = pl.multiple_of(kr0 * GRID_W, qb)
            qs = pl.multiple_of(rb * qb, qb)
            btype = jnp.where(rb == 0, 0, jnp.where(rb == nblk - 1, 2, 1))
            blocks.append((qs, k_ref[pl.ds(ks, kb), :], v_ref[pl.ds(ks, kb), :], q_ref[pl.ds(qs, qb), :], btype))
        logits = []
        for qs, kw, vw, q, btype in blocks:
            q2 = jnp.concatenate([jnp.where(in_head[hh], q, jnp.zeros_like(q)) for hh in range(2)], axis=0)
            s_loc = lax.dot_general(q2, kw, dn_t, preferred_element_type=F32)
            s_ctx = lax.dot_general(q2, kc, dn_t, preferred_element_type=F32)
            for hh in range(2):
                logits.append((s_loc[hh * qb:(hh + 1) * qb] + bias_ref[hh, btype], s_ctx[hh * qb:(hh + 1) * qb]))
        probs = []
        for s_loc, s_ctx in logits:
            m = jnp.maximum(jnp.max(s_loc, axis=-1, keepdims=True), jnp.max(s_ctx, axis=-1, keepdims=True))
            probs.append((jnp.exp2(s_loc - m).astype(BF16), jnp.exp2(s_ctx - m).astype(BF16)))
        outs = []
        for i, (p_loc, p_ctx) in enumerate(probs):
            hh = i % 2
            o = (jnp.dot(p_loc, head_values(blocks[i // 2][2], hh), preferred_element_type=F32)
                 + jnp.dot(p_ctx, vc[hh], preferred_element_type=F32))
            outs.append(o / pltpu.roll(o, NA_HEAD_DIM, 1))
        for u, blk in enumerate(blocks):
            o_ref[pl.ds(blk[0], qb), :] = jnp.where(in_head[0], outs[2 * u], outs[2 * u + 1]).astype(o_ref.dtype)
        return carry

    lax.fori_loop(0, nblk // NA_UNROLL, body, 0)


def _cast_riders(rest, n_cast):
    for w_ref, w16_ref in zip(rest[:n_cast], rest[n_cast + 1:2 * n_cast + 1]):
        w16_ref[...] = w_ref[...].astype(BF16)
    return rest[n_cast]


def _rider_specs(casts, batch, inner):
    steps = batch * inner
    cast_in, cast_out, cast_shapes = [], [], []
    for w, layer in casts:
        _, rows, cols = w.shape
        blk_rows = rows // steps
        assert blk_rows * steps == rows and blk_rows % HALO == 0
        cast_in.append(pl.BlockSpec((None, blk_rows, cols), lambda b, h, layer=layer: (layer, b * inner + h, 0)))
        cast_out.append(pl.BlockSpec((blk_rows, cols), lambda b, h: (b * inner + h, 0)))
        cast_shapes.append(jax.ShapeDtypeStruct((rows, cols), BF16))
    return cast_in, cast_out, cast_shapes


def _na(p_lat, p_ctx, band, casts, *, batch, s_len, t_len):
    hp = NA_HEADS // 2
    kblk = NA_WIDTH // LANES
    qb = NA_QROWS * GRID_W
    kbk = NA_KROWS * GRID_W
    cast_in, cast_out, cast_shapes = _rider_specs(casts, batch, hp)
    return pl.pallas_call(
        functools.partial(_na_kernel, n_cast=len(casts)),
        grid=(batch, hp),
        in_specs=[
            pl.BlockSpec((s_len, LANES), lambda b, h: (b, h)),
            pl.BlockSpec((s_len, LANES), lambda b, h: (b, kblk + h)),
            pl.BlockSpec((s_len, LANES), lambda b, h: (b, 2 * kblk + h)),
            pl.BlockSpec((t_len, LANES), lambda b, h: (b, kblk + h)),
            pl.BlockSpec((t_len, LANES), lambda b, h: (b, 2 * kblk + h)),
            pl.BlockSpec((2, 2 * NA_KH, LANES), lambda b, h: (h, 0, 0)),
        ] + cast_in,
        out_specs=[pl.BlockSpec((s_len, LANES), lambda b, h: (b, h))] + cast_out,
        out_shape=[jax.ShapeDtypeStruct((batch * s_len, NA_WIDTH), BF16)] + cast_shapes,
        scratch_shapes=[pltpu.VMEM((2, 3, qb, kbk), F32), pltpu.VMEM((2, 2 * NA_KH, GRID_W, LANES), F32)],
        compiler_params=_cparams(("arbitrary", "arbitrary")),
        name="na",
    )(p_lat, p_lat, p_lat, p_ctx, p_ctx, band, *[w for w, _ in casts])


def _hg_kernel(q_ref, ff_ref, fb_ref, i_ref, g_ref, cff_ref, cfb_ref, ci_ref, nw_ref, *rest, n_cast):
    o_ref = _cast_riders(rest, n_cast)
    acc_ref = rest[2 * n_cast + 1]
    c = HG_CHUNK
    s_len = q_ref.shape[0]
    t_len = ci_ref.shape[0]
    n = s_len // c
    dn_t = (((1,), (1,)), ((), ()))

    def gates(lf):
        return 1.0 - jnp.exp2(lf), lf

    def masked_sum(mask, lf):
        hi = lf.astype(BF16)
        lo = (lf - hi.astype(F32)).astype(BF16)
        r = jnp.dot(mask, jnp.concatenate([hi, lo], axis=1), preferred_element_type=F32)
        return r[:, :LANES] + r[:, LANES:]

    r_t = lax.broadcasted_iota(jnp.int32, (t_len, t_len), 0)
    c_t = lax.broadcasted_iota(jnp.int32, (t_len, t_len), 1)
    vct = ci_ref[...].astype(F32).T.astype(BF16)
    states = []
    for fref, after in ((cff_ref, c_t > r_t), (cfb_ref, c_t < r_t)):
        kk, lf = gates(fref[...])
        kd = (kk * jnp.exp2(masked_sum(after.astype(BF16), lf))).astype(BF16)
        states.append(jnp.dot(vct, kd, preferred_element_type=F32))

    r_c = lax.broadcasted_iota(jnp.int32, (c, c), 0)
    c_c = lax.broadcasted_iota(jnp.int32, (c, c), 1)
    tri = (c_c <= r_c, c_c >= r_c)
    tri_b = tuple(t.astype(BF16) for t in tri)

    def cum(d, lf):
        return masked_sum(tri_b[d], lf)

    acc_ref[...] = jnp.zeros_like(acc_ref)

    def body(jj, carry):
        chains = []
        for u in range(HG_UNROLL):
            j = jj * HG_UNROLL + u
            chains.append((0, pl.multiple_of(j * c, c)))
            chains.append((1, pl.multiple_of((n - 1 - j) * c, c)))
        pre = []
        for d, start in chains:
            rows = pl.ds(start, c)
            kk, lf = gates((ff_ref, fb_ref)[d][rows, :])
            pre.append((kk, lf, q_ref[rows, :].astype(F32), i_ref[rows, :]))
        bcs = [cum(d, p[1]) for (d, _), p in zip(chains, pre)]
        ops = []
        for (d, _), (kk, lf, qh, v), bc in zip(chains, pre, bcs):
            btot = bc[c - 1:c] if d == 0 else bc[0:1]
            mref = bc[c // 2:c // 2 + 1]
            qm = qh * jnp.exp2(bc - mref)
            km = kk * jnp.exp2(mref - bc)
            qe = (qm * jnp.exp2(mref)).astype(BF16)
            kd = (km * jnp.exp2(btot - mref)).astype(BF16)
            ops.append((qe, qm.astype(BF16), km.astype(BF16), kd, jnp.exp2(btot), v))
        atts = [jnp.where(tri[d], lax.dot_general(qm, km, dn_t, preferred_element_type=F32), 0.0).astype(BF16)
                for (d, _), (qe, qm, km, kd, eb, v) in zip(chains, ops)]
        upds = [lax.dot_general(v, kd, (((0,), (0,)), ((), ())), preferred_element_type=F32)
                for (qe, qm, km, kd, eb, v) in ops]
        intra = [jnp.dot(att, o[5], preferred_element_type=F32) for att, o in zip(atts, ops)]
        st = list(carry)
        for idx, (d, start) in enumerate(chains):
            qe, eb = ops[idx][0], ops[idx][4]
            o = intra[idx] + lax.dot_general(qe, st[d].astype(BF16), dn_t, preferred_element_type=F32)
            st[d] = eb * st[d] + upds[idx]
            acc_ref[pl.ds(start, c), :] += o
        return st[0], st[1]

    lax.fori_loop(0, n // HG_UNROLL, body, (states[0], states[1]))

    rb = 512
    nw = nw_ref[...]

    def readout(j, carry):
        rows = pl.ds(pl.multiple_of(j * rb, rb), rb)
        o = acc_ref[rows, :]
        o = o * lax.rsqrt(jnp.mean(o * o, axis=-1, keepdims=True) + EPS) * nw
        half = 0.5 * g_ref[rows, :].astype(F32)
        o_ref[rows, :] = (o * (half + half * jnp.tanh(half))).astype(o_ref.dtype)
        return carry

    lax.fori_loop(0, s_len // rb, readout, 0)


def _hgrn(p_lat, lf_lat, p_ctx, lf_ctx, norm_w, casts, *, batch, s_len, t_len):
    nh = HG_HEADS
    per = HG_WIDTH // LANES
    cast_in, cast_out, cast_shapes = _rider_specs(casts, batch, nh)

    def blk(rows, group):
        return pl.BlockSpec((rows, LANES), lambda b, h: (b, group * per + h))

    return pl.pallas_call(
        functools.partial(_hg_kernel, n_cast=len(casts)),
        grid=(batch, nh),
        in_specs=[blk(s_len, PROJ_HQ), blk(s_len, 0), blk(s_len, 1), blk(s_len, PROJ_HQ + 1),
                  blk(s_len, PROJ_HQ + 2), blk(t_len, 0), blk(t_len, 1), blk(t_len, PROJ_HQ + 1),
                  pl.BlockSpec((1, LANES), lambda b, h: (0, h))] + cast_in,
        out_specs=[pl.BlockSpec((s_len, LANES), lambda b, h: (b, h))] + cast_out,
        out_shape=[jax.ShapeDtypeStruct((batch * s_len, HG_WIDTH), BF16)] + cast_shapes,
        scratch_shapes=[pltpu.VMEM((s_len, LANES), F32)],
        compiler_params=_cparams(("arbitrary", "arbitrary")),
        name="hgrn",
    )(p_lat, lf_lat, lf_lat, p_lat, p_lat, lf_ctx, lf_ctx, p_ctx, norm_w, *[w for w, _ in casts])


def _halo_specs(tm, width, n_rows):
    per = tm // HALO
    last = n_rows // HALO - 1
    return [
        pl.BlockSpec((tm, width), lambda i: (i, 0)),
        pl.BlockSpec((HALO, width), lambda i: (jnp.maximum(i * per - 1, 0), 0)),
        pl.BlockSpec((HALO, width), lambda i: (jnp.minimum((i + 1) * per, last), 0)),
    ]


def _assemble(dst, main, prev, nxt, tm):
    dst[0:HALO, :] = prev[...]
    dst[HALO:HALO + tm, :] = main[...]
    dst[HALO + tm:, :] = nxt[...]


def _store_rows(he_s, h_prev, h_main, h_next, tm, tpb):
    t = pl.program_id(0) % tpb
    keep_prev = jnp.where(t != 0, 1.0, 0.0)
    keep_next = jnp.where(t != tpb - 1, 1.0, 0.0)
    he_s[0:HALO, :] = (h_prev * keep_prev).astype(BF16)
    he_s[HALO:HALO + tm, :] = h_main.astype(BF16)
    he_s[HALO + tm:, :] = (h_next * keep_next).astype(BF16)


def _store_mod_rows(he_s, xm, xp, xn, nw, shift, scale, tm, tpb):
    _store_rows(he_s, _rms_mod(xp[...], nw, shift, scale), _rms_mod(xm[...], nw, shift, scale),
                _rms_mod(xn[...], nw, shift, scale), tm, tpb)


def _conv3(a_ext, cw, cb, tm):
    rows = tm + 2 * HALO
    prev = pltpu.roll(a_ext, 1, 0)[HALO:HALO + tm]
    nxt = pltpu.roll(a_ext, rows - 1, 0)[HALO:HALO + tm]
    cur = a_ext[HALO:HALO + tm]
    return prev * cw[0:1] + cur * cw[1:2] + nxt * cw[2:3] + cb


def _gelu_exact(x):
    return 0.5 * x * (1.0 + lax.erf(x * np.float32(np.sqrt(0.5))))


def _convglu(he_s, act_s, wup_ref, cw_ref, cb_ref, wdown_ref, tm):
    d_ff = wdown_ref.shape[0]
    nchunk = d_ff // FF_CHUNK

    def cols(j, base=0):
        return slice(base + j * FF_CHUNK, base + (j + 1) * FF_CHUNK)

    def up(j):
        a = jnp.dot(he_s[...], wup_ref[:, cols(j)].astype(BF16), preferred_element_type=F32)
        val = jnp.dot(he_s[HALO:HALO + tm, :], wup_ref[:, cols(j, d_ff)].astype(BF16),
                      preferred_element_type=F32)
        return a, val

    nxt = up(0)
    for j in range(nchunk):
        a, val = nxt
        if j + 1 < nchunk:
            nxt = up(j + 1)
        cv = _conv3(a, cw_ref[:, cols(j)], cb_ref[:, cols(j)], tm)
        act_s[:, cols(j)] = (_gelu_exact(cv) * val).astype(BF16)
    return jnp.dot(act_s[...], wdown_ref[...], preferred_element_type=F32)


def _ffn0_kernel(xm, xp, xn, am, ap, an, gm, gp, gn, gate_m, shift_f, scale_f, gate_f, nw,
                 wo_ref, wup_ref, cw_ref, cb_ref, wdown_ref, o_ref,
                 xe_s, ae_s, ge_s, he_s, act_s, *, tm, tpb):
    b = pl.program_id(0) // tpb
    _assemble(xe_s, xm, xp, xn, tm)
    _assemble(ae_s, am, ap, an, tm)
    _assemble(ge_s, gm, gp, gn, tm)
    y = (jnp.dot(ae_s[...], wo_ref[0], preferred_element_type=F32)
         + jnp.dot(ge_s[...], wo_ref[1], preferred_element_type=F32))
    x1 = xe_s[...] + gate_m[pl.ds(b, 1), :] * y
    xe_s[...] = x1
    h = _rms_mod(x1, nw[...], shift_f[pl.ds(b, 1), :], scale_f[pl.ds(b, 1), :])
    _store_rows(he_s, h[0:HALO], h[HALO:HALO + tm], h[HALO + tm:], tm, tpb)
    y2 = _convglu(he_s, act_s, wup_ref, cw_ref, cb_ref, wdown_ref, tm)
    o_ref[...] = xe_s[HALO:HALO + tm, :] + gate_f[pl.ds(b, 1), :] * y2


def _ffn1_kernel(xm, xp, xn, shift_f, scale_f, gate_f, nw, fnw,
                 wup_ref, cw_ref, cb_ref, wdown_ref, o_ref,
                 he_s, act_s, *, tm, tpb):
    b = pl.program_id(0) // tpb
    _store_mod_rows(he_s, xm, xp, xn, nw[...], shift_f[pl.ds(b, 1), :], scale_f[pl.ds(b, 1), :], tm, tpb)
    y2 = _convglu(he_s, act_s, wup_ref, cw_ref, cb_ref, wdown_ref, tm)
    x2 = xm[...] + gate_f[pl.ds(b, 1), :] * y2
    ms = jnp.mean(x2 * x2, axis=-1, keepdims=True)
    o_ref[...] = x2 * lax.rsqrt(ms + EPS) * fnw[...]


def _mod_spec(k):
    return pl.BlockSpec((8, D_MODEL), lambda i: (0, k))


def _const_spec(shape):
    nd = len(shape)
    return pl.BlockSpec(shape, lambda i: (0,) * nd, pipeline_mode=pl.Buffered(1))


def _ffn_weight_specs(d_ff):
    return [
        _const_spec((D_MODEL, 2 * d_ff)),
        _const_spec((3, d_ff)),
        _const_spec((1, d_ff)),
        _const_spec((d_ff, D_MODEL)),
    ]


def _ffn0(x2d, a, g, mod, nw, wo, wup, cw, cb, wdown, *, tm, tpb):
    n_rows = x2d.shape[0]
    d_ff = wdown.shape[0]
    ext = tm + 2 * HALO
    kern = functools.partial(_ffn0_kernel, tm=tm, tpb=tpb)
    return pl.pallas_call(
        kern,
        grid=(n_rows // tm,),
        in_specs=(_halo_specs(tm, D_MODEL, n_rows) + _halo_specs(tm, NA_WIDTH, n_rows)
                  + _halo_specs(tm, HG_WIDTH, n_rows)
                  + [_mod_spec(2), _mod_spec(3), _mod_spec(4), _mod_spec(5),
                     _const_spec((1, D_MODEL)), _const_spec((2, NA_WIDTH, D_MODEL))]
                  + _ffn_weight_specs(d_ff)),
        out_specs=pl.BlockSpec((tm, D_MODEL), lambda i: (i, 0)),
        out_shape=jax.ShapeDtypeStruct((n_rows, D_MODEL), F32),
        scratch_shapes=[pltpu.VMEM((ext, D_MODEL), F32), pltpu.VMEM((ext, NA_WIDTH), BF16),
                        pltpu.VMEM((ext, HG_WIDTH), BF16), pltpu.VMEM((ext, D_MODEL), BF16),
                        pltpu.VMEM((tm, d_ff), BF16)],
        compiler_params=_cparams(("arbitrary",)),
        name="ffn0",
    )(x2d, x2d, x2d, a, a, a, g, g, g, mod, mod, mod, mod, nw, wo, wup, cw, cb, wdown)


def _ffn1(x2d, mod, nw, fnw, wup_layers, layer, cw, cb, wdown, *, tm, tpb):
    n_rows = x2d.shape[0]
    d_ff = wdown.shape[0]
    ext = tm + 2 * HALO
    kern = functools.partial(_ffn1_kernel, tm=tm, tpb=tpb)
    weight_specs = _ffn_weight_specs(d_ff)
    weight_specs[0] = pl.BlockSpec((None, D_MODEL, 2 * d_ff), lambda i: (layer, 0, 0),
                                   pipeline_mode=pl.Buffered(1))
    return pl.pallas_call(
        kern,
        grid=(n_rows // tm,),
        in_specs=(_halo_specs(tm, D_MODEL, n_rows)
                  + [_mod_spec(3), _mod_spec(4), _mod_spec(5),
                     _const_spec((1, D_MODEL)), _const_spec((1, D_MODEL))]
                  + weight_specs),
        out_specs=pl.BlockSpec((tm, D_MODEL), lambda i: (i, 0)),
        out_shape=jax.ShapeDtypeStruct((n_rows, D_MODEL), F32),
        scratch_shapes=[pltpu.VMEM((ext, D_MODEL), BF16), pltpu.VMEM((tm, d_ff), BF16)],
        compiler_params=_cparams(("arbitrary",)),
        name="ffn1",
    )(x2d, x2d, x2d, mod, mod, mod, nw, fnw, wup_layers, cw, cb, wdown)


def _mix_kernel(xm, xp, xn, shift_m, scale_m, gate_m, nw, win_ref, cw_ref, cb_ref, wout_ref, o_ref,
                he_s, z_s, *, tm, tpb):
    b = pl.program_id(0) // tpb
    width = wout_ref.shape[0]
    nchunk = width // FF_CHUNK
    _store_mod_rows(he_s, xm, xp, xn, nw[...], shift_m[pl.ds(b, 1), :], scale_m[pl.ds(b, 1), :], tm, tpb)

    def cols(j, base=0):
        return slice(base + j * FF_CHUNK, base + (j + 1) * FF_CHUNK)

    def w_in(j, base):
        return win_ref[:, cols(j, base)].astype(BF16)

    def up(j):
        gb = jnp.dot(he_s[HALO:HALO + tm, :], w_in(j, 0), preferred_element_type=F32)
        gc = jnp.dot(he_s[...], w_in(j, width), preferred_element_type=F32)
        u = jnp.dot(he_s[...], w_in(j, 2 * width), preferred_element_type=F32)
        return gb, gc, u

    nxt = up(0)
    for j in range(nchunk):
        gb, gc, u = nxt
        if j + 1 < nchunk:
            nxt = up(j + 1)
        z_s[:, cols(j)] = (gb * _conv3(gc * u, cw_ref[:, cols(j)], cb_ref[:, cols(j)], tm)).astype(BF16)
    y = jnp.dot(z_s[...], wout_ref[...].astype(BF16), preferred_element_type=F32)
    o_ref[...] = xm[...] + gate_m[pl.ds(b, 1), :] * y


def _mix(x2d, mod, nw, win, cw, cb, wout, *, tm, tpb):
    n_rows = x2d.shape[0]
    width = wout.shape[0]
    ext = tm + 2 * HALO
    kern = functools.partial(_mix_kernel, tm=tm, tpb=tpb)
    return pl.pallas_call(
        kern,
        grid=(n_rows // tm,),
        in_specs=(_halo_specs(tm, D_MODEL, n_rows)
                  + [_mod_spec(0), _mod_spec(1), _mod_spec(2), _const_spec((1, D_MODEL)),
                     _const_spec((D_MODEL, 3 * width)), _const_spec((3, width)),
                     _const_spec((1, width)), _const_spec((width, D_MODEL))]),
        out_specs=pl.BlockSpec((tm, D_MODEL), lambda i: (i, 0)),
        out_shape=jax.ShapeDtypeStruct((n_rows, D_MODEL), F32),
        scratch_shapes=[pltpu.VMEM((ext, D_MODEL), BF16), pltpu.VMEM((tm, width), BF16)],
        compiler_params=_cparams(("arbitrary",)),
        name="mix",
    )(x2d, x2d, x2d, mod, mod, mod, nw, win, cw, cb, wout)


def kernel(x, c, ctx, c_ctx, ada_w, ada_b, norm_mix_w, norm_ffn_w, ev_w_in, ev_w_out, na_rpb, hg_lb_logits,
           hg_norm_w, od_w_in, od_conv_w, od_conv_b, od_w_out, ffn_w_up, ffn_conv_w, ffn_conv_b, ffn_w_down,
           final_norm_w):
    batch, s_len, d = x.shape
    t_len = ctx.shape[1]
    n_rows = batch * s_len
    tpb = s_len // TM
    x2d = x.reshape(n_rows, d)
    ctx2d = ctx.reshape(batch * t_len, d)

    c_stack = jnp.concatenate([c, c_ctx[None], jnp.zeros((8 - batch - 1, d), F32)], axis=0)
    mods = _ada(c_stack, ada_w, ada_b, batch + 1)

    w_in0 = ev_w_in[0]
    nw_mix = norm_mix_w.reshape(-1, 1, d)
    nw_ffn = norm_ffn_w.reshape(-1, 1, d)
    p_lat, lf_lat = _proj(x2d, nw_mix[0], mods[0], 0, 1, w_in0, hg_lb_logits, tm=TM, tpb=tpb, fixed_row=None)
    p_ctx, lf_ctx = _proj(ctx2d, nw_mix[0], mods[0], 0, 1, w_in0, hg_lb_logits, tm=t_len, tpb=1,
                          fixed_row=batch)
    a_lat, wup0 = _na(p_lat, p_ctx, _na_bias_rows(na_rpb[0]), [(ffn_w_up, 0)],
                      batch=batch, s_len=s_len, t_len=t_len)
    g_lat, wo0, wdown0, wdown1 = _hgrn(p_lat, lf_lat, p_ctx, lf_ctx, hg_norm_w[0].reshape(1, HG_WIDTH),
                                       [(ev_w_out, 0), (ffn_w_down, 0), (ffn_w_down, 1)],
                                       batch=batch, s_len=s_len, t_len=t_len)
    x1 = _ffn0(x2d, a_lat, g_lat, mods[0], nw_ffn[0], wo0.reshape(2, NA_WIDTH, d),
               wup0, ffn_conv_w[0], ffn_conv_b[0][None], wdown0, tm=TM_FUSED, tpb=s_len // TM_FUSED)

    x2 = _mix(x1, mods[1], nw_mix[1], od_w_in[0], od_conv_w[0], od_conv_b[0][None], od_w_out[0], tm=TM, tpb=tpb)
    out = _ffn1(x2, mods[1], nw_ffn[1], final_norm_w.reshape(1, d),
                ffn_w_up, 1, ffn_conv_w[1], ffn_conv_b[1][None],
                wdown1, tm=TM, tpb=tpb)
    return out.reshape(batch, s_len, d)
```

```python
import functools

import numpy as np
import jax
import jax.numpy as jnp
from jax import lax
from jax.experimental import pallas as pl
from jax.experimental.pallas import tpu as pltpu

F32 = jnp.float32
BF16 = jnp.bfloat16

D_MODEL = 1024
GRID_W = 64
EPS = 1e-6
NA_HEADS = 8
NA_HEAD_DIM = 64
NA_WIDTH = NA_HEADS * NA_HEAD_DIM
NA_KH = 8
NA_KW = 16
HG_HEADS = 4
HG_WIDTH = HG_HEADS * 128

V7X_VMEM_BYTES = 64 * 1024 * 1024
VMEM_LIMIT = V7X_VMEM_BYTES - 8 * 1024 * 1024
LANES = 128
HALO = 16
NEG = -1e30
LOG2E = float(np.log2(np.e))

NA_QROWS = 4
NA_KROWS = NA_QROWS + NA_KH
NA_UNROLL = 8
HG_CHUNK = 64
HG_UNROLL = 32
FF_CHUNK = 256
PROJ_SRC = (0, 1, 2, 3, 6, 7, 4, 5)
PROJ_HQ = 3
PROJ_FF = 6
TM = 1024
TM_FUSED = 1024


def _cparams(sem):
    return pltpu.CompilerParams(dimension_semantics=sem, vmem_limit_bytes=VMEM_LIMIT)


def _rms_mod(x, nw, shift, scale):
    ms = jnp.mean(x * x, axis=-1, keepdims=True)
    return (x * lax.rsqrt(ms + EPS)) * (nw * (1.0 + scale)) + shift


def _ada_kernel(ct_ref, w_ref, b_ref, o_ref, *, n_cond):
    s = jax.nn.silu(ct_ref[...])
    o_ref[0] = jnp.zeros(o_ref.shape[1:], F32)
    for r in range(n_cond):
        o_ref[0, r:r + 1, :] = jnp.sum(s[:, r:r + 1] * w_ref[0], axis=0, keepdims=True) + b_ref[0]


def _ada(c_stack, ada_w, ada_b, n_cond):
    depth, d, n6 = ada_w.shape
    tn = 1536
    return pl.pallas_call(
        functools.partial(_ada_kernel, n_cond=n_cond),
        grid=(depth, n6 // tn),
        in_specs=[
            pl.BlockSpec((d, 8), lambda l, j: (0, 0)),
            pl.BlockSpec((1, d, tn), lambda l, j: (l, 0, j)),
            pl.BlockSpec((1, 1, tn), lambda l, j: (l, 0, j)),
        ],
        out_specs=pl.BlockSpec((1, 8, tn), lambda l, j: (l, 0, j)),
        out_shape=jax.ShapeDtypeStruct((depth, 8, n6), F32),
        compiler_params=_cparams(("arbitrary", "arbitrary")),
        name="ada",
    )(c_stack.T, ada_w, ada_b.reshape(depth, 1, n6))


def _forget_bounds(lbl_ref):
    ll = lbl_ref[...]
    mx = jnp.maximum(ll[0], ll[1])
    e0 = jnp.exp(ll[0] - mx)
    lb = e0 / (e0 + jnp.exp(ll[1] - mx))
    return 0.5 * (1.0 + lb), 0.5 * (1.0 - lb)


def _proj_kernel(x_ref, nw_ref, shift_ref, scale_ref, w_ref, lbl_ref, o_ref, lf_ref, h_s, *, tpb, fixed_row):
    if fixed_row is None:
        row = pl.program_id(0) // tpb
    else:
        row = fixed_row
    h = _rms_mod(x_ref[...], nw_ref[...], shift_ref[pl.ds(row, 1), :], scale_ref[pl.ds(row, 1), :])
    h_s[...] = h.astype(BF16)
    f_mid, f_amp = _forget_bounds(lbl_ref)
    ncol = HG_WIDTH
    nblk = w_ref.shape[1] // ncol
    for blk in sorted(range(nblk), key=lambda k: (k < PROJ_FF and k != PROJ_HQ, k)):
        cols = slice(blk * ncol, (blk + 1) * ncol)
        src = PROJ_SRC[blk]
        y = jnp.dot(h_s[...], w_ref[:, src * ncol:(src + 1) * ncol].astype(BF16), preferred_element_type=F32)
        if blk == 0:
            y = y * np.float32(LOG2E * NA_HEAD_DIM ** -0.5)
        if blk == PROJ_HQ:
            half = 0.5 * y
            y = half + half * jnp.tanh(half)
        if blk < PROJ_FF:
            o_ref[:, cols] = y.astype(o_ref.dtype)
        else:
            d = blk - PROJ_FF
            fg = f_mid[d:d + 1] + f_amp[d:d + 1] * jnp.tanh(0.5 * y)
            lf_ref[:, d * ncol:(d + 1) * ncol] = jnp.log(fg) * np.float32(1.0 / np.log(2.0))


def _proj(x2d, nw, mod, shift_blk, scale_blk, w, lb_logits, *, tm, tpb, fixed_row):
    n_rows, d = x2d.shape
    n = w.shape[1]
    n_bf = PROJ_FF * HG_WIDTH
    kern = functools.partial(_proj_kernel, tpb=tpb, fixed_row=fixed_row)
    return pl.pallas_call(
        kern,
        grid=(n_rows // tm,),
        in_specs=[
            pl.BlockSpec((tm, d), lambda i: (i, 0)),
            pl.BlockSpec((1, d), lambda i: (0, 0)),
            pl.BlockSpec((8, d), lambda i: (0, shift_blk)),
            pl.BlockSpec((8, d), lambda i: (0, scale_blk)),
            pl.BlockSpec((d, n), lambda i: (0, 0), pipeline_mode=pl.Buffered(1)),
            pl.BlockSpec(lb_logits.shape, lambda i: (0, 0, 0)),
        ],
        out_specs=[pl.BlockSpec((tm, n_bf), lambda i: (i, 0)),
                   pl.BlockSpec((tm, n - n_bf), lambda i: (i, 0))],
        out_shape=[jax.ShapeDtypeStruct((n_rows, n_bf), BF16),
                   jax.ShapeDtypeStruct((n_rows, n - n_bf), F32)],
        scratch_shapes=[pltpu.VMEM((tm, d), BF16)],
        compiler_params=_cparams(("arbitrary",)),
        name="proj",
    )(x2d, nw, mod, mod, w, lb_logits)


def _na_bias_rows(rpb):
    h, nr, nc = rpb.shape
    return jnp.pad(rpb.astype(F32) * LOG2E, ((0, 0), (0, 2 * NA_KH - nr), (0, LANES - nc)))


def _na_build_bands(rows_ref, band_ref):
    qc = lax.broadcasted_iota(jnp.int32, (GRID_W, LANES), 0)
    lane = lax.broadcasted_iota(jnp.int32, (GRID_W, LANES), 1)
    kc = jnp.where(lane < GRID_W, lane, lane - GRID_W)
    start = jnp.clip(qc - NA_KW // 2, 0, GRID_W - NA_KW)
    valid = (kc >= start) & (kc < start + NA_KW)
    masked = jnp.full((GRID_W, LANES), NEG, F32)
    for hh in range(band_ref.shape[0]):
        for a in range(2 * NA_KH - 1):
            x = jnp.broadcast_to(rows_ref[hh, a:a + 1, :], (GRID_W, LANES))
            lo = pltpu.roll(x, LANES - (NA_KW - 1), 1, stride=1, stride_axis=0)
            hi = pltpu.roll(x, LANES - (NA_KW - 1) - GRID_W, 1, stride=1, stride_axis=0)
            band_ref[hh, a] = jnp.where(valid, jnp.where(lane < GRID_W, lo, hi), masked)
        band_ref[hh, 2 * NA_KH - 1] = masked


def _na_block_index(rows):
    masked = 2 * NA_KH - 1

    def block_idx(rb):
        kr0 = int(np.clip(NA_QROWS * rb - NA_KH // 2, 0, rows - NA_KROWS))
        idx = np.empty((NA_QROWS, NA_KROWS), np.int32)
        for qi in range(NA_QROWS):
            qr = NA_QROWS * rb + qi
            r0 = int(np.clip(qr - NA_KH // 2, 0, rows - NA_KH))
            for kj in range(NA_KROWS):
                kr = kr0 + kj
                idx[qi, kj] = kr - qr + (NA_KH - 1) if r0 <= kr < r0 + NA_KH else masked
        return idx

    nblk = rows // NA_QROWS
    interior = block_idx(1)
    assert all(np.array_equal(block_idx(rb), interior) for rb in range(1, nblk - 1))
    return np.stack([block_idx(0), interior, block_idx(nblk - 1)])


def _na_kernel(q_ref, k_ref, v_ref, kc_ref, vc_ref, rows_ref, *rest, n_cast):
    o_ref = _cast_riders(rest, n_cast)
    bias_ref, band_ref = rest[2 * n_cast + 1:]
    s_len = q_ref.shape[0]
    qb = NA_QROWS * GRID_W
    kb = NA_KROWS * GRID_W
    nblk = s_len // qb
    rows = s_len // GRID_W
    _na_build_bands(rows_ref, band_ref)

    idx = _na_block_index(rows)
    left = lax.broadcasted_iota(jnp.int32, (GRID_W, LANES), 1) < GRID_W
    for hh in range(2):
        for t in range(3):
            for qi in range(NA_QROWS):
                for j in range(NA_KROWS // 2):
                    a, b = int(idx[t, qi, 2 * j]), int(idx[t, qi, 2 * j + 1])
                    tile = band_ref[hh, a] if a == b else jnp.where(left, band_ref[hh, a], band_ref[hh, b])
                    bias_ref[hh, t, qi * GRID_W:(qi + 1) * GRID_W, j * LANES:(j + 1) * LANES] = tile

    lane = lax.broadcasted_iota(jnp.int32, (qb, LANES), 1)
    in_head = [(lane >= NA_HEAD_DIM * hh) & (lane < NA_HEAD_DIM * (hh + 1)) for hh in range(2)]
    dn_t = (((1,), (1,)), ((), ()))

    def head_values(v, hh):
        return jnp.concatenate([jnp.where(in_head[hh], v[r:r + qb], jnp.ones_like(v[r:r + qb]))
                                for r in range(0, v.shape[0], qb)], axis=0)

    kc = kc_ref[...]
    vc = [head_values(vc_ref[...], hh) for hh in range(2)]

    def body(it, carry):
        blocks = []
        for u in range(NA_UNROLL):
            rb = it * NA_UNROLL + u
            kr0 = jnp.clip(NA_QROWS * rb - NA_KH // 2, 0, rows - NA_KROWS)
            ks = pl.multiple_of(kr0 * GRID_W, qb)
            qs = pl.multiple_of(rb * qb, qb)
            btype = jnp.where(rb == 0, 0, jnp.where(rb == nblk - 1, 2, 1))
            blocks.append((qs, k_ref[pl.ds(ks, kb), :], v_ref[pl.ds(ks, kb), :], q_ref[pl.ds(qs, qb), :], btype))
        logits = []
        for qs, kw, vw, q, btype in blocks:
            q2 = jnp.concatenate([jnp.where(in_head[hh], q, jnp.zeros_like(q)) for hh in range(2)], axis=0)
            s_loc = lax.dot_general(q2, kw, dn_t, preferred_element_type=F32)
            s_ctx = lax.dot_general(q2, kc, dn_t, preferred_element_type=F32)
            for hh in range(2):
                logits.append((s_loc[hh * qb:(hh + 1) * qb] + bias_ref[hh, btype], s_ctx[hh * qb:(hh + 1) * qb]))
        probs = []
        for s_loc, s_ctx in logits:
            m = jnp.maximum(jnp.max(s_loc, axis=-1, keepdims=True), jnp.max(s_ctx, axis=-1, keepdims=True))
            probs.append((jnp.exp2(s_loc - m).astype(BF16), jnp.exp2(s_ctx - m).astype(BF16)))
        outs = []
        for i, (p_loc, p_ctx) in enumerate(probs):
            hh = i % 2
            o = (jnp.dot(p_loc, head_values(blocks[i // 2][2], hh), preferred_element_type=F32)
                 + jnp.dot(p_ctx, vc[hh], preferred_element_type=F32))
            outs.append(o / pltpu.roll(o, NA_HEAD_DIM, 1))
        for u, blk in enumerate(blocks):
            o_ref[pl.ds(blk[0], qb), :] = jnp.where(in_head[0], outs[2 * u], outs[2 * u + 1]).astype(o_ref.dtype)
        return carry

    lax.fori_loop(0, nblk // NA_UNROLL, body, 0)


def _cast_riders(rest, n_cast):
    for w_ref, w16_ref in zip(rest[:n_cast], rest[n_cast + 1:2 * n_cast + 1]):
        w16_ref[...] = w_ref[...].astype(BF16)
    return rest[n_cast]


def _rider_specs(casts, grid):
    steps = int(np.prod(grid))

    def step(*idx):
        flat = idx[0]
        for extent, i in zip(grid[1:], idx[1:]):
            flat = flat * extent + i
        return flat

    cast_in, cast_out, cast_shapes = [], [], []
    for w, layer in casts:
        _, rows, cols = w.shape
        blk_rows = rows // steps
        assert blk_rows * steps == rows and blk_rows % HALO == 0
        cast_in.append(pl.BlockSpec((None, blk_rows, cols), lambda *idx, layer=layer: (layer, step(*idx), 0)))
        cast_out.append(pl.BlockSpec((blk_rows, cols), lambda *idx: (step(*idx), 0)))
        cast_shapes.append(jax.ShapeDtypeStruct((rows, cols), BF16))
    return cast_in, cast_out, cast_shapes


def _na(p_lat, p_ctx, band, casts, *, batch, s_len, t_len):
    hp = NA_HEADS // 2
    kblk = NA_WIDTH // LANES
    qb = NA_QROWS * GRID_W
    kbk = NA_KROWS * GRID_W
    cast_in, cast_out, cast_shapes = _rider_specs(casts, (batch, hp))
    return pl.pallas_call(
        functools.partial(_na_kernel, n_cast=len(casts)),
        grid=(batch, hp),
        in_specs=[
            pl.BlockSpec((s_len, LANES), lambda b, h: (b, h)),
            pl.BlockSpec((s_len, LANES), lambda b, h: (b, kblk + h)),
            pl.BlockSpec((s_len, LANES), lambda b, h: (b, 2 * kblk + h)),
            pl.BlockSpec((t_len, LANES), lambda b, h: (b, kblk + h)),
            pl.BlockSpec((t_len, LANES), lambda b, h: (b, 2 * kblk + h)),
            pl.BlockSpec((2, 2 * NA_KH, LANES), lambda b, h: (h, 0, 0)),
        ] + cast_in,
        out_specs=[pl.BlockSpec((s_len, LANES), lambda b, h: (b, h))] + cast_out,
        out_shape=[jax.ShapeDtypeStruct((batch * s_len, NA_WIDTH), BF16)] + cast_shapes,
        scratch_shapes=[pltpu.VMEM((2, 3, qb, kbk), F32), pltpu.VMEM((2, 2 * NA_KH, GRID_W, LANES), F32)],
        compiler_params=_cparams(("arbitrary", "arbitrary")),
        name="na",
    )(p_lat, p_lat, p_lat, p_ctx, p_ctx, band, *[w for w, _ in casts])


def _hg_kernel(q_ref, ff_ref, fb_ref, i_ref, g_ref, cff_ref, cfb_ref, ci_ref, nw_ref, *rest, n_cast):
    o_ref = _cast_riders(rest, n_cast)
    acc_ref = rest[2 * n_cast + 1]
    c = HG_CHUNK
    s_len = q_ref.shape[0]
    t_len = ci_ref.shape[0]
    n = s_len // c
    dn_t = (((1,), (1,)), ((), ()))

    def gates(lf):
        return 1.0 - jnp.exp2(lf), lf

    def masked_sum(mask, lf):
        hi = lf.astype(BF16)
        lo = (lf - hi.astype(F32)).astype(BF16)
        r = jnp.dot(mask, jnp.concatenate([hi, lo], axis=1), preferred_element_type=F32)
        return r[:, :LANES] + r[:, LANES:]

    r_t = lax.broadcasted_iota(jnp.int32, (t_len, t_len), 0)
    c_t = lax.broadcasted_iota(jnp.int32, (t_len, t_len), 1)
    vct = ci_ref[...].astype(F32).T.astype(BF16)
    states = []
    for fref, after in ((cff_ref, c_t > r_t), (cfb_ref, c_t < r_t)):
        kk, lf = gates(fref[...])
        kd = (kk * jnp.exp2(masked_sum(after.astype(BF16), lf))).astype(BF16)
        states.append(jnp.dot(vct, kd, preferred_element_type=F32))

    r_c = lax.broadcasted_iota(jnp.int32, (c, c), 0)
    c_c = lax.broadcasted_iota(jnp.int32, (c, c), 1)
    tri = (c_c <= r_c, c_c >= r_c)
    tri_b = tuple(t.astype(BF16) for t in tri)

    def cum(d, lf):
        return masked_sum(tri_b[d], lf)

    acc_ref[...] = jnp.zeros_like(acc_ref)

    def body(jj, carry):
        chains = []
        for u in range(HG_UNROLL):
            j = jj * HG_UNROLL + u
            chains.append((0, pl.multiple_of(j * c, c)))
            chains.append((1, pl.multiple_of((n - 1 - j) * c, c)))
        pre = []
        for d, start in chains:
            rows = pl.ds(start, c)
            kk, lf = gates((ff_ref, fb_ref)[d][rows, :])
            pre.append((kk, lf, q_ref[rows, :].astype(F32), i_ref[rows, :]))
        bcs = [cum(d, p[1]) for (d, _), p in zip(chains, pre)]
        ops = []
        for (d, _), (kk, lf, qh, v), bc in zip(chains, pre, bcs):
            btot = bc[c - 1:c] if d == 0 else bc[0:1]
            mref = bc[c // 2:c // 2 + 1]
            qm = qh * jnp.exp2(bc - mref)
            km = kk * jnp.exp2(mref - bc)
            qe = (qm * jnp.exp2(mref)).astype(BF16)
            kd = (km * jnp.exp2(btot - mref)).astype(BF16)
            ops.append((qe, qm.astype(BF16), km.astype(BF16), kd, jnp.exp2(btot), v))
        atts = [jnp.where(tri[d], lax.dot_general(qm, km, dn_t, preferred_element_type=F32), 0.0).astype(BF16)
                for (d, _), (qe, qm, km, kd, eb, v) in zip(chains, ops)]
        upds = [lax.dot_general(v, kd, (((0,), (0,)), ((), ())), preferred_element_type=F32)
                for (qe, qm, km, kd, eb, v) in ops]
        intra = [jnp.dot(att, o[5], preferred_element_type=F32) for att, o in zip(atts, ops)]
        st = list(carry)
        for idx, (d, start) in enumerate(chains):
            qe, eb = ops[idx][0], ops[idx][4]
            o = intra[idx] + lax.dot_general(qe, st[d].astype(BF16), dn_t, preferred_element_type=F32)
            st[d] = eb * st[d] + upds[idx]
            acc_ref[pl.ds(start, c), :] += o
        return st[0], st[1]

    lax.fori_loop(0, n // HG_UNROLL, body, (states[0], states[1]))

    rb = 512
    nw = nw_ref[...]

    def readout(j, carry):
        rows = pl.ds(pl.multiple_of(j * rb, rb), rb)
        o = acc_ref[rows, :]
        o = o * lax.rsqrt(jnp.mean(o * o, axis=-1, keepdims=True) + EPS) * nw
        half = 0.5 * g_ref[rows, :].astype(F32)
        o_ref[rows, :] = (o * (half + half * jnp.tanh(half))).astype(o_ref.dtype)
        return carry

    lax.fori_loop(0, s_len // rb, readout, 0)


def _hgrn(p_lat, lf_lat, p_ctx, lf_ctx, norm_w, casts, *, batch, s_len, t_len):
    nh = HG_HEADS
    per = HG_WIDTH // LANES
    cast_in, cast_out, cast_shapes = _rider_specs(casts, (batch, nh))

    def blk(rows, group):
        return pl.BlockSpec((rows, LANES), lambda b, h: (b, group * per + h))

    return pl.pallas_call(
        functools.partial(_hg_kernel, n_cast=len(casts)),
        grid=(batch, nh),
        in_specs=[blk(s_len, PROJ_HQ), blk(s_len, 0), blk(s_len, 1), blk(s_len, PROJ_HQ + 1),
                  blk(s_len, PROJ_HQ + 2), blk(t_len, 0), blk(t_len, 1), blk(t_len, PROJ_HQ + 1),
                  pl.BlockSpec((1, LANES), lambda b, h: (0, h))] + cast_in,
        out_specs=[pl.BlockSpec((s_len, LANES), lambda b, h: (b, h))] + cast_out,
        out_shape=[jax.ShapeDtypeStruct((batch * s_len, HG_WIDTH), BF16)] + cast_shapes,
        scratch_shapes=[pltpu.VMEM((s_len, LANES), F32)],
        compiler_params=_cparams(("arbitrary", "arbitrary")),
        name="hgrn",
    )(p_lat, lf_lat, lf_lat, p_lat, p_lat, lf_ctx, lf_ctx, p_ctx, norm_w, *[w for w, _ in casts])


def _halo_specs(tm, width, n_rows):
    per = tm // HALO
    last = n_rows // HALO - 1
    return [
        pl.BlockSpec((tm, width), lambda i: (i, 0)),
        pl.BlockSpec((HALO, width), lambda i: (jnp.maximum(i * per - 1, 0), 0)),
        pl.BlockSpec((HALO, width), lambda i: (jnp.minimum((i + 1) * per, last), 0)),
    ]


def _assemble(dst, main, prev, nxt, tm):
    dst[0:HALO, :] = prev[...]
    dst[HALO:HALO + tm, :] = main[...]
    dst[HALO + tm:, :] = nxt[...]


def _store_rows(he_s, h_prev, h_main, h_next, tm, tpb):
    t = pl.program_id(0) % tpb
    keep_prev = jnp.where(t != 0, 1.0, 0.0)
    keep_next = jnp.where(t != tpb - 1, 1.0, 0.0)
    he_s[0:HALO, :] = (h_prev * keep_prev).astype(BF16)
    he_s[HALO:HALO + tm, :] = h_main.astype(BF16)
    he_s[HALO + tm:, :] = (h_next * keep_next).astype(BF16)


def _store_mod_rows(he_s, xm, xp, xn, nw, shift, scale, tm, tpb):
    _store_rows(he_s, _rms_mod(xp[...], nw, shift, scale), _rms_mod(xm[...], nw, shift, scale),
                _rms_mod(xn[...], nw, shift, scale), tm, tpb)


def _conv3(a_ext, cw, cb, tm):
    rows = tm + 2 * HALO
    prev = pltpu.roll(a_ext, 1, 0)[HALO:HALO + tm]
    nxt = pltpu.roll(a_ext, rows - 1, 0)[HALO:HALO + tm]
    cur = a_ext[HALO:HALO + tm]
    return prev * cw[0:1] + cur * cw[1:2] + nxt * cw[2:3] + cb


def _gelu_exact(x):
    return 0.5 * x * (1.0 + lax.erf(x * np.float32(np.sqrt(0.5))))


def _convglu(he_s, act_s, wup_ref, cw_ref, cb_ref, wdown_ref, tm):
    d_ff = wdown_ref.shape[0]
    nchunk = d_ff // FF_CHUNK

    def cols(j, base=0):
        return slice(base + j * FF_CHUNK, base + (j + 1) * FF_CHUNK)

    def up(j):
        a = jnp.dot(he_s[...], wup_ref[:, cols(j)], preferred_element_type=F32)
        val = jnp.dot(he_s[HALO:HALO + tm, :], wup_ref[:, cols(j, d_ff)], preferred_element_type=F32)
        return a, val

    nxt = up(0)
    for j in range(nchunk):
        a, val = nxt
        if j + 1 < nchunk:
            nxt = up(j + 1)
        cv = _conv3(a, cw_ref[:, cols(j)], cb_ref[:, cols(j)], tm)
        act_s[:, cols(j)] = (_gelu_exact(cv) * val).astype(BF16)
    return jnp.dot(act_s[...], wdown_ref[...], preferred_element_type=F32)


def _ffn0_kernel(xm, xp, xn, am, ap, an, gm, gp, gn, gate_m, shift_f, scale_f, gate_f, nw,
                 wo_ref, wup_ref, cw_ref, cb_ref, wdown_ref, o_ref,
                 xe_s, ae_s, ge_s, he_s, act_s, *, tm, tpb):
    b = pl.program_id(0) // tpb
    _assemble(xe_s, xm, xp, xn, tm)
    _assemble(ae_s, am, ap, an, tm)
    _assemble(ge_s, gm, gp, gn, tm)
    y = (jnp.dot(ae_s[...], wo_ref[0], preferred_element_type=F32)
         + jnp.dot(ge_s[...], wo_ref[1], preferred_element_type=F32))
    x1 = xe_s[...] + gate_m[pl.ds(b, 1), :] * y
    xe_s[...] = x1
    h = _rms_mod(x1, nw[...], shift_f[pl.ds(b, 1), :], scale_f[pl.ds(b, 1), :])
    _store_rows(he_s, h[0:HALO], h[HALO:HALO + tm], h[HALO + tm:], tm, tpb)
    y2 = _convglu(he_s, act_s, wup_ref, cw_ref, cb_ref, wdown_ref, tm)
    o_ref[...] = xe_s[HALO:HALO + tm, :] + gate_f[pl.ds(b, 1), :] * y2


def _ffn1_kernel(xm, xp, xn, shift_f, scale_f, gate_f, nw, fnw,
                 wup_ref, cw_ref, cb_ref, wdown_ref, o_ref,
                 he_s, act_s, *, tm, tpb):
    b = pl.program_id(0) // tpb
    _store_mod_rows(he_s, xm, xp, xn, nw[...], shift_f[pl.ds(b, 1), :], scale_f[pl.ds(b, 1), :], tm, tpb)
    y2 = _convglu(he_s, act_s, wup_ref, cw_ref, cb_ref, wdown_ref, tm)
    x2 = xm[...] + gate_f[pl.ds(b, 1), :] * y2
    ms = jnp.mean(x2 * x2, axis=-1, keepdims=True)
    o_ref[...] = x2 * lax.rsqrt(ms + EPS) * fnw[...]


def _mod_spec(k):
    return pl.BlockSpec((8, D_MODEL), lambda i: (0, k))


def _const_spec(shape):
    nd = len(shape)
    return pl.BlockSpec(shape, lambda i: (0,) * nd, pipeline_mode=pl.Buffered(1))


def _ffn_weight_specs(d_ff):
    return [
        _const_spec((D_MODEL, 2 * d_ff)),
        _const_spec((3, d_ff)),
        _const_spec((1, d_ff)),
        _const_spec((d_ff, D_MODEL)),
    ]


def _ffn0(x2d, a, g, mod, nw, wo, wup, cw, cb, wdown, *, tm, tpb):
    n_rows = x2d.shape[0]
    d_ff = wdown.shape[0]
    ext = tm + 2 * HALO
    kern = functools.partial(_ffn0_kernel, tm=tm, tpb=tpb)
    return pl.pallas_call(
        kern,
        grid=(n_rows // tm,),
        in_specs=(_halo_specs(tm, D_MODEL, n_rows) + _halo_specs(tm, NA_WIDTH, n_rows)
                  + _halo_specs(tm, HG_WIDTH, n_rows)
                  + [_mod_spec(2), _mod_spec(3), _mod_spec(4), _mod_spec(5),
                     _const_spec((1, D_MODEL)), _const_spec((2, NA_WIDTH, D_MODEL))]
                  + _ffn_weight_specs(d_ff)),
        out_specs=pl.BlockSpec((tm, D_MODEL), lambda i: (i, 0)),
        out_shape=jax.ShapeDtypeStruct((n_rows, D_MODEL), F32),
        scratch_shapes=[pltpu.VMEM((ext, D_MODEL), F32), pltpu.VMEM((ext, NA_WIDTH), BF16),
                        pltpu.VMEM((ext, HG_WIDTH), BF16), pltpu.VMEM((ext, D_MODEL), BF16),
                        pltpu.VMEM((tm, d_ff), BF16)],
        compiler_params=_cparams(("arbitrary",)),
        name="ffn0",
    )(x2d, x2d, x2d, a, a, a, g, g, g, mod, mod, mod, mod, nw, wo, wup, cw, cb, wdown)


def _ffn1(x2d, mod, nw, fnw, wup, cw, cb, wdown, *, tm, tpb):
    n_rows = x2d.shape[0]
    d_ff = wdown.shape[0]
    ext = tm + 2 * HALO
    kern = functools.partial(_ffn1_kernel, tm=tm, tpb=tpb)
    weight_specs = _ffn_weight_specs(d_ff)
    return pl.pallas_call(
        kern,
        grid=(n_rows // tm,),
        in_specs=(_halo_specs(tm, D_MODEL, n_rows)
                  + [_mod_spec(3), _mod_spec(4), _mod_spec(5),
                     _const_spec((1, D_MODEL)), _const_spec((1, D_MODEL))]
                  + weight_specs),
        out_specs=pl.BlockSpec((tm, D_MODEL), lambda i: (i, 0)),
        out_shape=jax.ShapeDtypeStruct((n_rows, D_MODEL), F32),
        scratch_shapes=[pltpu.VMEM((ext, D_MODEL), BF16), pltpu.VMEM((tm, d_ff), BF16)],
        compiler_params=_cparams(("arbitrary",)),
        name="ffn1",
    )(x2d, x2d, x2d, mod, mod, mod, nw, fnw, wup, cw, cb, wdown)


def _mix_kernel(xm, xp, xn, shift_m, scale_m, gate_m, nw, win_ref, cw_ref, cb_ref, wout_ref, *rest,
                tm, tpb, n_cast):
    o_ref = _cast_riders(rest, n_cast)
    he_s, z_s = rest[2 * n_cast + 1:]
    b = pl.program_id(0) // tpb
    width = wout_ref.shape[0]
    nchunk = width // FF_CHUNK
    _store_mod_rows(he_s, xm, xp, xn, nw[...], shift_m[pl.ds(b, 1), :], scale_m[pl.ds(b, 1), :], tm, tpb)

    def cols(j, base=0):
        return slice(base + j * FF_CHUNK, base + (j + 1) * FF_CHUNK)

    def w_in(j, base):
        return win_ref[:, cols(j, base)].astype(BF16)

    def up(j):
        gb = jnp.dot(he_s[HALO:HALO + tm, :], w_in(j, 0), preferred_element_type=F32)
        gc = jnp.dot(he_s[...], w_in(j, width), preferred_element_type=F32)
        u = jnp.dot(he_s[...], w_in(j, 2 * width), preferred_element_type=F32)
        return gb, gc, u

    nxt = up(0)
    for j in range(nchunk):
        gb, gc, u = nxt
        if j + 1 < nchunk:
            nxt = up(j + 1)
        z_s[:, cols(j)] = (gb * _conv3(gc * u, cw_ref[:, cols(j)], cb_ref[:, cols(j)], tm)).astype(BF16)
    y = jnp.dot(z_s[...], wout_ref[...].astype(BF16), preferred_element_type=F32)
    o_ref[...] = xm[...] + gate_m[pl.ds(b, 1), :] * y


def _mix(x2d, mod, nw, win, cw, cb, wout, casts, *, tm, tpb):
    n_rows = x2d.shape[0]
    width = wout.shape[0]
    ext = tm + 2 * HALO
    cast_in, cast_out, cast_shapes = _rider_specs(casts, (n_rows // tm,))
    kern = functools.partial(_mix_kernel, tm=tm, tpb=tpb, n_cast=len(casts))
    return pl.pallas_call(
        kern,
        grid=(n_rows // tm,),
        in_specs=(_halo_specs(tm, D_MODEL, n_rows)
                  + [_mod_spec(0), _mod_spec(1), _mod_spec(2), _const_spec((1, D_MODEL)),
                     _const_spec((D_MODEL, 3 * width)), _const_spec((3, width)),
                     _const_spec((1, width)), _const_spec((width, D_MODEL))] + cast_in),
        out_specs=[pl.BlockSpec((tm, D_MODEL), lambda i: (i, 0))] + cast_out,
        out_shape=[jax.ShapeDtypeStruct((n_rows, D_MODEL), F32)] + cast_shapes,
        scratch_shapes=[pltpu.VMEM((ext, D_MODEL), BF16), pltpu.VMEM((tm, width), BF16)],
        compiler_params=_cparams(("arbitrary",)),
        name="mix",
    )(x2d, x2d, x2d, mod, mod, mod, nw, win, cw, cb, wout, *[w for w, _ in casts])


def kernel(x, c, ctx, c_ctx, ada_w, ada_b, norm_mix_w, norm_ffn_w, ev_w_in, ev_w_out, na_rpb, hg_lb_logits,
           hg_norm_w, od_w_in, od_conv_w, od_conv_b, od_w_out, ffn_w_up, ffn_conv_w, ffn_conv_b, ffn_w_down,
           final_norm_w):
    batch, s_len, d = x.shape
    t_len = ctx.shape[1]
    n_rows = batch * s_len
    tpb = s_len // TM
    x2d = x.reshape(n_rows, d)
    ctx2d = ctx.reshape(batch * t_len, d)

    c_stack = jnp.concatenate([c, c_ctx[None], jnp.zeros((8 - batch - 1, d), F32)], axis=0)
    mods = _ada(c_stack, ada_w, ada_b, batch + 1)

    w_in0 = ev_w_in[0]
    nw_mix = norm_mix_w.reshape(-1, 1, d)
    nw_ffn = norm_ffn_w.reshape(-1, 1, d)
    p_lat, lf_lat = _proj(x2d, nw_mix[0], mods[0], 0, 1, w_in0, hg_lb_logits, tm=TM, tpb=tpb, fixed_row=None)
    p_ctx, lf_ctx = _proj(ctx2d, nw_mix[0], mods[0], 0, 1, w_in0, hg_lb_logits, tm=t_len, tpb=1,
                          fixed_row=batch)
    a_lat, wup0 = _na(p_lat, p_ctx, _na_bias_rows(na_rpb[0]), [(ffn_w_up, 0)],
                      batch=batch, s_len=s_len, t_len=t_len)
    g_lat, wo0, wdown0, wdown1 = _hgrn(p_lat, lf_lat, p_ctx, lf_ctx, hg_norm_w[0].reshape(1, HG_WIDTH),
                                       [(ev_w_out, 0), (ffn_w_down, 0), (ffn_w_down, 1)],
                                       batch=batch, s_len=s_len, t_len=t_len)
    x1 = _ffn0(x2d, a_lat, g_lat, mods[0], nw_ffn[0], wo0.reshape(2, NA_WIDTH, d),
               wup0, ffn_conv_w[0], ffn_conv_b[0][None], wdown0, tm=TM_FUSED, tpb=s_len // TM_FUSED)

    x2, wup1 = _mix(x1, mods[1], nw_mix[1], od_w_in[0], od_conv_w[0], od_conv_b[0][None], od_w_out[0],
                    [(ffn_w_up, 1)], tm=TM, tpb=tpb)
    out = _ffn1(x2, mods[1], nw_ffn[1], final_norm_w.reshape(1, d),
                wup1, ffn_conv_w[1], ffn_conv_b[1][None], wdown1, tm=TM, tpb=tpb)
    return out.reshape(batch, s_len, d)
```

```python
import functools

import numpy as np
import jax
import jax.numpy as jnp
from jax import lax
from jax.experimental import pallas as pl
from jax.experimental.pallas import tpu as pltpu

F32 = jnp.float32
BF16 = jnp.bfloat16

D_MODEL = 1024
GRID_W = 64
EPS = 1e-6
NA_HEADS = 8
NA_HEAD_DIM = 64
NA_WIDTH = NA_HEADS * NA_HEAD_DIM
NA_KH = 8
NA_KW = 16
HG_HEADS = 4

V7X_VMEM_BYTES = 64 * 1024 * 1024
VMEM_LIMIT = V7X_VMEM_BYTES - 8 * 1024 * 1024
LANES = 128
SUBLANES = 8
HALO = 16
HG_WIDTH = HG_HEADS * LANES
NEG = -1e30
LOG2E = float(np.log2(np.e))

NA_QROWS = 4
NA_KROWS = NA_QROWS + NA_KH
NA_UNROLL = 8
HG_CHUNK = 64
HG_UNROLL = 32
HG_READOUT_ROWS = 512
FF_CHUNK = 256
ADA_COLS = 1536
PROJ_SRC = (0, 1, 2, 3, 6, 7, 4, 5)
PROJ_HQ = 3
PROJ_FF = 6
TM = 1024


def _cparams(sem):
    return pltpu.CompilerParams(dimension_semantics=sem, vmem_limit_bytes=VMEM_LIMIT)


def _rms_mod(x, nw, shift, scale):
    ms = jnp.mean(x * x, axis=-1, keepdims=True)
    return (x * lax.rsqrt(ms + EPS)) * (nw * (1.0 + scale)) + shift


def _ada_kernel(ct_ref, w_ref, b_ref, o_ref, *, n_cond):
    s = jax.nn.silu(ct_ref[...])
    o_ref[0] = jnp.zeros(o_ref.shape[1:], F32)
    for r in range(n_cond):
        o_ref[0, r:r + 1, :] = jnp.sum(s[:, r:r + 1] * w_ref[0], axis=0, keepdims=True) + b_ref[0]


def _ada(c_stack, ada_w, ada_b, n_cond):
    depth, d, n6 = ada_w.shape
    tn = ADA_COLS
    return pl.pallas_call(
        functools.partial(_ada_kernel, n_cond=n_cond),
        grid=(depth, n6 // tn),
        in_specs=[
            pl.BlockSpec((d, SUBLANES), lambda l, j: (0, 0)),
            pl.BlockSpec((1, d, tn), lambda l, j: (l, 0, j)),
            pl.BlockSpec((1, 1, tn), lambda l, j: (l, 0, j)),
        ],
        out_specs=pl.BlockSpec((1, SUBLANES, tn), lambda l, j: (l, 0, j)),
        out_shape=jax.ShapeDtypeStruct((depth, SUBLANES, n6), F32),
        compiler_params=_cparams(("arbitrary", "arbitrary")),
        name="ada",
    )(c_stack.T, ada_w, ada_b.reshape(depth, 1, n6))


def _forget_bounds(lbl_ref):
    ll = lbl_ref[...]
    mx = jnp.maximum(ll[0], ll[1])
    e0 = jnp.exp(ll[0] - mx)
    lb = e0 / (e0 + jnp.exp(ll[1] - mx))
    return 0.5 * (1.0 + lb), 0.5 * (1.0 - lb)


def _proj_kernel(x_ref, nw_ref, shift_ref, scale_ref, w_ref, lbl_ref, o_ref, lf_ref, h_s, *, tpb, fixed_row):
    if fixed_row is None:
        row = pl.program_id(0) // tpb
    else:
        row = fixed_row
    h = _rms_mod(x_ref[...], nw_ref[...], shift_ref[pl.ds(row, 1), :], scale_ref[pl.ds(row, 1), :])
    h_s[...] = h.astype(BF16)
    f_mid, f_amp = _forget_bounds(lbl_ref)
    ncol = HG_WIDTH
    nblk = w_ref.shape[1] // ncol
    for blk in sorted(range(nblk), key=lambda k: (k < PROJ_FF and k != PROJ_HQ, k)):
        cols = slice(blk * ncol, (blk + 1) * ncol)
        src = PROJ_SRC[blk]
        y = jnp.dot(h_s[...], w_ref[:, src * ncol:(src + 1) * ncol].astype(BF16), preferred_element_type=F32)
        if blk == 0:
            y = y * np.float32(LOG2E * NA_HEAD_DIM ** -0.5)
        if blk == PROJ_HQ:
            half = 0.5 * y
            y = half + half * jnp.tanh(half)
        if blk < PROJ_FF:
            o_ref[:, cols] = y.astype(o_ref.dtype)
        else:
            d = blk - PROJ_FF
            fg = f_mid[d:d + 1] + f_amp[d:d + 1] * jnp.tanh(0.5 * y)
            lf_ref[:, d * ncol:(d + 1) * ncol] = jnp.log(fg) * np.float32(1.0 / np.log(2.0))


def _proj(x2d, nw, mod, shift_blk, scale_blk, w, lb_logits, *, tm, tpb, fixed_row):
    n_rows, d = x2d.shape
    n = w.shape[1]
    n_bf = PROJ_FF * HG_WIDTH
    kern = functools.partial(_proj_kernel, tpb=tpb, fixed_row=fixed_row)
    return pl.pallas_call(
        kern,
        grid=(n_rows // tm,),
        in_specs=[
            pl.BlockSpec((tm, d), lambda i: (i, 0)),
            pl.BlockSpec((1, d), lambda i: (0, 0)),
            pl.BlockSpec((SUBLANES, d), lambda i: (0, shift_blk)),
            pl.BlockSpec((SUBLANES, d), lambda i: (0, scale_blk)),
            pl.BlockSpec((d, n), lambda i: (0, 0), pipeline_mode=pl.Buffered(1)),
            pl.BlockSpec(lb_logits.shape, lambda i: (0, 0, 0)),
        ],
        out_specs=[pl.BlockSpec((tm, n_bf), lambda i: (i, 0)),
                   pl.BlockSpec((tm, n - n_bf), lambda i: (i, 0))],
        out_shape=[jax.ShapeDtypeStruct((n_rows, n_bf), BF16),
                   jax.ShapeDtypeStruct((n_rows, n - n_bf), F32)],
        scratch_shapes=[pltpu.VMEM((tm, d), BF16)],
        compiler_params=_cparams(("arbitrary",)),
        name="proj",
    )(x2d, nw, mod, mod, w, lb_logits)


def _na_bias_rows(rpb):
    h, nr, nc = rpb.shape
    return jnp.pad(rpb.astype(F32) * LOG2E, ((0, 0), (0, 2 * NA_KH - nr), (0, LANES - nc)))


def _na_build_bands(rows_ref, band_ref):
    qc = lax.broadcasted_iota(jnp.int32, (GRID_W, LANES), 0)
    lane = lax.broadcasted_iota(jnp.int32, (GRID_W, LANES), 1)
    kc = jnp.where(lane < GRID_W, lane, lane - GRID_W)
    start = jnp.clip(qc - NA_KW // 2, 0, GRID_W - NA_KW)
    valid = (kc >= start) & (kc < start + NA_KW)
    masked = jnp.full((GRID_W, LANES), NEG, F32)
    for hh in range(band_ref.shape[0]):
        for a in range(2 * NA_KH - 1):
            x = jnp.broadcast_to(rows_ref[hh, a:a + 1, :], (GRID_W, LANES))
            lo = pltpu.roll(x, LANES - (NA_KW - 1), 1, stride=1, stride_axis=0)
            hi = pltpu.roll(x, LANES - (NA_KW - 1) - GRID_W, 1, stride=1, stride_axis=0)
            band_ref[hh, a] = jnp.where(valid, jnp.where(lane < GRID_W, lo, hi), masked)
        band_ref[hh, 2 * NA_KH - 1] = masked


def _na_block_index(rows):
    masked = 2 * NA_KH - 1

    def block_idx(rb):
        kr0 = int(np.clip(NA_QROWS * rb - NA_KH // 2, 0, rows - NA_KROWS))
        idx = np.empty((NA_QROWS, NA_KROWS), np.int32)
        for qi in range(NA_QROWS):
            qr = NA_QROWS * rb + qi
            r0 = int(np.clip(qr - NA_KH // 2, 0, rows - NA_KH))
            for kj in range(NA_KROWS):
                kr = kr0 + kj
                idx[qi, kj] = kr - qr + (NA_KH - 1) if r0 <= kr < r0 + NA_KH else masked
        return idx

    nblk = rows // NA_QROWS
    interior = block_idx(1)
    assert all(np.array_equal(block_idx(rb), interior) for rb in range(1, nblk - 1))
    return np.stack([block_idx(0), interior, block_idx(nblk - 1)])


def _na_kernel(q_ref, k_ref, v_ref, kc_ref, vc_ref, rows_ref, *rest, n_cast):
    o_ref = _cast_riders(rest, n_cast)
    bias_ref, band_ref = rest[2 * n_cast + 1:]
    s_len = q_ref.shape[0]
    qb = NA_QROWS * GRID_W
    kb = NA_KROWS * GRID_W
    nblk = s_len // qb
    rows = s_len // GRID_W
    _na_build_bands(rows_ref, band_ref)

    idx = _na_block_index(rows)
    left = lax.broadcasted_iota(jnp.int32, (GRID_W, LANES), 1) < GRID_W
    for hh in range(2):
        for t in range(3):
            for qi in range(NA_QROWS):
                for j in range(NA_KROWS // 2):
                    a, b = int(idx[t, qi, 2 * j]), int(idx[t, qi, 2 * j + 1])
                    tile = band_ref[hh, a] if a == b else jnp.where(left, band_ref[hh, a], band_ref[hh, b])
                    bias_ref[hh, t, qi * GRID_W:(qi + 1) * GRID_W, j * LANES:(j + 1) * LANES] = tile

    lane = lax.broadcasted_iota(jnp.int32, (qb, LANES), 1)
    in_head = [(lane >= NA_HEAD_DIM * hh) & (lane < NA_HEAD_DIM * (hh + 1)) for hh in range(2)]
    dn_t = (((1,), (1,)), ((), ()))

    def head_values(v, hh):
        return jnp.concatenate([jnp.where(in_head[hh], v[r:r + qb], jnp.ones_like(v[r:r + qb]))
                                for r in range(0, v.shape[0], qb)], axis=0)

    kc = kc_ref[...]
    vc = [head_values(vc_ref[...], hh) for hh in range(2)]

    def body(it, carry):
        blocks = []
        for u in range(NA_UNROLL):
            rb = it * NA_UNROLL + u
            kr0 = jnp.clip(NA_QROWS * rb - NA_KH // 2, 0, rows - NA_KROWS)
            ks = pl.multiple_of(kr0 * GRID_W, qb)
            qs = pl.multiple_of(rb * qb, qb)
            btype = jnp.where(rb == 0, 0, jnp.where(rb == nblk - 1, 2, 1))
            blocks.append((qs, k_ref[pl.ds(ks, kb), :], v_ref[pl.ds(ks, kb), :], q_ref[pl.ds(qs, qb), :], btype))
        logits = []
        for qs, kw, vw, q, btype in blocks:
            q2 = jnp.concatenate([jnp.where(in_head[hh], q, jnp.zeros_like(q)) for hh in range(2)], axis=0)
            s_loc = lax.dot_general(q2, kw, dn_t, preferred_element_type=F32)
            s_ctx = lax.dot_general(q2, kc, dn_t, preferred_element_type=F32)
            for hh in range(2):
                logits.append((s_loc[hh * qb:(hh + 1) * qb] + bias_ref[hh, btype], s_ctx[hh * qb:(hh + 1) * qb]))
        probs = []
        for s_loc, s_ctx in logits:
            m = jnp.maximum(jnp.max(s_loc, axis=-1, keepdims=True), jnp.max(s_ctx, axis=-1, keepdims=True))
            probs.append((jnp.exp2(s_loc - m).astype(BF16), jnp.exp2(s_ctx - m).astype(BF16)))
        outs = []
        for i, (p_loc, p_ctx) in enumerate(probs):
            hh = i % 2
            o = (jnp.dot(p_loc, head_values(blocks[i // 2][2], hh), preferred_element_type=F32)
                 + jnp.dot(p_ctx, vc[hh], preferred_element_type=F32))
            outs.append(o / pltpu.roll(o, NA_HEAD_DIM, 1))
        for u, blk in enumerate(blocks):
            o_ref[pl.ds(blk[0], qb), :] = jnp.where(in_head[0], outs[2 * u], outs[2 * u + 1]).astype(o_ref.dtype)
        return carry

    lax.fori_loop(0, nblk // NA_UNROLL, body, 0)


def _cast_riders(rest, n_cast):
    for w_ref, w16_ref in zip(rest[:n_cast], rest[n_cast + 1:2 * n_cast + 1]):
        w16_ref[...] = w_ref[...].astype(BF16)
    return rest[n_cast]


def _rider_specs(casts, grid):
    steps = int(np.prod(grid))

    def step(*idx):
        flat = idx[0]
        for extent, i in zip(grid[1:], idx[1:]):
            flat = flat * extent + i
        return flat

    cast_in, cast_out, cast_shapes = [], [], []
    for w, layer in casts:
        _, rows, cols = w.shape
        blk_rows = rows // steps
        assert blk_rows * steps == rows and blk_rows % HALO == 0
        cast_in.append(pl.BlockSpec((None, blk_rows, cols), lambda *idx, layer=layer: (layer, step(*idx), 0)))
        cast_out.append(pl.BlockSpec((blk_rows, cols), lambda *idx: (step(*idx), 0)))
        cast_shapes.append(jax.ShapeDtypeStruct((rows, cols), BF16))
    return cast_in, cast_out, cast_shapes


def _na(p_lat, p_ctx, band, casts, *, batch, s_len, t_len):
    hp = NA_HEADS // 2
    kblk = NA_WIDTH // LANES
    qb = NA_QROWS * GRID_W
    kbk = NA_KROWS * GRID_W
    cast_in, cast_out, cast_shapes = _rider_specs(casts, (batch, hp))
    return pl.pallas_call(
        functools.partial(_na_kernel, n_cast=len(casts)),
        grid=(batch, hp),
        in_specs=[
            pl.BlockSpec((s_len, LANES), lambda b, h: (b, h)),
            pl.BlockSpec((s_len, LANES), lambda b, h: (b, kblk + h)),
            pl.BlockSpec((s_len, LANES), lambda b, h: (b, 2 * kblk + h)),
            pl.BlockSpec((t_len, LANES), lambda b, h: (b, kblk + h)),
            pl.BlockSpec((t_len, LANES), lambda b, h: (b, 2 * kblk + h)),
            pl.BlockSpec((2, 2 * NA_KH, LANES), lambda b, h: (h, 0, 0)),
        ] + cast_in,
        out_specs=[pl.BlockSpec((s_len, LANES), lambda b, h: (b, h))] + cast_out,
        out_shape=[jax.ShapeDtypeStruct((batch * s_len, NA_WIDTH), BF16)] + cast_shapes,
        scratch_shapes=[pltpu.VMEM((2, 3, qb, kbk), F32), pltpu.VMEM((2, 2 * NA_KH, GRID_W, LANES), F32)],
        compiler_params=_cparams(("arbitrary", "arbitrary")),
        name="na",
    )(p_lat, p_lat, p_lat, p_ctx, p_ctx, band, *[w for w, _ in casts])


def _hg_kernel(q_ref, ff_ref, fb_ref, i_ref, g_ref, cff_ref, cfb_ref, ci_ref, nw_ref, *rest, n_cast):
    o_ref = _cast_riders(rest, n_cast)
    acc_ref = rest[2 * n_cast + 1]
    c = HG_CHUNK
    s_len = q_ref.shape[0]
    t_len = ci_ref.shape[0]
    n = s_len // c
    dn_t = (((1,), (1,)), ((), ()))

    def gates(lf):
        return 1.0 - jnp.exp2(lf), lf

    def masked_sum(mask, lf):
        hi = lf.astype(BF16)
        lo = (lf - hi.astype(F32)).astype(BF16)
        r = jnp.dot(mask, jnp.concatenate([hi, lo], axis=1), preferred_element_type=F32)
        return r[:, :LANES] + r[:, LANES:]

    r_t = lax.broadcasted_iota(jnp.int32, (t_len, t_len), 0)
    c_t = lax.broadcasted_iota(jnp.int32, (t_len, t_len), 1)
    vct = ci_ref[...].astype(F32).T.astype(BF16)
    states = []
    for fref, after in ((cff_ref, c_t > r_t), (cfb_ref, c_t < r_t)):
        kk, lf = gates(fref[...])
        kd = (kk * jnp.exp2(masked_sum(after.astype(BF16), lf))).astype(BF16)
        states.append(jnp.dot(vct, kd, preferred_element_type=F32))

    r_c = lax.broadcasted_iota(jnp.int32, (c, c), 0)
    c_c = lax.broadcasted_iota(jnp.int32, (c, c), 1)
    tri = (c_c <= r_c, c_c >= r_c)
    tri_b = tuple(t.astype(BF16) for t in tri)

    def cum(d, lf):
        return masked_sum(tri_b[d], lf)

    acc_ref[...] = jnp.zeros_like(acc_ref)

    def body(jj, carry):
        chains = []
        for u in range(HG_UNROLL):
            j = jj * HG_UNROLL + u
            chains.append((0, pl.multiple_of(j * c, c)))
            chains.append((1, pl.multiple_of((n - 1 - j) * c, c)))
        pre = []
        for d, start in chains:
            rows = pl.ds(start, c)
            kk, lf = gates((ff_ref, fb_ref)[d][rows, :])
            pre.append((kk, lf, q_ref[rows, :].astype(F32), i_ref[rows, :]))
        bcs = [cum(d, p[1]) for (d, _), p in zip(chains, pre)]
        ops = []
        for (d, _), (kk, lf, qh, v), bc in zip(chains, pre, bcs):
            btot = bc[c - 1:c] if d == 0 else bc[0:1]
            mref = bc[c // 2:c // 2 + 1]
            qm = qh * jnp.exp2(bc - mref)
            km = kk * jnp.exp2(mref - bc)
            qe = (qm * jnp.exp2(mref)).astype(BF16)
            kd = (km * jnp.exp2(btot - mref)).astype(BF16)
            ops.append((qe, qm.astype(BF16), km.astype(BF16), kd, jnp.exp2(btot), v))
        atts = [jnp.where(tri[d], lax.dot_general(qm, km, dn_t, preferred_element_type=F32), 0.0).astype(BF16)
                for (d, _), (qe, qm, km, kd, eb, v) in zip(chains, ops)]
        upds = [lax.dot_general(v, kd, (((0,), (0,)), ((), ())), preferred_element_type=F32)
                for (qe, qm, km, kd, eb, v) in ops]
        intra = [jnp.dot(att, o[5], preferred_element_type=F32) for att, o in zip(atts, ops)]
        st = list(carry)
        for idx, (d, start) in enumerate(chains):
            qe, eb = ops[idx][0], ops[idx][4]
            o = intra[idx] + lax.dot_general(qe, st[d].astype(BF16), dn_t, preferred_element_type=F32)
            st[d] = eb * st[d] + upds[idx]
            acc_ref[pl.ds(start, c), :] += o
        return st[0], st[1]

    lax.fori_loop(0, n // HG_UNROLL, body, (states[0], states[1]))

    rb = HG_READOUT_ROWS
    nw = nw_ref[...]

    def readout(j, carry):
        rows = pl.ds(pl.multiple_of(j * rb, rb), rb)
        o = acc_ref[rows, :]
        o = o * lax.rsqrt(jnp.mean(o * o, axis=-1, keepdims=True) + EPS) * nw
        half = 0.5 * g_ref[rows, :].astype(F32)
        o_ref[rows, :] = (o * (half + half * jnp.tanh(half))).astype(o_ref.dtype)
        return carry

    lax.fori_loop(0, s_len // rb, readout, 0)


def _hgrn(p_lat, lf_lat, p_ctx, lf_ctx, norm_w, casts, *, batch, s_len, t_len):
    nh = HG_HEADS
    per = HG_WIDTH // LANES
    cast_in, cast_out, cast_shapes = _rider_specs(casts, (batch, nh))

    def blk(rows, group):
        return pl.BlockSpec((rows, LANES), lambda b, h: (b, group * per + h))

    return pl.pallas_call(
        functools.partial(_hg_kernel, n_cast=len(casts)),
        grid=(batch, nh),
        in_specs=[blk(s_len, PROJ_HQ), blk(s_len, 0), blk(s_len, 1), blk(s_len, PROJ_HQ + 1),
                  blk(s_len, PROJ_HQ + 2), blk(t_len, 0), blk(t_len, 1), blk(t_len, PROJ_HQ + 1),
                  pl.BlockSpec((1, LANES), lambda b, h: (0, h))] + cast_in,
        out_specs=[pl.BlockSpec((s_len, LANES), lambda b, h: (b, h))] + cast_out,
        out_shape=[jax.ShapeDtypeStruct((batch * s_len, HG_WIDTH), BF16)] + cast_shapes,
        scratch_shapes=[pltpu.VMEM((s_len, LANES), F32)],
        compiler_params=_cparams(("arbitrary", "arbitrary")),
        name="hgrn",
    )(p_lat, lf_lat, lf_lat, p_lat, p_lat, lf_ctx, lf_ctx, p_ctx, norm_w, *[w for w, _ in casts])


def _halo_specs(tm, width, n_rows):
    per = tm // HALO
    last = n_rows // HALO - 1
    return [
        pl.BlockSpec((tm, width), lambda i: (i, 0)),
        pl.BlockSpec((HALO, width), lambda i: (jnp.maximum(i * per - 1, 0), 0)),
        pl.BlockSpec((HALO, width), lambda i: (jnp.minimum((i + 1) * per, last), 0)),
    ]


def _assemble(dst, main, prev, nxt, tm):
    dst[0:HALO, :] = prev[...]
    dst[HALO:HALO + tm, :] = main[...]
    dst[HALO + tm:, :] = nxt[...]


def _store_rows(he_s, h_prev, h_main, h_next, tm, tpb):
    t = pl.program_id(0) % tpb
    keep_prev = jnp.where(t != 0, 1.0, 0.0)
    keep_next = jnp.where(t != tpb - 1, 1.0, 0.0)
    he_s[0:HALO, :] = (h_prev * keep_prev).astype(BF16)
    he_s[HALO:HALO + tm, :] = h_main.astype(BF16)
    he_s[HALO + tm:, :] = (h_next * keep_next).astype(BF16)


def _store_mod_rows(he_s, xm, xp, xn, nw, shift, scale, tm, tpb):
    _store_rows(he_s, _rms_mod(xp[...], nw, shift, scale), _rms_mod(xm[...], nw, shift, scale),
                _rms_mod(xn[...], nw, shift, scale), tm, tpb)


def _conv3(a_ext, cw, cb, tm):
    rows = tm + 2 * HALO
    prev = pltpu.roll(a_ext, 1, 0)[HALO:HALO + tm]
    nxt = pltpu.roll(a_ext, rows - 1, 0)[HALO:HALO + tm]
    cur = a_ext[HALO:HALO + tm]
    return prev * cw[0:1] + cur * cw[1:2] + nxt * cw[2:3] + cb


def _gelu_exact(x):
    return 0.5 * x * (1.0 + lax.erf(x * np.float32(np.sqrt(0.5))))


def _convglu(he_s, act_s, wup_ref, cw_ref, cb_ref, wdown_ref, tm):
    d_ff = wdown_ref.shape[0]
    nchunk = d_ff // FF_CHUNK

    def cols(j, base=0):
        return slice(base + j * FF_CHUNK, base + (j + 1) * FF_CHUNK)

    def up(j):
        a = jnp.dot(he_s[...], wup_ref[:, cols(j)], preferred_element_type=F32)
        val = jnp.dot(he_s[HALO:HALO + tm, :], wup_ref[:, cols(j, d_ff)], preferred_element_type=F32)
        return a, val

    nxt = up(0)
    for j in range(nchunk):
        a, val = nxt
        if j + 1 < nchunk:
            nxt = up(j + 1)
        cv = _conv3(a, cw_ref[:, cols(j)], cb_ref[:, cols(j)], tm)
        act_s[:, cols(j)] = (_gelu_exact(cv) * val).astype(BF16)
    return jnp.dot(act_s[...], wdown_ref[...], preferred_element_type=F32)


def _ffn0_kernel(xm, xp, xn, am, ap, an, gm, gp, gn, gate_m, shift_f, scale_f, gate_f, nw,
                 wo_ref, wup_ref, cw_ref, cb_ref, wdown_ref, o_ref,
                 xe_s, ae_s, ge_s, he_s, act_s, *, tm, tpb):
    b = pl.program_id(0) // tpb
    _assemble(xe_s, xm, xp, xn, tm)
    _assemble(ae_s, am, ap, an, tm)
    _assemble(ge_s, gm, gp, gn, tm)
    y = (jnp.dot(ae_s[...], wo_ref[0], preferred_element_type=F32)
         + jnp.dot(ge_s[...], wo_ref[1], preferred_element_type=F32))
    x1 = xe_s[...] + gate_m[pl.ds(b, 1), :] * y
    xe_s[...] = x1
    h = _rms_mod(x1, nw[...], shift_f[pl.ds(b, 1), :], scale_f[pl.ds(b, 1), :])
    _store_rows(he_s, h[0:HALO], h[HALO:HALO + tm], h[HALO + tm:], tm, tpb)
    y2 = _convglu(he_s, act_s, wup_ref, cw_ref, cb_ref, wdown_ref, tm)
    o_ref[...] = xe_s[HALO:HALO + tm, :] + gate_f[pl.ds(b, 1), :] * y2


def _ffn1_kernel(xm, xp, xn, shift_f, scale_f, gate_f, nw, fnw,
                 wup_ref, cw_ref, cb_ref, wdown_ref, o_ref,
                 he_s, act_s, *, tm, tpb):
    b = pl.program_id(0) // tpb
    _store_mod_rows(he_s, xm, xp, xn, nw[...], shift_f[pl.ds(b, 1), :], scale_f[pl.ds(b, 1), :], tm, tpb)
    y2 = _convglu(he_s, act_s, wup_ref, cw_ref, cb_ref, wdown_ref, tm)
    x2 = xm[...] + gate_f[pl.ds(b, 1), :] * y2
    ms = jnp.mean(x2 * x2, axis=-1, keepdims=True)
    o_ref[...] = x2 * lax.rsqrt(ms + EPS) * fnw[...]


def _mod_spec(k):
    return pl.BlockSpec((SUBLANES, D_MODEL), lambda i: (0, k))


def _const_spec(shape):
    nd = len(shape)
    return pl.BlockSpec(shape, lambda i: (0,) * nd, pipeline_mode=pl.Buffered(1))


def _ffn_weight_specs(d_ff):
    return [
        _const_spec((D_MODEL, 2 * d_ff)),
        _const_spec((3, d_ff)),
        _const_spec((1, d_ff)),
        _const_spec((d_ff, D_MODEL)),
    ]


def _ffn0(x2d, a, g, mod, nw, wo, wup, cw, cb, wdown, *, tm, tpb):
    n_rows = x2d.shape[0]
    d_ff = wdown.shape[0]
    ext = tm + 2 * HALO
    kern = functools.partial(_ffn0_kernel, tm=tm, tpb=tpb)
    return pl.pallas_call(
        kern,
        grid=(n_rows // tm,),
        in_specs=(_halo_specs(tm, D_MODEL, n_rows) + _halo_specs(tm, NA_WIDTH, n_rows)
                  + _halo_specs(tm, HG_WIDTH, n_rows)
                  + [_mod_spec(2), _mod_spec(3), _mod_spec(4), _mod_spec(5),
                     _const_spec((1, D_MODEL)), _const_spec((2, NA_WIDTH, D_MODEL))]
                  + _ffn_weight_specs(d_ff)),
        out_specs=pl.BlockSpec((tm, D_MODEL), lambda i: (i, 0)),
        out_shape=jax.ShapeDtypeStruct((n_rows, D_MODEL), F32),
        scratch_shapes=[pltpu.VMEM((ext, D_MODEL), F32), pltpu.VMEM((ext, NA_WIDTH), BF16),
                        pltpu.VMEM((ext, HG_WIDTH), BF16), pltpu.VMEM((ext, D_MODEL), BF16),
                        pltpu.VMEM((tm, d_ff), BF16)],
        compiler_params=_cparams(("arbitrary",)),
        name="ffn0",
    )(x2d, x2d, x2d, a, a, a, g, g, g, mod, mod, mod, mod, nw, wo, wup, cw, cb, wdown)


def _ffn1(x2d, mod, nw, fnw, wup, cw, cb, wdown, *, tm, tpb):
    n_rows = x2d.shape[0]
    d_ff = wdown.shape[0]
    ext = tm + 2 * HALO
    kern = functools.partial(_ffn1_kernel, tm=tm, tpb=tpb)
    weight_specs = _ffn_weight_specs(d_ff)
    return pl.pallas_call(
        kern,
        grid=(n_rows // tm,),
        in_specs=(_halo_specs(tm, D_MODEL, n_rows)
                  + [_mod_spec(3), _mod_spec(4), _mod_spec(5),
                     _const_spec((1, D_MODEL)), _const_spec((1, D_MODEL))]
                  + weight_specs),
        out_specs=pl.BlockSpec((tm, D_MODEL), lambda i: (i, 0)),
        out_shape=jax.ShapeDtypeStruct((n_rows, D_MODEL), F32),
        scratch_shapes=[pltpu.VMEM((ext, D_MODEL), BF16), pltpu.VMEM((tm, d_ff), BF16)],
        compiler_params=_cparams(("arbitrary",)),
        name="ffn1",
    )(x2d, x2d, x2d, mod, mod, mod, nw, fnw, wup, cw, cb, wdown)


def _mix_kernel(xm, xp, xn, shift_m, scale_m, gate_m, nw, win_ref, cw_ref, cb_ref, wout_ref, *rest,
                tm, tpb, n_cast):
    o_ref = _cast_riders(rest, n_cast)
    he_s, z_s = rest[2 * n_cast + 1:]
    b = pl.program_id(0) // tpb
    width = wout_ref.shape[0]
    nchunk = width // FF_CHUNK
    _store_mod_rows(he_s, xm, xp, xn, nw[...], shift_m[pl.ds(b, 1), :], scale_m[pl.ds(b, 1), :], tm, tpb)

    def cols(j, base=0):
        return slice(base + j * FF_CHUNK, base + (j + 1) * FF_CHUNK)

    def w_in(j, base):
        return win_ref[:, cols(j, base)].astype(BF16)

    def up(j):
        gb = jnp.dot(he_s[HALO:HALO + tm, :], w_in(j, 0), preferred_element_type=F32)
        gc = jnp.dot(he_s[...], w_in(j, width), preferred_element_type=F32)
        u = jnp.dot(he_s[...], w_in(j, 2 * width), preferred_element_type=F32)
        return gb, gc, u

    nxt = up(0)
    for j in range(nchunk):
        gb, gc, u = nxt
        if j + 1 < nchunk:
            nxt = up(j + 1)
        z_s[:, cols(j)] = (gb * _conv3(gc * u, cw_ref[:, cols(j)], cb_ref[:, cols(j)], tm)).astype(BF16)
    y = jnp.dot(z_s[...], wout_ref[...].astype(BF16), preferred_element_type=F32)
    o_ref[...] = xm[...] + gate_m[pl.ds(b, 1), :] * y


def _mix(x2d, mod, nw, win, cw, cb, wout, casts, *, tm, tpb):
    n_rows = x2d.shape[0]
    width = wout.shape[0]
    ext = tm + 2 * HALO
    cast_in, cast_out, cast_shapes = _rider_specs(casts, (n_rows // tm,))
    kern = functools.partial(_mix_kernel, tm=tm, tpb=tpb, n_cast=len(casts))
    return pl.pallas_call(
        kern,
        grid=(n_rows // tm,),
        in_specs=(_halo_specs(tm, D_MODEL, n_rows)
                  + [_mod_spec(0), _mod_spec(1), _mod_spec(2), _const_spec((1, D_MODEL)),
                     _const_spec((D_MODEL, 3 * width)), _const_spec((3, width)),
                     _const_spec((1, width)), _const_spec((width, D_MODEL))] + cast_in),
        out_specs=[pl.BlockSpec((tm, D_MODEL), lambda i: (i, 0))] + cast_out,
        out_shape=[jax.ShapeDtypeStruct((n_rows, D_MODEL), F32)] + cast_shapes,
        scratch_shapes=[pltpu.VMEM((ext, D_MODEL), BF16), pltpu.VMEM((tm, width), BF16)],
        compiler_params=_cparams(("arbitrary",)),
        name="mix",
    )(x2d, x2d, x2d, mod, mod, mod, nw, win, cw, cb, wout, *[w for w, _ in casts])


def kernel(x, c, ctx, c_ctx, ada_w, ada_b, norm_mix_w, norm_ffn_w, ev_w_in, ev_w_out, na_rpb, hg_lb_logits,
           hg_norm_w, od_w_in, od_conv_w, od_conv_b, od_w_out, ffn_w_up, ffn_conv_w, ffn_conv_b, ffn_w_down,
           final_norm_w):
    batch, s_len, d = x.shape
    t_len = ctx.shape[1]
    n_rows = batch * s_len
    tpb = s_len // TM
    x2d = x.reshape(n_rows, d)
    ctx2d = ctx.reshape(batch * t_len, d)

    c_stack = jnp.concatenate([c, c_ctx[None], jnp.zeros((SUBLANES - batch - 1, d), F32)], axis=0)
    mods = _ada(c_stack, ada_w, ada_b, batch + 1)

    w_in0 = ev_w_in[0]
    nw_mix = norm_mix_w.reshape(-1, 1, d)
    nw_ffn = norm_ffn_w.reshape(-1, 1, d)
    p_lat, lf_lat = _proj(x2d, nw_mix[0], mods[0], 0, 1, w_in0, hg_lb_logits, tm=TM, tpb=tpb, fixed_row=None)
    p_ctx, lf_ctx = _proj(ctx2d, nw_mix[0], mods[0], 0, 1, w_in0, hg_lb_logits, tm=t_len, tpb=1,
                          fixed_row=batch)
    a_lat, wup0 = _na(p_lat, p_ctx, _na_bias_rows(na_rpb[0]), [(ffn_w_up, 0)],
                      batch=batch, s_len=s_len, t_len=t_len)
    g_lat, wo0, wdown0, wdown1 = _hgrn(p_lat, lf_lat, p_ctx, lf_ctx, hg_norm_w[0].reshape(1, HG_WIDTH),
                                       [(ev_w_out, 0), (ffn_w_down, 0), (ffn_w_down, 1)],
                                       batch=batch, s_len=s_len, t_len=t_len)
    x1 = _ffn0(x2d, a_lat, g_lat, mods[0], nw_ffn[0], wo0.reshape(2, NA_WIDTH, d),
               wup0, ffn_conv_w[0], ffn_conv_b[0][None], wdown0, tm=TM, tpb=tpb)

    x2, wup1 = _mix(x1, mods[1], nw_mix[1], od_w_in[0], od_conv_w[0], od_conv_b[0][None], od_w_out[0],
                    [(ffn_w_up, 1)], tm=TM, tpb=tpb)
    out = _ffn1(x2, mods[1], nw_ffn[1], final_norm_w.reshape(1, d),
                wup1, ffn_conv_w[1], ffn_conv_b[1][None], wdown1, tm=TM, tpb=tpb)
    return out.reshape(batch, s_len, d)
```

```python
import functools

import numpy as np
import jax
import jax.numpy as jnp
from jax import lax
from jax.experimental import pallas as pl
from jax.experimental.pallas import tpu as pltpu

F32 = jnp.float32
BF16 = jnp.bfloat16

D_MODEL = 1024
GRID_W = 64
EPS = 1e-6
NA_HEADS = 8
NA_HEAD_DIM = 64
NA_WIDTH = NA_HEADS * NA_HEAD_DIM
NA_KH = 8
NA_KW = 16
HG_HEADS = 4

V7X_VMEM_BYTES = 64 * 1024 * 1024
VMEM_LIMIT = V7X_VMEM_BYTES - 8 * 1024 * 1024
LANES = 128
SUBLANES = 8
HALO = 16
HG_WIDTH = HG_HEADS * LANES
NEG = -1e30
LOG2E = float(np.log2(np.e))

NA_QROWS = 4
NA_KROWS = NA_QROWS + NA_KH
NA_UNROLL = 8
HG_CHUNK = 64
HG_UNROLL = 32
HG_READOUT_ROWS = 512
FF_CHUNK = 256
ADA_COLS = 1024
PROJ_SRC = (0, 1, 2, 3, 6, 7, 4, 5)
PROJ_HQ = 3
PROJ_FF = 6
TM = 1024


def _cparams(sem):
    return pltpu.CompilerParams(dimension_semantics=sem, vmem_limit_bytes=VMEM_LIMIT)


def _rms_mod(x, nw, shift, scale):
    ms = jnp.mean(x * x, axis=-1, keepdims=True)
    return (x * lax.rsqrt(ms + EPS)) * (nw * (1.0 + scale)) + shift


def _modulation(ct_ref, w_ref, b_ref, o_ref, n_cond):
    s = jax.nn.silu(ct_ref[...])
    o_ref[...] = jnp.zeros(o_ref.shape, F32)
    for r in range(n_cond):
        o_ref[r:r + 1, :] = jnp.sum(s[:, r:r + 1] * w_ref[...], axis=0, keepdims=True) + b_ref[...]


def _ada_kernel(ct_ref, w_ref, b_ref, o_ref, *, n_cond):
    _modulation(ct_ref, w_ref, b_ref, o_ref, n_cond)


def _ada(ct, ada_w, ada_b3, n_cond, layer, ncols):
    d = ada_w.shape[1]
    tn = ADA_COLS
    return pl.pallas_call(
        functools.partial(_ada_kernel, n_cond=n_cond),
        grid=(ncols // tn,),
        in_specs=[
            pl.BlockSpec((d, SUBLANES), lambda j: (0, 0)),
            pl.BlockSpec((None, d, tn), lambda j: (layer, 0, j)),
            pl.BlockSpec((None, 1, tn), lambda j: (layer, 0, j)),
        ],
        out_specs=pl.BlockSpec((SUBLANES, tn), lambda j: (0, j)),
        out_shape=jax.ShapeDtypeStruct((SUBLANES, ncols), F32),
        compiler_params=_cparams(("arbitrary",)),
        name="ada",
    )(ct, ada_w, ada_b3)


def _forget_bounds(lbl_ref):
    ll = lbl_ref[...]
    mx = jnp.maximum(ll[0], ll[1])
    e0 = jnp.exp(ll[0] - mx)
    lb = e0 / (e0 + jnp.exp(ll[1] - mx))
    return 0.5 * (1.0 + lb), 0.5 * (1.0 - lb)


def _proj_kernel(x_ref, nw_ref, shift_ref, scale_ref, w_ref, lbl_ref, *rest, tpb, fixed_row, n_jobs, n_cond):
    o_ref, lf_ref = rest[1 + 2 * n_jobs:3 + 2 * n_jobs] if n_jobs else rest[:2]
    h_s = rest[-1]
    for job in range(n_jobs):
        _modulation(rest[0], rest[1 + 2 * job], rest[2 + 2 * job], rest[3 + 2 * n_jobs + job], n_cond)
    if fixed_row is None:
        row = pl.program_id(0) // tpb
    else:
        row = fixed_row
    h = _rms_mod(x_ref[...], nw_ref[...], shift_ref[pl.ds(row, 1), :], scale_ref[pl.ds(row, 1), :])
    h_s[...] = h.astype(BF16)
    f_mid, f_amp = _forget_bounds(lbl_ref)
    ncol = HG_WIDTH
    nblk = w_ref.shape[1] // ncol
    for blk in sorted(range(nblk), key=lambda k: (k < PROJ_FF and k != PROJ_HQ, k)):
        cols = slice(blk * ncol, (blk + 1) * ncol)
        src = PROJ_SRC[blk]
        y = jnp.dot(h_s[...], w_ref[:, src * ncol:(src + 1) * ncol].astype(BF16), preferred_element_type=F32)
        if blk == 0:
            y = y * np.float32(LOG2E * NA_HEAD_DIM ** -0.5)
        if blk == PROJ_HQ:
            half = 0.5 * y
            y = half + half * jnp.tanh(half)
        if blk < PROJ_FF:
            o_ref[:, cols] = y.astype(o_ref.dtype)
        else:
            d = blk - PROJ_FF
            fg = f_mid[d:d + 1] + f_amp[d:d + 1] * jnp.tanh(0.5 * y)
            lf_ref[:, d * ncol:(d + 1) * ncol] = jnp.log(fg) * np.float32(1.0 / np.log(2.0))


def _proj(x2d, nw, mod, shift_blk, scale_blk, w, lb_logits, *, tm, tpb, fixed_row, mod_jobs=None):
    n_rows, d = x2d.shape
    n = w.shape[1]
    n_bf = PROJ_FF * HG_WIDTH
    steps = n_rows // tm
    job_in, job_out, job_shapes, job_args, n_cond = [], [], [], [], 0
    if mod_jobs is not None:
        ct, ada_w, ada_b3, n_cond, ranges = mod_jobs
        job_in.append(pl.BlockSpec((d, SUBLANES), lambda i: (0, 0)))
        job_args.append(ct)
        for layer, col0, ncols in ranges:
            bc = ncols // steps
            assert bc * steps == ncols and bc % LANES == 0 and col0 % bc == 0
            first = col0 // bc
            job_in.append(pl.BlockSpec((None, d, bc), lambda i, layer=layer, first=first: (layer, 0, first + i)))
            job_in.append(pl.BlockSpec((None, 1, bc), lambda i, layer=layer, first=first: (layer, 0, first + i)))
            job_out.append(pl.BlockSpec((SUBLANES, bc), lambda i: (0, i)))
            job_shapes.append(jax.ShapeDtypeStruct((SUBLANES, ncols), F32))
            job_args += [ada_w, ada_b3]
    kern = functools.partial(_proj_kernel, tpb=tpb, fixed_row=fixed_row, n_jobs=len(job_out), n_cond=n_cond)
    return pl.pallas_call(
        kern,
        grid=(steps,),
        in_specs=[
            pl.BlockSpec((tm, d), lambda i: (i, 0)),
            pl.BlockSpec((1, d), lambda i: (0, 0)),
            pl.BlockSpec((SUBLANES, d), lambda i: (0, shift_blk)),
            pl.BlockSpec((SUBLANES, d), lambda i: (0, scale_blk)),
            pl.BlockSpec((d, n), lambda i: (0, 0), pipeline_mode=pl.Buffered(1)),
            pl.BlockSpec(lb_logits.shape, lambda i: (0, 0, 0)),
        ] + job_in,
        out_specs=[pl.BlockSpec((tm, n_bf), lambda i: (i, 0)),
                   pl.BlockSpec((tm, n - n_bf), lambda i: (i, 0))] + job_out,
        out_shape=[jax.ShapeDtypeStruct((n_rows, n_bf), BF16),
                   jax.ShapeDtypeStruct((n_rows, n - n_bf), F32)] + job_shapes,
        scratch_shapes=[pltpu.VMEM((tm, d), BF16)],
        compiler_params=_cparams(("arbitrary",)),
        name="proj",
    )(x2d, nw, mod, mod, w, lb_logits, *job_args)


def _na_bias_rows(rpb):
    h, nr, nc = rpb.shape
    return jnp.pad(rpb.astype(F32) * LOG2E, ((0, 0), (0, 2 * NA_KH - nr), (0, LANES - nc)))


def _na_build_bands(rows_ref, band_ref):
    qc = lax.broadcasted_iota(jnp.int32, (GRID_W, LANES), 0)
    lane = lax.broadcasted_iota(jnp.int32, (GRID_W, LANES), 1)
    kc = jnp.where(lane < GRID_W, lane, lane - GRID_W)
    start = jnp.clip(qc - NA_KW // 2, 0, GRID_W - NA_KW)
    valid = (kc >= start) & (kc < start + NA_KW)
    masked = jnp.full((GRID_W, LANES), NEG, F32)
    for hh in range(band_ref.shape[0]):
        for a in range(2 * NA_KH - 1):
            x = jnp.broadcast_to(rows_ref[hh, a:a + 1, :], (GRID_W, LANES))
            lo = pltpu.roll(x, LANES - (NA_KW - 1), 1, stride=1, stride_axis=0)
            hi = pltpu.roll(x, LANES - (NA_KW - 1) - GRID_W, 1, stride=1, stride_axis=0)
            band_ref[hh, a] = jnp.where(valid, jnp.where(lane < GRID_W, lo, hi), masked)
        band_ref[hh, 2 * NA_KH - 1] = masked


def _na_block_index(rows):
    masked = 2 * NA_KH - 1

    def block_idx(rb):
        kr0 = int(np.clip(NA_QROWS * rb - NA_KH // 2, 0, rows - NA_KROWS))
        idx = np.empty((NA_QROWS, NA_KROWS), np.int32)
        for qi in range(NA_QROWS):
            qr = NA_QROWS * rb + qi
            r0 = int(np.clip(qr - NA_KH // 2, 0, rows - NA_KH))
            for kj in range(NA_KROWS):
                kr = kr0 + kj
                idx[qi, kj] = kr - qr + (NA_KH - 1) if r0 <= kr < r0 + NA_KH else masked
        return idx

    nblk = rows // NA_QROWS
    interior = block_idx(1)
    assert all(np.array_equal(block_idx(rb), interior) for rb in range(1, nblk - 1))
    return np.stack([block_idx(0), interior, block_idx(nblk - 1)])


def _na_kernel(q_ref, k_ref, v_ref, kc_ref, vc_ref, rows_ref, *rest, n_cast):
    o_ref = _cast_riders(rest, n_cast)
    bias_ref, band_ref = rest[2 * n_cast + 1:]
    s_len = q_ref.shape[0]
    qb = NA_QROWS * GRID_W
    kb = NA_KROWS * GRID_W
    nblk = s_len // qb
    rows = s_len // GRID_W
    _na_build_bands(rows_ref, band_ref)

    idx = _na_block_index(rows)
    left = lax.broadcasted_iota(jnp.int32, (GRID_W, LANES), 1) < GRID_W
    for hh in range(2):
        for t in range(3):
            for qi in range(NA_QROWS):
                for j in range(NA_KROWS // 2):
                    a, b = int(idx[t, qi, 2 * j]), int(idx[t, qi, 2 * j + 1])
                    tile = band_ref[hh, a] if a == b else jnp.where(left, band_ref[hh, a], band_ref[hh, b])
                    bias_ref[hh, t, qi * GRID_W:(qi + 1) * GRID_W, j * LANES:(j + 1) * LANES] = tile

    lane = lax.broadcasted_iota(jnp.int32, (qb, LANES), 1)
    in_head = [(lane >= NA_HEAD_DIM * hh) & (lane < NA_HEAD_DIM * (hh + 1)) for hh in range(2)]
    dn_t = (((1,), (1,)), ((), ()))

    def head_values(v, hh):
        return jnp.concatenate([jnp.where(in_head[hh], v[r:r + qb], jnp.ones_like(v[r:r + qb]))
                                for r in range(0, v.shape[0], qb)], axis=0)

    kc = kc_ref[...]
    vc = [head_values(vc_ref[...], hh) for hh in range(2)]

    def body(it, carry):
        blocks = []
        for u in range(NA_UNROLL):
            rb = it * NA_UNROLL + u
            kr0 = jnp.clip(NA_QROWS * rb - NA_KH // 2, 0, rows - NA_KROWS)
            ks = pl.multiple_of(kr0 * GRID_W, qb)
            qs = pl.multiple_of(rb * qb, qb)
            btype = jnp.where(rb == 0, 0, jnp.where(rb == nblk - 1, 2, 1))
            blocks.append((qs, k_ref[pl.ds(ks, kb), :], v_ref[pl.ds(ks, kb), :], q_ref[pl.ds(qs, qb), :], btype))
        logits = []
        for qs, kw, vw, q, btype in blocks:
            q2 = jnp.concatenate([jnp.where(in_head[hh], q, jnp.zeros_like(q)) for hh in range(2)], axis=0)
            s_loc = lax.dot_general(q2, kw, dn_t, preferred_element_type=F32)
            s_ctx = lax.dot_general(q2, kc, dn_t, preferred_element_type=F32)
            for hh in range(2):
                logits.append((s_loc[hh * qb:(hh + 1) * qb] + bias_ref[hh, btype], s_ctx[hh * qb:(hh + 1) * qb]))
        probs = []
        for s_loc, s_ctx in logits:
            m = jnp.maximum(jnp.max(s_loc, axis=-1, keepdims=True), jnp.max(s_ctx, axis=-1, keepdims=True))
            probs.append((jnp.exp2(s_loc - m).astype(BF16), jnp.exp2(s_ctx - m).astype(BF16)))
        outs = []
        for i, (p_loc, p_ctx) in enumerate(probs):
            hh = i % 2
            o = (jnp.dot(p_loc, head_values(blocks[i // 2][2], hh), preferred_element_type=F32)
                 + jnp.dot(p_ctx, vc[hh], preferred_element_type=F32))
            outs.append(o / pltpu.roll(o, NA_HEAD_DIM, 1))
        for u, blk in enumerate(blocks):
            o_ref[pl.ds(blk[0], qb), :] = jnp.where(in_head[0], outs[2 * u], outs[2 * u + 1]).astype(o_ref.dtype)
        return carry

    lax.fori_loop(0, nblk // NA_UNROLL, body, 0)


def _cast_riders(rest, n_cast):
    for w_ref, w16_ref in zip(rest[:n_cast], rest[n_cast + 1:2 * n_cast + 1]):
        w16_ref[...] = w_ref[...].astype(BF16)
    return rest[n_cast]


def _rider_specs(casts, grid):
    steps = int(np.prod(grid))

    def step(*idx):
        flat = idx[0]
        for extent, i in zip(grid[1:], idx[1:]):
            flat = flat * extent + i
        return flat

    cast_in, cast_out, cast_shapes = [], [], []
    for w, layer in casts:
        _, rows, cols = w.shape
        blk_rows = rows // steps
        assert blk_rows * steps == rows and blk_rows % HALO == 0
        cast_in.append(pl.BlockSpec((None, blk_rows, cols), lambda *idx, layer=layer: (layer, step(*idx), 0)))
        cast_out.append(pl.BlockSpec((blk_rows, cols), lambda *idx: (step(*idx), 0)))
        cast_shapes.append(jax.ShapeDtypeStruct((rows, cols), BF16))
    return cast_in, cast_out, cast_shapes


def _na(p_lat, p_ctx, band, casts, *, batch, s_len, t_len):
    hp = NA_HEADS // 2
    kblk = NA_WIDTH // LANES
    qb = NA_QROWS * GRID_W
    kbk = NA_KROWS * GRID_W
    cast_in, cast_out, cast_shapes = _rider_specs(casts, (batch, hp))
    return pl.pallas_call(
        functools.partial(_na_kernel, n_cast=len(casts)),
        grid=(batch, hp),
        in_specs=[
            pl.BlockSpec((s_len, LANES), lambda b, h: (b, h)),
            pl.BlockSpec((s_len, LANES), lambda b, h: (b, kblk + h)),
            pl.BlockSpec((s_len, LANES), lambda b, h: (b, 2 * kblk + h)),
            pl.BlockSpec((t_len, LANES), lambda b, h: (b, kblk + h)),
            pl.BlockSpec((t_len, LANES), lambda b, h: (b, 2 * kblk + h)),
            pl.BlockSpec((2, 2 * NA_KH, LANES), lambda b, h: (h, 0, 0)),
        ] + cast_in,
        out_specs=[pl.BlockSpec((s_len, LANES), lambda b, h: (b, h))] + cast_out,
        out_shape=[jax.ShapeDtypeStruct((batch * s_len, NA_WIDTH), BF16)] + cast_shapes,
        scratch_shapes=[pltpu.VMEM((2, 3, qb, kbk), F32), pltpu.VMEM((2, 2 * NA_KH, GRID_W, LANES), F32)],
        compiler_params=_cparams(("arbitrary", "arbitrary")),
        name="na",
    )(p_lat, p_lat, p_lat, p_ctx, p_ctx, band, *[w for w, _ in casts])


def _hg_kernel(q_ref, ff_ref, fb_ref, i_ref, g_ref, cff_ref, cfb_ref, ci_ref, nw_ref, *rest, n_cast):
    o_ref = _cast_riders(rest, n_cast)
    acc_ref = rest[2 * n_cast + 1]
    c = HG_CHUNK
    s_len = q_ref.shape[0]
    t_len = ci_ref.shape[0]
    n = s_len // c
    dn_t = (((1,), (1,)), ((), ()))

    def gates(lf):
        return 1.0 - jnp.exp2(lf), lf

    def masked_sum(mask, lf):
        hi = lf.astype(BF16)
        lo = (lf - hi.astype(F32)).astype(BF16)
        r = jnp.dot(mask, jnp.concatenate([hi, lo], axis=1), preferred_element_type=F32)
        return r[:, :LANES] + r[:, LANES:]

    r_t = lax.broadcasted_iota(jnp.int32, (t_len, t_len), 0)
    c_t = lax.broadcasted_iota(jnp.int32, (t_len, t_len), 1)
    vct = ci_ref[...].astype(F32).T.astype(BF16)
    states = []
    for fref, after in ((cff_ref, c_t > r_t), (cfb_ref, c_t < r_t)):
        kk, lf = gates(fref[...])
        kd = (kk * jnp.exp2(masked_sum(after.astype(BF16), lf))).astype(BF16)
        states.append(jnp.dot(vct, kd, preferred_element_type=F32))

    r_c = lax.broadcasted_iota(jnp.int32, (c, c), 0)
    c_c = lax.broadcasted_iota(jnp.int32, (c, c), 1)
    tri = (c_c <= r_c, c_c >= r_c)
    tri_b = tuple(t.astype(BF16) for t in tri)

    def cum(d, lf):
        return masked_sum(tri_b[d], lf)

    acc_ref[...] = jnp.zeros_like(acc_ref)

    def body(jj, carry):
        chains = []
        for u in range(HG_UNROLL):
            j = jj * HG_UNROLL + u
            chains.append((0, pl.multiple_of(j * c, c)))
            chains.append((1, pl.multiple_of((n - 1 - j) * c, c)))
        pre = []
        for d, start in chains:
            rows = pl.ds(start, c)
            kk, lf = gates((ff_ref, fb_ref)[d][rows, :])
            pre.append((kk, lf, q_ref[rows, :].astype(F32), i_ref[rows, :]))
        bcs = [cum(d, p[1]) for (d, _), p in zip(chains, pre)]
        ops = []
        for (d, _), (kk, lf, qh, v), bc in zip(chains, pre, bcs):
            btot = bc[c - 1:c] if d == 0 else bc[0:1]
            mref = bc[c // 2:c // 2 + 1]
            qm = qh * jnp.exp2(bc - mref)
            km = kk * jnp.exp2(mref - bc)
            qe = (qm * jnp.exp2(mref)).astype(BF16)
            kd = (km * jnp.exp2(btot - mref)).astype(BF16)
            ops.append((qe, qm.astype(BF16), km.astype(BF16), kd, jnp.exp2(btot), v))
        atts = [jnp.where(tri[d], lax.dot_general(qm, km, dn_t, preferred_element_type=F32), 0.0).astype(BF16)
                for (d, _), (qe, qm, km, kd, eb, v) in zip(chains, ops)]
        upds = [lax.dot_general(v, kd, (((0,), (0,)), ((), ())), preferred_element_type=F32)
                for (qe, qm, km, kd, eb, v) in ops]
        intra = [jnp.dot(att, o[5], preferred_element_type=F32) for att, o in zip(atts, ops)]
        st = list(carry)
        for idx, (d, start) in enumerate(chains):
            qe, eb = ops[idx][0], ops[idx][4]
            o = intra[idx] + lax.dot_general(qe, st[d].astype(BF16), dn_t, preferred_element_type=F32)
            st[d] = eb * st[d] + upds[idx]
            acc_ref[pl.ds(start, c), :] += o
        return st[0], st[1]

    lax.fori_loop(0, n // HG_UNROLL, body, (states[0], states[1]))

    rb = HG_READOUT_ROWS
    nw = nw_ref[...]

    def readout(j, carry):
        rows = pl.ds(pl.multiple_of(j * rb, rb), rb)
        o = acc_ref[rows, :]
        o = o * lax.rsqrt(jnp.mean(o * o, axis=-1, keepdims=True) + EPS) * nw
        half = 0.5 * g_ref[rows, :].astype(F32)
        o_ref[rows, :] = (o * (half + half * jnp.tanh(half))).astype(o_ref.dtype)
        return carry

    lax.fori_loop(0, s_len // rb, readout, 0)


def _hgrn(p_lat, lf_lat, p_ctx, lf_ctx, norm_w, casts, *, batch, s_len, t_len):
    nh = HG_HEADS
    per = HG_WIDTH // LANES
    cast_in, cast_out, cast_shapes = _rider_specs(casts, (batch, nh))

    def blk(rows, group):
        return pl.BlockSpec((rows, LANES), lambda b, h: (b, group * per + h))

    return pl.pallas_call(
        functools.partial(_hg_kernel, n_cast=len(casts)),
        grid=(batch, nh),
        in_specs=[blk(s_len, PROJ_HQ), blk(s_len, 0), blk(s_len, 1), blk(s_len, PROJ_HQ + 1),
                  blk(s_len, PROJ_HQ + 2), blk(t_len, 0), blk(t_len, 1), blk(t_len, PROJ_HQ + 1),
                  pl.BlockSpec((1, LANES), lambda b, h: (0, h))] + cast_in,
        out_specs=[pl.BlockSpec((s_len, LANES), lambda b, h: (b, h))] + cast_out,
        out_shape=[jax.ShapeDtypeStruct((batch * s_len, HG_WIDTH), BF16)] + cast_shapes,
        scratch_shapes=[pltpu.VMEM((s_len, LANES), F32)],
        compiler_params=_cparams(("arbitrary", "arbitrary")),
        name="hgrn",
    )(p_lat, lf_lat, lf_lat, p_lat, p_lat, lf_ctx, lf_ctx, p_ctx, norm_w, *[w for w, _ in casts])


def _halo_specs(tm, width, n_rows):
    per = tm // HALO
    last = n_rows // HALO - 1
    return [
        pl.BlockSpec((tm, width), lambda i: (i, 0)),
        pl.BlockSpec((HALO, width), lambda i: (jnp.maximum(i * per - 1, 0), 0)),
        pl.BlockSpec((HALO, width), lambda i: (jnp.minimum((i + 1) * per, last), 0)),
    ]


def _assemble(dst, main, prev, nxt, tm):
    dst[0:HALO, :] = prev[...]
    dst[HALO:HALO + tm, :] = main[...]
    dst[HALO + tm:, :] = nxt[...]


def _store_rows(he_s, h_prev, h_main, h_next, tm, tpb):
    t = pl.program_id(0) % tpb
    keep_prev = jnp.where(t != 0, 1.0, 0.0)
    keep_next = jnp.where(t != tpb - 1, 1.0, 0.0)
    he_s[0:HALO, :] = (h_prev * keep_prev).astype(BF16)
    he_s[HALO:HALO + tm, :] = h_main.astype(BF16)
    he_s[HALO + tm:, :] = (h_next * keep_next).astype(BF16)


def _store_mod_rows(he_s, xm, xp, xn, nw, shift, scale, tm, tpb):
    _store_rows(he_s, _rms_mod(xp[...], nw, shift, scale), _rms_mod(xm[...], nw, shift, scale),
                _rms_mod(xn[...], nw, shift, scale), tm, tpb)


def _conv3(a_ext, cw, cb, tm):
    rows = tm + 2 * HALO
    prev = pltpu.roll(a_ext, 1, 0)[HALO:HALO + tm]
    nxt = pltpu.roll(a_ext, rows - 1, 0)[HALO:HALO + tm]
    cur = a_ext[HALO:HALO + tm]
    return prev * cw[0:1] + cur * cw[1:2] + nxt * cw[2:3] + cb


def _gelu_exact(x):
    return 0.5 * x * (1.0 + lax.erf(x * np.float32(np.sqrt(0.5))))


def _convglu(he_s, act_s, wup_ref, cw_ref, cb_ref, wdown_ref, tm):
    d_ff = wdown_ref.shape[0]
    nchunk = d_ff // FF_CHUNK

    def cols(j, base=0):
        return slice(base + j * FF_CHUNK, base + (j + 1) * FF_CHUNK)

    def up(j):
        a = jnp.dot(he_s[...], wup_ref[:, cols(j)], preferred_element_type=F32)
        val = jnp.dot(he_s[HALO:HALO + tm, :], wup_ref[:, cols(j, d_ff)], preferred_element_type=F32)
        return a, val

    nxt = up(0)
    for j in range(nchunk):
        a, val = nxt
        if j + 1 < nchunk:
            nxt = up(j + 1)
        cv = _conv3(a, cw_ref[:, cols(j)], cb_ref[:, cols(j)], tm)
        act_s[:, cols(j)] = (_gelu_exact(cv) * val).astype(BF16)
    return jnp.dot(act_s[...], wdown_ref[...], preferred_element_type=F32)


def _ffn0_kernel(xm, xp, xn, am, ap, an, gm, gp, gn, gate_m, shift_f, scale_f, gate_f, nw,
                 wo_ref, wup_ref, cw_ref, cb_ref, wdown_ref, o_ref,
                 xe_s, ae_s, ge_s, he_s, act_s, *, tm, tpb):
    b = pl.program_id(0) // tpb
    _assemble(xe_s, xm, xp, xn, tm)
    _assemble(ae_s, am, ap, an, tm)
    _assemble(ge_s, gm, gp, gn, tm)
    y = (jnp.dot(ae_s[...], wo_ref[0], preferred_element_type=F32)
         + jnp.dot(ge_s[...], wo_ref[1], preferred_element_type=F32))
    x1 = xe_s[...] + gate_m[pl.ds(b, 1), :] * y
    xe_s[...] = x1
    h = _rms_mod(x1, nw[...], shift_f[pl.ds(b, 1), :], scale_f[pl.ds(b, 1), :])
    _store_rows(he_s, h[0:HALO], h[HALO:HALO + tm], h[HALO + tm:], tm, tpb)
    y2 = _convglu(he_s, act_s, wup_ref, cw_ref, cb_ref, wdown_ref, tm)
    o_ref[...] = xe_s[HALO:HALO + tm, :] + gate_f[pl.ds(b, 1), :] * y2


def _ffn1_kernel(xm, xp, xn, shift_f, scale_f, gate_f, nw, fnw,
                 wup_ref, cw_ref, cb_ref, wdown_ref, o_ref,
                 he_s, act_s, *, tm, tpb):
    b = pl.program_id(0) // tpb
    _store_mod_rows(he_s, xm, xp, xn, nw[...], shift_f[pl.ds(b, 1), :], scale_f[pl.ds(b, 1), :], tm, tpb)
    y2 = _convglu(he_s, act_s, wup_ref, cw_ref, cb_ref, wdown_ref, tm)
    x2 = xm[...] + gate_f[pl.ds(b, 1), :] * y2
    ms = jnp.mean(x2 * x2, axis=-1, keepdims=True)
    o_ref[...] = x2 * lax.rsqrt(ms + EPS) * fnw[...]


def _mod_spec(k):
    return pl.BlockSpec((SUBLANES, D_MODEL), lambda i: (0, k))


def _const_spec(shape):
    nd = len(shape)
    return pl.BlockSpec(shape, lambda i: (0,) * nd, pipeline_mode=pl.Buffered(1))


def _ffn_weight_specs(d_ff):
    return [
        _const_spec((D_MODEL, 2 * d_ff)),
        _const_spec((3, d_ff)),
        _const_spec((1, d_ff)),
        _const_spec((d_ff, D_MODEL)),
    ]


def _ffn0(x2d, a, g, mod, nw, wo, wup, cw, cb, wdown, *, tm, tpb):
    n_rows = x2d.shape[0]
    d_ff = wdown.shape[0]
    ext = tm + 2 * HALO
    kern = functools.partial(_ffn0_kernel, tm=tm, tpb=tpb)
    return pl.pallas_call(
        kern,
        grid=(n_rows // tm,),
        in_specs=(_halo_specs(tm, D_MODEL, n_rows) + _halo_specs(tm, NA_WIDTH, n_rows)
                  + _halo_specs(tm, HG_WIDTH, n_rows)
                  + [_mod_spec(0), _mod_spec(1), _mod_spec(2), _mod_spec(3),
                     _const_spec((1, D_MODEL)), _const_spec((2, NA_WIDTH, D_MODEL))]
                  + _ffn_weight_specs(d_ff)),
        out_specs=pl.BlockSpec((tm, D_MODEL), lambda i: (i, 0)),
        out_shape=jax.ShapeDtypeStruct((n_rows, D_MODEL), F32),
        scratch_shapes=[pltpu.VMEM((ext, D_MODEL), F32), pltpu.VMEM((ext, NA_WIDTH), BF16),
                        pltpu.VMEM((ext, HG_WIDTH), BF16), pltpu.VMEM((ext, D_MODEL), BF16),
                        pltpu.VMEM((tm, d_ff), BF16)],
        compiler_params=_cparams(("arbitrary",)),
        name="ffn0",
    )(x2d, x2d, x2d, a, a, a, g, g, g, mod, mod, mod, mod, nw, wo, wup, cw, cb, wdown)


def _ffn1(x2d, mod, nw, fnw, wup, cw, cb, wdown, *, tm, tpb):
    n_rows = x2d.shape[0]
    d_ff = wdown.shape[0]
    ext = tm + 2 * HALO
    kern = functools.partial(_ffn1_kernel, tm=tm, tpb=tpb)
    weight_specs = _ffn_weight_specs(d_ff)
    return pl.pallas_call(
        kern,
        grid=(n_rows // tm,),
        in_specs=(_halo_specs(tm, D_MODEL, n_rows)
                  + [_mod_spec(3), _mod_spec(4), _mod_spec(5),
                     _const_spec((1, D_MODEL)), _const_spec((1, D_MODEL))]
                  + weight_specs),
        out_specs=pl.BlockSpec((tm, D_MODEL), lambda i: (i, 0)),
        out_shape=jax.ShapeDtypeStruct((n_rows, D_MODEL), F32),
        scratch_shapes=[pltpu.VMEM((ext, D_MODEL), BF16), pltpu.VMEM((tm, d_ff), BF16)],
        compiler_params=_cparams(("arbitrary",)),
        name="ffn1",
    )(x2d, x2d, x2d, mod, mod, mod, nw, fnw, wup, cw, cb, wdown)


def _mix_kernel(xm, xp, xn, shift_m, scale_m, gate_m, nw, win_ref, cw_ref, cb_ref, wout_ref, *rest,
                tm, tpb, n_cast):
    o_ref = _cast_riders(rest, n_cast)
    he_s, z_s = rest[2 * n_cast + 1:]
    b = pl.program_id(0) // tpb
    width = wout_ref.shape[0]
    nchunk = width // FF_CHUNK
    _store_mod_rows(he_s, xm, xp, xn, nw[...], shift_m[pl.ds(b, 1), :], scale_m[pl.ds(b, 1), :], tm, tpb)

    def cols(j, base=0):
        return slice(base + j * FF_CHUNK, base + (j + 1) * FF_CHUNK)

    def w_in(j, base):
        return win_ref[:, cols(j, base)].astype(BF16)

    def up(j):
        gb = jnp.dot(he_s[HALO:HALO + tm, :], w_in(j, 0), preferred_element_type=F32)
        gc = jnp.dot(he_s[...], w_in(j, width), preferred_element_type=F32)
        u = jnp.dot(he_s[...], w_in(j, 2 * width), preferred_element_type=F32)
        return gb, gc, u

    nxt = up(0)
    for j in range(nchunk):
        gb, gc, u = nxt
        if j + 1 < nchunk:
            nxt = up(j + 1)
        z_s[:, cols(j)] = (gb * _conv3(gc * u, cw_ref[:, cols(j)], cb_ref[:, cols(j)], tm)).astype(BF16)
    y = jnp.dot(z_s[...], wout_ref[...].astype(BF16), preferred_element_type=F32)
    o_ref[...] = xm[...] + gate_m[pl.ds(b, 1), :] * y


def _mix(x2d, mod, nw, win, cw, cb, wout, casts, *, tm, tpb):
    n_rows = x2d.shape[0]
    width = wout.shape[0]
    ext = tm + 2 * HALO
    cast_in, cast_out, cast_shapes = _rider_specs(casts, (n_rows // tm,))
    kern = functools.partial(_mix_kernel, tm=tm, tpb=tpb, n_cast=len(casts))
    return pl.pallas_call(
        kern,
        grid=(n_rows // tm,),
        in_specs=(_halo_specs(tm, D_MODEL, n_rows)
                  + [_mod_spec(0), _mod_spec(1), _mod_spec(2), _const_spec((1, D_MODEL)),
                     _const_spec((D_MODEL, 3 * width)), _const_spec((3, width)),
                     _const_spec((1, width)), _const_spec((width, D_MODEL))] + cast_in),
        out_specs=[pl.BlockSpec((tm, D_MODEL), lambda i: (i, 0))] + cast_out,
        out_shape=[jax.ShapeDtypeStruct((n_rows, D_MODEL), F32)] + cast_shapes,
        scratch_shapes=[pltpu.VMEM((ext, D_MODEL), BF16), pltpu.VMEM((tm, width), BF16)],
        compiler_params=_cparams(("arbitrary",)),
        name="mix",
    )(x2d, x2d, x2d, mod, mod, mod, nw, win, cw, cb, wout, *[w for w, _ in casts])


def kernel(x, c, ctx, c_ctx, ada_w, ada_b, norm_mix_w, norm_ffn_w, ev_w_in, ev_w_out, na_rpb, hg_lb_logits,
           hg_norm_w, od_w_in, od_conv_w, od_conv_b, od_w_out, ffn_w_up, ffn_conv_w, ffn_conv_b, ffn_w_down,
           final_norm_w):
    batch, s_len, d = x.shape
    t_len = ctx.shape[1]
    n_rows = batch * s_len
    tpb = s_len // TM
    x2d = x.reshape(n_rows, d)
    ctx2d = ctx.reshape(batch * t_len, d)

    c_stack = jnp.concatenate([c, c_ctx[None], jnp.zeros((SUBLANES - batch - 1, d), F32)], axis=0)
    ct = c_stack.T
    ada_b3 = ada_b.reshape(ada_b.shape[0], 1, -1)
    mod_in = _ada(ct, ada_w, ada_b3, batch + 1, 0, 2 * d)
    mod_jobs = (ct, ada_w, ada_b3, batch + 1, [(0, 2 * d, 4 * d), (1, 0, 6 * d)])

    w_in0 = ev_w_in[0]
    nw_mix = norm_mix_w.reshape(-1, 1, d)
    nw_ffn = norm_ffn_w.reshape(-1, 1, d)
    p_lat, lf_lat, mod0, mod1 = _proj(x2d, nw_mix[0], mod_in, 0, 1, w_in0, hg_lb_logits, tm=TM, tpb=tpb,
                                      fixed_row=None, mod_jobs=mod_jobs)
    p_ctx, lf_ctx = _proj(ctx2d, nw_mix[0], mod_in, 0, 1, w_in0, hg_lb_logits, tm=t_len, tpb=1,
                          fixed_row=batch)
    a_lat, wup0 = _na(p_lat, p_ctx, _na_bias_rows(na_rpb[0]), [(ffn_w_up, 0)],
                      batch=batch, s_len=s_len, t_len=t_len)
    g_lat, wo0, wdown0, wdown1 = _hgrn(p_lat, lf_lat, p_ctx, lf_ctx, hg_norm_w[0].reshape(1, HG_WIDTH),
                                       [(ev_w_out, 0), (ffn_w_down, 0), (ffn_w_down, 1)],
                                       batch=batch, s_len=s_len, t_len=t_len)
    x1 = _ffn0(x2d, a_lat, g_lat, mod0, nw_ffn[0], wo0.reshape(2, NA_WIDTH, d),
               wup0, ffn_conv_w[0], ffn_conv_b[0][None], wdown0, tm=TM, tpb=tpb)

    x2, wup1 = _mix(x1, mod1, nw_mix[1], od_w_in[0], od_conv_w[0], od_conv_b[0][None], od_w_out[0],
                    [(ffn_w_up, 1)], tm=TM, tpb=tpb)
    out = _ffn1(x2, mod1, nw_ffn[1], final_norm_w.reshape(1, d),
                wup1, ffn_conv_w[1], ffn_conv_b[1][None], wdown1, tm=TM, tpb=tpb)
    return out.reshape(batch, s_len, d)
```

```python
import functools

import numpy as np
import jax
import jax.numpy as jnp
from jax import lax
from jax.experimental import pallas as pl
from jax.experimental.pallas import tpu as pltpu

F32 = jnp.float32
BF16 = jnp.bfloat16

D_MODEL = 1024
GRID_W = 64
EPS = 1e-6
NA_HEADS = 8
NA_HEAD_DIM = 64
NA_WIDTH = NA_HEADS * NA_HEAD_DIM
NA_KH = 8
NA_KW = 16
HG_HEADS = 4

V7X_VMEM_BYTES = 64 * 1024 * 1024
VMEM_LIMIT = V7X_VMEM_BYTES - 8 * 1024 * 1024
LANES = 128
SUBLANES = 8
HALO = 16
HG_WIDTH = HG_HEADS * LANES
NEG = -1e30
LOG2E = float(np.log2(np.e))

NA_QROWS = 4
NA_KROWS = NA_QROWS + NA_KH
NA_UNROLL = 8
HG_CHUNK = 64
HG_UNROLL = 32
HG_READOUT_ROWS = 512
FF_CHUNK = 256
ADA_COLS = 1024
PROJ_SRC = (0, 1, 2, 3, 6, 7, 4, 5)
PROJ_HQ = 3
PROJ_FF = 6
TM = 1024


def _cparams(sem):
    return pltpu.CompilerParams(dimension_semantics=sem, vmem_limit_bytes=VMEM_LIMIT)


def _rms_mod(x, nw, shift, scale):
    ms = jnp.mean(x * x, axis=-1, keepdims=True)
    return (x * lax.rsqrt(ms + EPS)) * (nw * (1.0 + scale)) + shift


def _modulation(ct_ref, w_ref, b_ref, o_ref, n_cond):
    s = jax.nn.silu(ct_ref[...])
    o_ref[...] = jnp.zeros(o_ref.shape, F32)
    for r in range(n_cond):
        o_ref[r:r + 1, :] = jnp.sum(s[:, r:r + 1] * w_ref[...], axis=0, keepdims=True) + b_ref[...]


def _ada_kernel(ct_ref, w_ref, b_ref, o_ref, *, n_cond):
    _modulation(ct_ref, w_ref, b_ref, o_ref, n_cond)


def _ada(ct, ada_w, ada_b3, n_cond, layer, ncols):
    d = ada_w.shape[1]
    tn = ADA_COLS
    return pl.pallas_call(
        functools.partial(_ada_kernel, n_cond=n_cond),
        grid=(ncols // tn,),
        in_specs=[
            pl.BlockSpec((d, SUBLANES), lambda j: (0, 0)),
            pl.BlockSpec((None, d, tn), lambda j: (layer, 0, j)),
            pl.BlockSpec((None, 1, tn), lambda j: (layer, 0, j)),
        ],
        out_specs=pl.BlockSpec((SUBLANES, tn), lambda j: (0, j)),
        out_shape=jax.ShapeDtypeStruct((SUBLANES, ncols), F32),
        compiler_params=_cparams(("arbitrary",)),
        name="ada",
    )(ct, ada_w, ada_b3)


def _forget_bounds(lbl_ref):
    ll = lbl_ref[...]
    mx = jnp.maximum(ll[0], ll[1])
    e0 = jnp.exp(ll[0] - mx)
    lb = e0 / (e0 + jnp.exp(ll[1] - mx))
    return 0.5 * (1.0 + lb), 0.5 * (1.0 - lb)


def _proj_kernel(x_ref, cx_ref, nw_ref, shift_ref, scale_ref, w_ref, lbl_ref, ct_ref, *rest,
                 tpb, ctx_row, n_jobs, n_cond):
    jobs_in = rest[:2 * n_jobs]
    o_ref, lf_ref, octx_ref, lfctx_ref = rest[2 * n_jobs:2 * n_jobs + 4]
    jobs_out = rest[2 * n_jobs + 4:3 * n_jobs + 4]
    h_s = rest[-1]
    for job in range(n_jobs):
        _modulation(ct_ref, jobs_in[2 * job], jobs_in[2 * job + 1], jobs_out[job], n_cond)
    tm = x_ref.shape[0]
    row = pl.program_id(0) // tpb
    nw = nw_ref[...]
    h_s[0:tm, :] = _rms_mod(x_ref[...], nw, shift_ref[pl.ds(row, 1), :], scale_ref[pl.ds(row, 1), :]).astype(BF16)
    h_s[tm:, :] = _rms_mod(cx_ref[...], nw, shift_ref[ctx_row:ctx_row + 1, :],
                           scale_ref[ctx_row:ctx_row + 1, :]).astype(BF16)
    f_mid, f_amp = _forget_bounds(lbl_ref)
    ncol = HG_WIDTH
    nblk = w_ref.shape[1] // ncol
    for blk in sorted(range(nblk), key=lambda k: (k < PROJ_FF and k != PROJ_HQ, k)):
        cols = slice(blk * ncol, (blk + 1) * ncol)
        src = PROJ_SRC[blk]
        y = jnp.dot(h_s[...], w_ref[:, src * ncol:(src + 1) * ncol].astype(BF16), preferred_element_type=F32)
        if blk == 0:
            y = y * np.float32(LOG2E * NA_HEAD_DIM ** -0.5)
        if blk == PROJ_HQ:
            half = 0.5 * y
            y = half + half * jnp.tanh(half)
        if blk < PROJ_FF:
            y = y.astype(o_ref.dtype)
            o_ref[:, cols] = y[:tm]
            octx_ref[:, cols] = y[tm:]
        else:
            d = blk - PROJ_FF
            fg = f_mid[d:d + 1] + f_amp[d:d + 1] * jnp.tanh(0.5 * y)
            lf = jnp.log(fg) * np.float32(1.0 / np.log(2.0))
            lf_ref[:, d * ncol:(d + 1) * ncol] = lf[:tm]
            lfctx_ref[:, d * ncol:(d + 1) * ncol] = lf[tm:]


def _proj(x2d, ctx2d, nw, mod, w, lb_logits, ct, ada_w, ada_b3, n_cond, ranges, *, tm, tpb, ctx_row):
    n_rows, d = x2d.shape
    n = w.shape[1]
    n_bf = PROJ_FF * HG_WIDTH
    steps = n_rows // tm
    cb = ctx2d.shape[0] // steps
    assert cb * steps == ctx2d.shape[0] and cb % HALO == 0
    job_in, job_out, job_shapes, job_args = [], [], [], []
    for layer, col0, ncols in ranges:
        bc = ncols // steps
        assert bc * steps == ncols and bc % LANES == 0 and col0 % bc == 0
        first = col0 // bc
        job_in.append(pl.BlockSpec((None, d, bc), lambda i, layer=layer, first=first: (layer, 0, first + i)))
        job_in.append(pl.BlockSpec((None, 1, bc), lambda i, layer=layer, first=first: (layer, 0, first + i)))
        job_out.append(pl.BlockSpec((SUBLANES, bc), lambda i: (0, i)))
        job_shapes.append(jax.ShapeDtypeStruct((SUBLANES, ncols), F32))
        job_args += [ada_w, ada_b3]
    kern = functools.partial(_proj_kernel, tpb=tpb, ctx_row=ctx_row, n_jobs=len(ranges), n_cond=n_cond)
    return pl.pallas_call(
        kern,
        grid=(steps,),
        in_specs=[
            pl.BlockSpec((tm, d), lambda i: (i, 0)),
            pl.BlockSpec((cb, d), lambda i: (i, 0)),
            pl.BlockSpec((1, d), lambda i: (0, 0)),
            pl.BlockSpec((SUBLANES, d), lambda i: (0, 0)),
            pl.BlockSpec((SUBLANES, d), lambda i: (0, 1)),
            pl.BlockSpec((d, n), lambda i: (0, 0), pipeline_mode=pl.Buffered(1)),
            pl.BlockSpec(lb_logits.shape, lambda i: (0, 0, 0)),
            pl.BlockSpec((d, SUBLANES), lambda i: (0, 0)),
        ] + job_in,
        out_specs=[pl.BlockSpec((tm, n_bf), lambda i: (i, 0)),
                   pl.BlockSpec((tm, n - n_bf), lambda i: (i, 0)),
                   pl.BlockSpec((cb, n_bf), lambda i: (i, 0)),
                   pl.BlockSpec((cb, n - n_bf), lambda i: (i, 0))] + job_out,
        out_shape=[jax.ShapeDtypeStruct((n_rows, n_bf), BF16),
                   jax.ShapeDtypeStruct((n_rows, n - n_bf), F32),
                   jax.ShapeDtypeStruct((ctx2d.shape[0], n_bf), BF16),
                   jax.ShapeDtypeStruct((ctx2d.shape[0], n - n_bf), F32)] + job_shapes,
        scratch_shapes=[pltpu.VMEM((tm + cb, d), BF16)],
        compiler_params=_cparams(("arbitrary",)),
        name="proj",
    )(x2d, ctx2d, nw, mod, mod, w, lb_logits, ct, *job_args)


def _na_bias_rows(rpb):
    h, nr, nc = rpb.shape
    return jnp.pad(rpb.astype(F32) * LOG2E, ((0, 0), (0, 2 * NA_KH - nr), (0, LANES - nc)))


def _na_build_bands(rows_ref, band_ref):
    qc = lax.broadcasted_iota(jnp.int32, (GRID_W, LANES), 0)
    lane = lax.broadcasted_iota(jnp.int32, (GRID_W, LANES), 1)
    kc = jnp.where(lane < GRID_W, lane, lane - GRID_W)
    start = jnp.clip(qc - NA_KW // 2, 0, GRID_W - NA_KW)
    valid = (kc >= start) & (kc < start + NA_KW)
    masked = jnp.full((GRID_W, LANES), NEG, F32)
    for hh in range(band_ref.shape[0]):
        for a in range(2 * NA_KH - 1):
            x = jnp.broadcast_to(rows_ref[hh, a:a + 1, :], (GRID_W, LANES))
            lo = pltpu.roll(x, LANES - (NA_KW - 1), 1, stride=1, stride_axis=0)
            hi = pltpu.roll(x, LANES - (NA_KW - 1) - GRID_W, 1, stride=1, stride_axis=0)
            band_ref[hh, a] = jnp.where(valid, jnp.where(lane < GRID_W, lo, hi), masked)
        band_ref[hh, 2 * NA_KH - 1] = masked


def _na_block_index(rows):
    masked = 2 * NA_KH - 1

    def block_idx(rb):
        kr0 = int(np.clip(NA_QROWS * rb - NA_KH // 2, 0, rows - NA_KROWS))
        idx = np.empty((NA_QROWS, NA_KROWS), np.int32)
        for qi in range(NA_QROWS):
            qr = NA_QROWS * rb + qi
            r0 = int(np.clip(qr - NA_KH // 2, 0, rows - NA_KH))
            for kj in range(NA_KROWS):
                kr = kr0 + kj
                idx[qi, kj] = kr - qr + (NA_KH - 1) if r0 <= kr < r0 + NA_KH else masked
        return idx

    nblk = rows // NA_QROWS
    interior = block_idx(1)
    assert all(np.array_equal(block_idx(rb), interior) for rb in range(1, nblk - 1))
    return np.stack([block_idx(0), interior, block_idx(nblk - 1)])


def _na_kernel(q_ref, k_ref, v_ref, kc_ref, vc_ref, rows_ref, *rest, n_cast):
    o_ref = _cast_riders(rest, n_cast)
    bias_ref, band_ref = rest[2 * n_cast + 1:]
    s_len = q_ref.shape[0]
    qb = NA_QROWS * GRID_W
    kb = NA_KROWS * GRID_W
    nblk = s_len // qb
    rows = s_len // GRID_W
    _na_build_bands(rows_ref, band_ref)

    idx = _na_block_index(rows)
    left = lax.broadcasted_iota(jnp.int32, (GRID_W, LANES), 1) < GRID_W
    for hh in range(2):
        for t in range(3):
            for qi in range(NA_QROWS):
                for j in range(NA_KROWS // 2):
                    a, b = int(idx[t, qi, 2 * j]), int(idx[t, qi, 2 * j + 1])
                    tile = band_ref[hh, a] if a == b else jnp.where(left, band_ref[hh, a], band_ref[hh, b])
                    bias_ref[hh, t, qi * GRID_W:(qi + 1) * GRID_W, j * LANES:(j + 1) * LANES] = tile

    lane = lax.broadcasted_iota(jnp.int32, (qb, LANES), 1)
    in_head = [(lane >= NA_HEAD_DIM * hh) & (lane < NA_HEAD_DIM * (hh + 1)) for hh in range(2)]
    dn_t = (((1,), (1,)), ((), ()))

    def head_values(v, hh):
        return jnp.concatenate([jnp.where(in_head[hh], v[r:r + qb], jnp.ones_like(v[r:r + qb]))
                                for r in range(0, v.shape[0], qb)], axis=0)

    kc = kc_ref[...]
    vc = [head_values(vc_ref[...], hh) for hh in range(2)]

    def body(it, carry):
        blocks = []
        for u in range(NA_UNROLL):
            rb = it * NA_UNROLL + u
            kr0 = jnp.clip(NA_QROWS * rb - NA_KH // 2, 0, rows - NA_KROWS)
            ks = pl.multiple_of(kr0 * GRID_W, qb)
            qs = pl.multiple_of(rb * qb, qb)
            btype = jnp.where(rb == 0, 0, jnp.where(rb == nblk - 1, 2, 1))
            blocks.append((qs, k_ref[pl.ds(ks, kb), :], v_ref[pl.ds(ks, kb), :], q_ref[pl.ds(qs, qb), :], btype))
        logits = []
        for qs, kw, vw, q, btype in blocks:
            q2 = jnp.concatenate([jnp.where(in_head[hh], q, jnp.zeros_like(q)) for hh in range(2)], axis=0)
            s_loc = lax.dot_general(q2, kw, dn_t, preferred_element_type=F32)
            s_ctx = lax.dot_general(q2, kc, dn_t, preferred_element_type=F32)
            for hh in range(2):
                logits.append((s_loc[hh * qb:(hh + 1) * qb] + bias_ref[hh, btype], s_ctx[hh * qb:(hh + 1) * qb]))
        probs = []
        for s_loc, s_ctx in logits:
            m = jnp.maximum(jnp.max(s_loc, axis=-1, keepdims=True), jnp.max(s_ctx, axis=-1, keepdims=True))
            probs.append((jnp.exp2(s_loc - m).astype(BF16), jnp.exp2(s_ctx - m).astype(BF16)))
        outs = []
        for i, (p_loc, p_ctx) in enumerate(probs):
            hh = i % 2
            o = (jnp.dot(p_loc, head_values(blocks[i // 2][2], hh), preferred_element_type=F32)
                 + jnp.dot(p_ctx, vc[hh], preferred_element_type=F32))
            outs.append(o / pltpu.roll(o, NA_HEAD_DIM, 1))
        for u, blk in enumerate(blocks):
            o_ref[pl.ds(blk[0], qb), :] = jnp.where(in_head[0], outs[2 * u], outs[2 * u + 1]).astype(o_ref.dtype)
        return carry

    lax.fori_loop(0, nblk // NA_UNROLL, body, 0)


def _cast_riders(rest, n_cast):
    for w_ref, w16_ref in zip(rest[:n_cast], rest[n_cast + 1:2 * n_cast + 1]):
        w16_ref[...] = w_ref[...].astype(BF16)
    return rest[n_cast]


def _rider_specs(casts, grid):
    steps = int(np.prod(grid))

    def step(*idx):
        flat = idx[0]
        for extent, i in zip(grid[1:], idx[1:]):
            flat = flat * extent + i
        return flat

    cast_in, cast_out, cast_shapes = [], [], []
    for w, layer in casts:
        _, rows, cols = w.shape
        blk_rows = rows // steps
        assert blk_rows * steps == rows and blk_rows % HALO == 0
        cast_in.append(pl.BlockSpec((None, blk_rows, cols), lambda *idx, layer=layer: (layer, step(*idx), 0)))
        cast_out.append(pl.BlockSpec((blk_rows, cols), lambda *idx: (step(*idx), 0)))
        cast_shapes.append(jax.ShapeDtypeStruct((rows, cols), BF16))
    return cast_in, cast_out, cast_shapes


def _na(p_lat, p_ctx, band, casts, *, batch, s_len, t_len):
    hp = NA_HEADS // 2
    kblk = NA_WIDTH // LANES
    qb = NA_QROWS * GRID_W
    kbk = NA_KROWS * GRID_W
    cast_in, cast_out, cast_shapes = _rider_specs(casts, (batch, hp))
    return pl.pallas_call(
        functools.partial(_na_kernel, n_cast=len(casts)),
        grid=(batch, hp),
        in_specs=[
            pl.BlockSpec((s_len, LANES), lambda b, h: (b, h)),
            pl.BlockSpec((s_len, LANES), lambda b, h: (b, kblk + h)),
            pl.BlockSpec((s_len, LANES), lambda b, h: (b, 2 * kblk + h)),
            pl.BlockSpec((t_len, LANES), lambda b, h: (b, kblk + h)),
            pl.BlockSpec((t_len, LANES), lambda b, h: (b, 2 * kblk + h)),
            pl.BlockSpec((2, 2 * NA_KH, LANES), lambda b, h: (h, 0, 0)),
        ] + cast_in,
        out_specs=[pl.BlockSpec((s_len, LANES), lambda b, h: (b, h))] + cast_out,
        out_shape=[jax.ShapeDtypeStruct((batch * s_len, NA_WIDTH), BF16)] + cast_shapes,
        scratch_shapes=[pltpu.VMEM((2, 3, qb, kbk), F32), pltpu.VMEM((2, 2 * NA_KH, GRID_W, LANES), F32)],
        compiler_params=_cparams(("arbitrary", "arbitrary")),
        name="na",
    )(p_lat, p_lat, p_lat, p_ctx, p_ctx, band, *[w for w, _ in casts])


def _hg_kernel(q_ref, ff_ref, fb_ref, i_ref, g_ref, cff_ref, cfb_ref, ci_ref, nw_ref, *rest, n_cast):
    o_ref = _cast_riders(rest, n_cast)
    acc_ref = rest[2 * n_cast + 1]
    c = HG_CHUNK
    s_len = q_ref.shape[0]
    t_len = ci_ref.shape[0]
    n = s_len // c
    dn_t = (((1,), (1,)), ((), ()))

    def gates(lf):
        return 1.0 - jnp.exp2(lf), lf

    def masked_sum(mask, lf):
        hi = lf.astype(BF16)
        lo = (lf - hi.astype(F32)).astype(BF16)
        r = jnp.dot(mask, jnp.concatenate([hi, lo], axis=1), preferred_element_type=F32)
        return r[:, :LANES] + r[:, LANES:]

    r_t = lax.broadcasted_iota(jnp.int32, (t_len, t_len), 0)
    c_t = lax.broadcasted_iota(jnp.int32, (t_len, t_len), 1)
    vct = ci_ref[...].astype(F32).T.astype(BF16)
    states = []
    for fref, after in ((cff_ref, c_t > r_t), (cfb_ref, c_t < r_t)):
        kk, lf = gates(fref[...])
        kd = (kk * jnp.exp2(masked_sum(after.astype(BF16), lf))).astype(BF16)
        states.append(jnp.dot(vct, kd, preferred_element_type=F32))

    r_c = lax.broadcasted_iota(jnp.int32, (c, c), 0)
    c_c = lax.broadcasted_iota(jnp.int32, (c, c), 1)
    tri = (c_c <= r_c, c_c >= r_c)
    tri_b = tuple(t.astype(BF16) for t in tri)

    def cum(d, lf):
        return masked_sum(tri_b[d], lf)

    acc_ref[...] = jnp.zeros_like(acc_ref)

    def body(jj, carry):
        chains = []
        for u in range(HG_UNROLL):
            j = jj * HG_UNROLL + u
            chains.append((0, pl.multiple_of(j * c, c)))
            chains.append((1, pl.multiple_of((n - 1 - j) * c, c)))
        pre = []
        for d, start in chains:
            rows = pl.ds(start, c)
            kk, lf = gates((ff_ref, fb_ref)[d][rows, :])
            pre.append((kk, lf, q_ref[rows, :].astype(F32), i_ref[rows, :]))
        bcs = [cum(d, p[1]) for (d, _), p in zip(chains, pre)]
        ops = []
        for (d, _), (kk, lf, qh, v), bc in zip(chains, pre, bcs):
            btot = bc[c - 1:c] if d == 0 else bc[0:1]
            mref = bc[c // 2:c // 2 + 1]
            qm = qh * jnp.exp2(bc - mref)
            km = kk * jnp.exp2(mref - bc)
            qe = (qm * jnp.exp2(mref)).astype(BF16)
            kd = (km * jnp.exp2(btot - mref)).astype(BF16)
            ops.append((qe, qm.astype(BF16), km.astype(BF16), kd, jnp.exp2(btot), v))
        atts = [jnp.where(tri[d], lax.dot_general(qm, km, dn_t, preferred_element_type=F32), 0.0).astype(BF16)
                for (d, _), (qe, qm, km, kd, eb, v) in zip(chains, ops)]
        upds = [lax.dot_general(v, kd, (((0,), (0,)), ((), ())), preferred_element_type=F32)
                for (qe, qm, km, kd, eb, v) in ops]
        intra = [jnp.dot(att, o[5], preferred_element_type=F32) for att, o in zip(atts, ops)]
        st = list(carry)
        for idx, (d, start) in enumerate(chains):
            qe, eb = ops[idx][0], ops[idx][4]
            o = intra[idx] + lax.dot_general(qe, st[d].astype(BF16), dn_t, preferred_element_type=F32)
            st[d] = eb * st[d] + upds[idx]
            acc_ref[pl.ds(start, c), :] += o
        return st[0], st[1]

    lax.fori_loop(0, n // HG_UNROLL, body, (states[0], states[1]))

    rb = HG_READOUT_ROWS
    nw = nw_ref[...]

    def readout(j, carry):
        rows = pl.ds(pl.multiple_of(j * rb, rb), rb)
        o = acc_ref[rows, :]
        o = o * lax.rsqrt(jnp.mean(o * o, axis=-1, keepdims=True) + EPS) * nw
        half = 0.5 * g_ref[rows, :].astype(F32)
        o_ref[rows, :] = (o * (half + half * jnp.tanh(half))).astype(o_ref.dtype)
        return carry

    lax.fori_loop(0, s_len // rb, readout, 0)


def _hgrn(p_lat, lf_lat, p_ctx, lf_ctx, norm_w, casts, *, batch, s_len, t_len):
    nh = HG_HEADS
    per = HG_WIDTH // LANES
    cast_in, cast_out, cast_shapes = _rider_specs(casts, (batch, nh))

    def blk(rows, group):
        return pl.BlockSpec((rows, LANES), lambda b, h: (b, group * per + h))

    return pl.pallas_call(
        functools.partial(_hg_kernel, n_cast=len(casts)),
        grid=(batch, nh),
        in_specs=[blk(s_len, PROJ_HQ), blk(s_len, 0), blk(s_len, 1), blk(s_len, PROJ_HQ + 1),
                  blk(s_len, PROJ_HQ + 2), blk(t_len, 0), blk(t_len, 1), blk(t_len, PROJ_HQ + 1),
                  pl.BlockSpec((1, LANES), lambda b, h: (0, h))] + cast_in,
        out_specs=[pl.BlockSpec((s_len, LANES), lambda b, h: (b, h))] + cast_out,
        out_shape=[jax.ShapeDtypeStruct((batch * s_len, HG_WIDTH), BF16)] + cast_shapes,
        scratch_shapes=[pltpu.VMEM((s_len, LANES), F32)],
        compiler_params=_cparams(("arbitrary", "arbitrary")),
        name="hgrn",
    )(p_lat, lf_lat, lf_lat, p_lat, p_lat, lf_ctx, lf_ctx, p_ctx, norm_w, *[w for w, _ in casts])


def _halo_specs(tm, width, n_rows):
    per = tm // HALO
    last = n_rows // HALO - 1
    return [
        pl.BlockSpec((tm, width), lambda i: (i, 0)),
        pl.BlockSpec((HALO, width), lambda i: (jnp.maximum(i * per - 1, 0), 0)),
        pl.BlockSpec((HALO, width), lambda i: (jnp.minimum((i + 1) * per, last), 0)),
    ]


def _assemble(dst, main, prev, nxt, tm):
    dst[0:HALO, :] = prev[...]
    dst[HALO:HALO + tm, :] = main[...]
    dst[HALO + tm:, :] = nxt[...]


def _store_rows(he_s, h_prev, h_main, h_next, tm, tpb):
    t = pl.program_id(0) % tpb
    keep_prev = jnp.where(t != 0, 1.0, 0.0)
    keep_next = jnp.where(t != tpb - 1, 1.0, 0.0)
    he_s[0:HALO, :] = (h_prev * keep_prev).astype(BF16)
    he_s[HALO:HALO + tm, :] = h_main.astype(BF16)
    he_s[HALO + tm:, :] = (h_next * keep_next).astype(BF16)


def _store_mod_rows(he_s, xm, xp, xn, nw, shift, scale, tm, tpb):
    _store_rows(he_s, _rms_mod(xp[...], nw, shift, scale), _rms_mod(xm[...], nw, shift, scale),
                _rms_mod(xn[...], nw, shift, scale), tm, tpb)


def _conv3(a_ext, cw, cb, tm):
    rows = tm + 2 * HALO
    prev = pltpu.roll(a_ext, 1, 0)[HALO:HALO + tm]
    nxt = pltpu.roll(a_ext, rows - 1, 0)[HALO:HALO + tm]
    cur = a_ext[HALO:HALO + tm]
    return prev * cw[0:1] + cur * cw[1:2] + nxt * cw[2:3] + cb


def _gelu_exact(x):
    return 0.5 * x * (1.0 + lax.erf(x * np.float32(np.sqrt(0.5))))


def _convglu(he_s, act_s, wup_ref, cw_ref, cb_ref, wdown_ref, tm):
    d_ff = wdown_ref.shape[0]
    nchunk = d_ff // FF_CHUNK

    def cols(j, base=0):
        return slice(base + j * FF_CHUNK, base + (j + 1) * FF_CHUNK)

    def up(j):
        a = jnp.dot(he_s[...], wup_ref[:, cols(j)], preferred_element_type=F32)
        val = jnp.dot(he_s[HALO:HALO + tm, :], wup_ref[:, cols(j, d_ff)], preferred_element_type=F32)
        return a, val

    nxt = up(0)
    for j in range(nchunk):
        a, val = nxt
        if j + 1 < nchunk:
            nxt = up(j + 1)
        cv = _conv3(a, cw_ref[:, cols(j)], cb_ref[:, cols(j)], tm)
        act_s[:, cols(j)] = (_gelu_exact(cv) * val).astype(BF16)
    return jnp.dot(act_s[...], wdown_ref[...], preferred_element_type=F32)


def _ffn0_kernel(xm, xp, xn, am, ap, an, gm, gp, gn, gate_m, shift_f, scale_f, gate_f, nw,
                 wo_ref, wup_ref, cw_ref, cb_ref, wdown_ref, o_ref,
                 xe_s, ae_s, ge_s, he_s, act_s, *, tm, tpb):
    b = pl.program_id(0) // tpb
    _assemble(xe_s, xm, xp, xn, tm)
    _assemble(ae_s, am, ap, an, tm)
    _assemble(ge_s, gm, gp, gn, tm)
    y = (jnp.dot(ae_s[...], wo_ref[0], preferred_element_type=F32)
         + jnp.dot(ge_s[...], wo_ref[1], preferred_element_type=F32))
    x1 = xe_s[...] + gate_m[pl.ds(b, 1), :] * y
    xe_s[...] = x1
    h = _rms_mod(x1, nw[...], shift_f[pl.ds(b, 1), :], scale_f[pl.ds(b, 1), :])
    _store_rows(he_s, h[0:HALO], h[HALO:HALO + tm], h[HALO + tm:], tm, tpb)
    y2 = _convglu(he_s, act_s, wup_ref, cw_ref, cb_ref, wdown_ref, tm)
    o_ref[...] = xe_s[HALO:HALO + tm, :] + gate_f[pl.ds(b, 1), :] * y2


def _ffn1_kernel(xm, xp, xn, shift_f, scale_f, gate_f, nw, fnw,
                 wup_ref, cw_ref, cb_ref, wdown_ref, o_ref,
                 he_s, act_s, *, tm, tpb):
    b = pl.program_id(0) // tpb
    _store_mod_rows(he_s, xm, xp, xn, nw[...], shift_f[pl.ds(b, 1), :], scale_f[pl.ds(b, 1), :], tm, tpb)
    y2 = _convglu(he_s, act_s, wup_ref, cw_ref, cb_ref, wdown_ref, tm)
    x2 = xm[...] + gate_f[pl.ds(b, 1), :] * y2
    ms = jnp.mean(x2 * x2, axis=-1, keepdims=True)
    o_ref[...] = x2 * lax.rsqrt(ms + EPS) * fnw[...]


def _mod_spec(k):
    return pl.BlockSpec((SUBLANES, D_MODEL), lambda i: (0, k))


def _const_spec(shape):
    nd = len(shape)
    return pl.BlockSpec(shape, lambda i: (0,) * nd, pipeline_mode=pl.Buffered(1))


def _ffn_weight_specs(d_ff):
    return [
        _const_spec((D_MODEL, 2 * d_ff)),
        _const_spec((3, d_ff)),
        _const_spec((1, d_ff)),
        _const_spec((d_ff, D_MODEL)),
    ]


def _ffn0(x2d, a, g, mod, nw, wo, wup, cw, cb, wdown, *, tm, tpb):
    n_rows = x2d.shape[0]
    d_ff = wdown.shape[0]
    ext = tm + 2 * HALO
    kern = functools.partial(_ffn0_kernel, tm=tm, tpb=tpb)
    return pl.pallas_call(
        kern,
        grid=(n_rows // tm,),
        in_specs=(_halo_specs(tm, D_MODEL, n_rows) + _halo_specs(tm, NA_WIDTH, n_rows)
                  + _halo_specs(tm, HG_WIDTH, n_rows)
                  + [_mod_spec(0), _mod_spec(1), _mod_spec(2), _mod_spec(3),
                     _const_spec((1, D_MODEL)), _const_spec((2, NA_WIDTH, D_MODEL))]
                  + _ffn_weight_specs(d_ff)),
        out_specs=pl.BlockSpec((tm, D_MODEL), lambda i: (i, 0)),
        out_shape=jax.ShapeDtypeStruct((n_rows, D_MODEL), F32),
        scratch_shapes=[pltpu.VMEM((ext, D_MODEL), F32), pltpu.VMEM((ext, NA_WIDTH), BF16),
                        pltpu.VMEM((ext, HG_WIDTH), BF16), pltpu.VMEM((ext, D_MODEL), BF16),
                        pltpu.VMEM((tm, d_ff), BF16)],
        compiler_params=_cparams(("arbitrary",)),
        name="ffn0",
    )(x2d, x2d, x2d, a, a, a, g, g, g, mod, mod, mod, mod, nw, wo, wup, cw, cb, wdown)


def _ffn1(x2d, mod, nw, fnw, wup, cw, cb, wdown, *, tm, tpb):
    n_rows = x2d.shape[0]
    d_ff = wdown.shape[0]
    ext = tm + 2 * HALO
    kern = functools.partial(_ffn1_kernel, tm=tm, tpb=tpb)
    weight_specs = _ffn_weight_specs(d_ff)
    return pl.pallas_call(
        kern,
        grid=(n_rows // tm,),
        in_specs=(_halo_specs(tm, D_MODEL, n_rows)
                  + [_mod_spec(3), _mod_spec(4), _mod_spec(5),
                     _const_spec((1, D_MODEL)), _const_spec((1, D_MODEL))]
                  + weight_specs),
        out_specs=pl.BlockSpec((tm, D_MODEL), lambda i: (i, 0)),
        out_shape=jax.ShapeDtypeStruct((n_rows, D_MODEL), F32),
        scratch_shapes=[pltpu.VMEM((ext, D_MODEL), BF16), pltpu.VMEM((tm, d_ff), BF16)],
        compiler_params=_cparams(("arbitrary",)),
        name="ffn1",
    )(x2d, x2d, x2d, mod, mod, mod, nw, fnw, wup, cw, cb, wdown)


def _mix_kernel(xm, xp, xn, shift_m, scale_m, gate_m, nw, win_ref, cw_ref, cb_ref, wout_ref, *rest,
                tm, tpb, n_cast):
    o_ref = _cast_riders(rest, n_cast)
    he_s, z_s = rest[2 * n_cast + 1:]
    b = pl.program_id(0) // tpb
    width = wout_ref.shape[0]
    nchunk = width // FF_CHUNK
    _store_mod_rows(he_s, xm, xp, xn, nw[...], shift_m[pl.ds(b, 1), :], scale_m[pl.ds(b, 1), :], tm, tpb)

    def cols(j, base=0):
        return slice(base + j * FF_CHUNK, base + (j + 1) * FF_CHUNK)

    def w_in(j, base):
        return win_ref[:, cols(j, base)].astype(BF16)

    def up(j):
        gb = jnp.dot(he_s[HALO:HALO + tm, :], w_in(j, 0), preferred_element_type=F32)
        gc = jnp.dot(he_s[...], w_in(j, width), preferred_element_type=F32)
        u = jnp.dot(he_s[...], w_in(j, 2 * width), preferred_element_type=F32)
        return gb, gc, u

    nxt = up(0)
    for j in range(nchunk):
        gb, gc, u = nxt
        if j + 1 < nchunk:
            nxt = up(j + 1)
        z_s[:, cols(j)] = (gb * _conv3(gc * u, cw_ref[:, cols(j)], cb_ref[:, cols(j)], tm)).astype(BF16)
    y = jnp.dot(z_s[...], wout_ref[...].astype(BF16), preferred_element_type=F32)
    o_ref[...] = xm[...] + gate_m[pl.ds(b, 1), :] * y


def _mix(x2d, mod, nw, win, cw, cb, wout, casts, *, tm, tpb):
    n_rows = x2d.shape[0]
    width = wout.shape[0]
    ext = tm + 2 * HALO
    cast_in, cast_out, cast_shapes = _rider_specs(casts, (n_rows // tm,))
    kern = functools.partial(_mix_kernel, tm=tm, tpb=tpb, n_cast=len(casts))
    return pl.pallas_call(
        kern,
        grid=(n_rows // tm,),
        in_specs=(_halo_specs(tm, D_MODEL, n_rows)
                  + [_mod_spec(0), _mod_spec(1), _mod_spec(2), _const_spec((1, D_MODEL)),
                     _const_spec((D_MODEL, 3 * width)), _const_spec((3, width)),
                     _const_spec((1, width)), _const_spec((width, D_MODEL))] + cast_in),
        out_specs=[pl.BlockSpec((tm, D_MODEL), lambda i: (i, 0))] + cast_out,
        out_shape=[jax.ShapeDtypeStruct((n_rows, D_MODEL), F32)] + cast_shapes,
        scratch_shapes=[pltpu.VMEM((ext, D_MODEL), BF16), pltpu.VMEM((tm, width), BF16)],
        compiler_params=_cparams(("arbitrary",)),
        name="mix",
    )(x2d, x2d, x2d, mod, mod, mod, nw, win, cw, cb, wout, *[w for w, _ in casts])


def kernel(x, c, ctx, c_ctx, ada_w, ada_b, norm_mix_w, norm_ffn_w, ev_w_in, ev_w_out, na_rpb, hg_lb_logits,
           hg_norm_w, od_w_in, od_conv_w, od_conv_b, od_w_out, ffn_w_up, ffn_conv_w, ffn_conv_b, ffn_w_down,
           final_norm_w):
    batch, s_len, d = x.shape
    t_len = ctx.shape[1]
    n_rows = batch * s_len
    tpb = s_len // TM
    x2d = x.reshape(n_rows, d)
    ctx2d = ctx.reshape(batch * t_len, d)

    c_stack = jnp.concatenate([c, c_ctx[None], jnp.zeros((SUBLANES - batch - 1, d), F32)], axis=0)
    ct = c_stack.T
    ada_b3 = ada_b.reshape(ada_b.shape[0], 1, -1)
    mod_in = _ada(ct, ada_w, ada_b3, batch + 1, 0, 2 * d)

    nw_mix = norm_mix_w.reshape(-1, 1, d)
    nw_ffn = norm_ffn_w.reshape(-1, 1, d)
    p_lat, lf_lat, p_ctx, lf_ctx, mod0, mod1 = _proj(
        x2d, ctx2d, nw_mix[0], mod_in, ev_w_in[0], hg_lb_logits, ct, ada_w, ada_b3, batch + 1,
        [(0, 2 * d, 4 * d), (1, 0, 6 * d)], tm=TM, tpb=tpb, ctx_row=batch)
    a_lat, wup0 = _na(p_lat, p_ctx, _na_bias_rows(na_rpb[0]), [(ffn_w_up, 0)],
                      batch=batch, s_len=s_len, t_len=t_len)
    g_lat, wo0, wdown0, wdown1 = _hgrn(p_lat, lf_lat, p_ctx, lf_ctx, hg_norm_w[0].reshape(1, HG_WIDTH),
                                       [(ev_w_out, 0), (ffn_w_down, 0), (ffn_w_down, 1)],
                                       batch=batch, s_len=s_len, t_len=t_len)
    x1 = _ffn0(x2d, a_lat, g_lat, mod0, nw_ffn[0], wo0.reshape(2, NA_WIDTH, d),
               wup0, ffn_conv_w[0], ffn_conv_b[0][None], wdown0, tm=TM, tpb=tpb)

    x2, wup1 = _mix(x1, mod1, nw_mix[1], od_w_in[0], od_conv_w[0], od_conv_b[0][None], od_w_out[0],
                    [(ffn_w_up, 1)], tm=TM, tpb=tpb)
    out = _ffn1(x2, mod1, nw_ffn[1], final_norm_w.reshape(1, d),
                wup1, ffn_conv_w[1], ffn_conv_b[1][None], wdown1, tm=TM, tpb=tpb)
    return out.reshape(batch, s_len, d)
```

```python
import functools

import numpy as np
import jax
import jax.numpy as jnp
from jax import lax
from jax.experimental import pallas as pl
from jax.experimental.pallas import tpu as pltpu

F32 = jnp.float32
BF16 = jnp.bfloat16

D_MODEL = 1024
GRID_W = 64
EPS = 1e-6
NA_HEADS = 8
NA_HEAD_DIM = 64
NA_WIDTH = NA_HEADS * NA_HEAD_DIM
NA_KH = 8
NA_KW = 16
HG_HEADS = 4

V7X_VMEM_BYTES = 64 * 1024 * 1024
VMEM_LIMIT = V7X_VMEM_BYTES - 8 * 1024 * 1024
LANES = 128
SUBLANES = 8
HALO = 16
HG_WIDTH = HG_HEADS * LANES
NEG = -1e30
LOG2E = float(np.log2(np.e))

NA_QROWS = 4
NA_KROWS = NA_QROWS + NA_KH
NA_UNROLL = 8
HG_CHUNK = 64
HG_UNROLL = 32
HG_READOUT_ROWS = 512
FF_CHUNK = 256
ADA_COLS = 1024
PROJ_SRC = (0, 1, 2, 3, 6, 7, 4, 5)
PROJ_HQ = 3
PROJ_FF = 6
TM = 1024


def _cparams(sem):
    return pltpu.CompilerParams(dimension_semantics=sem, vmem_limit_bytes=VMEM_LIMIT)


def _rms_mod(x, nw, shift, scale):
    ms = jnp.mean(x * x, axis=-1, keepdims=True)
    return (x * lax.rsqrt(ms + EPS)) * (nw * (1.0 + scale)) + shift


def _modulation(ct_ref, w_ref, b_ref, o_ref, n_cond):
    s = jax.nn.silu(ct_ref[...])
    o_ref[...] = jnp.zeros(o_ref.shape, F32)
    for r in range(n_cond):
        o_ref[r:r + 1, :] = jnp.sum(s[:, r:r + 1] * w_ref[...], axis=0, keepdims=True) + b_ref[...]


def _ada_kernel(ct_ref, w_ref, b_ref, o_ref, *, n_cond):
    _modulation(ct_ref, w_ref, b_ref, o_ref, n_cond)


def _ada(ct, ada_w, ada_b3, n_cond, layer, ncols):
    d = ada_w.shape[1]
    tn = ADA_COLS
    return pl.pallas_call(
        functools.partial(_ada_kernel, n_cond=n_cond),
        grid=(ncols // tn,),
        in_specs=[
            pl.BlockSpec((d, SUBLANES), lambda j: (0, 0)),
            pl.BlockSpec((None, d, tn), lambda j: (layer, 0, j)),
            pl.BlockSpec((None, 1, tn), lambda j: (layer, 0, j)),
        ],
        out_specs=pl.BlockSpec((SUBLANES, tn), lambda j: (0, j)),
        out_shape=jax.ShapeDtypeStruct((SUBLANES, ncols), F32),
        compiler_params=_cparams(("arbitrary",)),
        name="ada",
    )(ct, ada_w, ada_b3)


def _forget_bounds(lbl_ref):
    ll = lbl_ref[...]
    mx = jnp.maximum(ll[0], ll[1])
    e0 = jnp.exp(ll[0] - mx)
    lb = e0 / (e0 + jnp.exp(ll[1] - mx))
    return 0.5 * (1.0 + lb), 0.5 * (1.0 - lb)


def _proj_kernel(x_ref, cx_ref, nw_ref, shift_ref, scale_ref, w_ref, lbl_ref, ct_ref, *rest,
                 tpb, ctx_row, n_jobs, n_cond):
    jobs_in = rest[:2 * n_jobs]
    o_ref, lf_ref, octx_ref, lfctx_ref = rest[2 * n_jobs:2 * n_jobs + 4]
    jobs_out = rest[2 * n_jobs + 4:3 * n_jobs + 4]
    h_s = rest[-1]
    for job in range(n_jobs):
        _modulation(ct_ref, jobs_in[2 * job], jobs_in[2 * job + 1], jobs_out[job], n_cond)
    tm = x_ref.shape[0]
    row = pl.program_id(0) // tpb
    nw = nw_ref[...]
    h_s[0:tm, :] = _rms_mod(x_ref[...], nw, shift_ref[pl.ds(row, 1), :], scale_ref[pl.ds(row, 1), :]).astype(BF16)
    h_s[tm:, :] = _rms_mod(cx_ref[...], nw, shift_ref[ctx_row:ctx_row + 1, :],
                           scale_ref[ctx_row:ctx_row + 1, :]).astype(BF16)
    f_mid, f_amp = _forget_bounds(lbl_ref)
    ncol = HG_WIDTH
    nblk = w_ref.shape[1] // ncol
    for blk in sorted(range(nblk), key=lambda k: (k < PROJ_FF and k != PROJ_HQ, k)):
        cols = slice(blk * ncol, (blk + 1) * ncol)
        src = PROJ_SRC[blk]
        y = jnp.dot(h_s[...], w_ref[:, src * ncol:(src + 1) * ncol].astype(BF16), preferred_element_type=F32)
        if blk == 0:
            y = y * np.float32(LOG2E * NA_HEAD_DIM ** -0.5)
        if blk == PROJ_HQ:
            half = 0.5 * y
            y = half + half * jnp.tanh(half)
        if blk < PROJ_FF:
            y = y.astype(o_ref.dtype)
            o_ref[:, cols] = y[:tm]
            octx_ref[:, cols] = y[tm:]
        else:
            d = blk - PROJ_FF
            fg = f_mid[d:d + 1] + f_amp[d:d + 1] * jnp.tanh(0.5 * y)
            lf = jnp.log(fg) * np.float32(1.0 / np.log(2.0))
            lf_ref[:, d * ncol:(d + 1) * ncol] = lf[:tm]
            lfctx_ref[:, d * ncol:(d + 1) * ncol] = lf[tm:]


def _proj(x2d, ctx2d, nw, mod, w, lb_logits, ct, ada_w, ada_b3, n_cond, ranges, *, tm, tpb, ctx_row):
    n_rows, d = x2d.shape
    n = w.shape[1]
    n_bf = PROJ_FF * HG_WIDTH
    steps = n_rows // tm
    cb = ctx2d.shape[0] // steps
    assert cb * steps == ctx2d.shape[0] and cb % HALO == 0
    job_in, job_out, job_shapes, job_args = [], [], [], []
    for layer, col0, ncols in ranges:
        bc = ncols // steps
        assert bc * steps == ncols and bc % LANES == 0 and col0 % bc == 0
        first = col0 // bc
        job_in.append(pl.BlockSpec((None, d, bc), lambda i, layer=layer, first=first: (layer, 0, first + i)))
        job_in.append(pl.BlockSpec((None, 1, bc), lambda i, layer=layer, first=first: (layer, 0, first + i)))
        job_out.append(pl.BlockSpec((SUBLANES, bc), lambda i: (0, i)))
        job_shapes.append(jax.ShapeDtypeStruct((SUBLANES, ncols), F32))
        job_args += [ada_w, ada_b3]
    kern = functools.partial(_proj_kernel, tpb=tpb, ctx_row=ctx_row, n_jobs=len(ranges), n_cond=n_cond)
    return pl.pallas_call(
        kern,
        grid=(steps,),
        in_specs=[
            pl.BlockSpec((tm, d), lambda i: (i, 0)),
            pl.BlockSpec((cb, d), lambda i: (i, 0)),
            pl.BlockSpec((1, d), lambda i: (0, 0)),
            pl.BlockSpec((SUBLANES, d), lambda i: (0, 0)),
            pl.BlockSpec((SUBLANES, d), lambda i: (0, 1)),
            pl.BlockSpec((d, n), lambda i: (0, 0), pipeline_mode=pl.Buffered(1)),
            pl.BlockSpec(lb_logits.shape, lambda i: (0, 0, 0)),
            pl.BlockSpec((d, SUBLANES), lambda i: (0, 0)),
        ] + job_in,
        out_specs=[pl.BlockSpec((tm, n_bf), lambda i: (i, 0)),
                   pl.BlockSpec((tm, n - n_bf), lambda i: (i, 0)),
                   pl.BlockSpec((cb, n_bf), lambda i: (i, 0)),
                   pl.BlockSpec((cb, n - n_bf), lambda i: (i, 0))] + job_out,
        out_shape=[jax.ShapeDtypeStruct((n_rows, n_bf), BF16),
                   jax.ShapeDtypeStruct((n_rows, n - n_bf), F32),
                   jax.ShapeDtypeStruct((ctx2d.shape[0], n_bf), BF16),
                   jax.ShapeDtypeStruct((ctx2d.shape[0], n - n_bf), F32)] + job_shapes,
        scratch_shapes=[pltpu.VMEM((tm + cb, d), BF16)],
        compiler_params=_cparams(("arbitrary",)),
        name="proj",
    )(x2d, ctx2d, nw, mod, mod, w, lb_logits, ct, *job_args)


def _na_bias_rows(rpb):
    h, nr, nc = rpb.shape
    return jnp.pad(rpb.astype(F32) * LOG2E, ((0, 0), (0, 2 * NA_KH - nr), (0, LANES - nc)))


def _na_build_bands(rows_ref, band_ref):
    qc = lax.broadcasted_iota(jnp.int32, (GRID_W, LANES), 0)
    lane = lax.broadcasted_iota(jnp.int32, (GRID_W, LANES), 1)
    kc = jnp.where(lane < GRID_W, lane, lane - GRID_W)
    start = jnp.clip(qc - NA_KW // 2, 0, GRID_W - NA_KW)
    valid = (kc >= start) & (kc < start + NA_KW)
    masked = jnp.full((GRID_W, LANES), NEG, F32)
    for hh in range(band_ref.shape[0]):
        for a in range(2 * NA_KH - 1):
            x = jnp.broadcast_to(rows_ref[hh, a:a + 1, :], (GRID_W, LANES))
            lo = pltpu.roll(x, LANES - (NA_KW - 1), 1, stride=1, stride_axis=0)
            hi = pltpu.roll(x, LANES - (NA_KW - 1) - GRID_W, 1, stride=1, stride_axis=0)
            band_ref[hh, a] = jnp.where(valid, jnp.where(lane < GRID_W, lo, hi), masked)
        band_ref[hh, 2 * NA_KH - 1] = masked


def _na_block_index(rows):
    masked = 2 * NA_KH - 1

    def block_idx(rb):
        kr0 = int(np.clip(NA_QROWS * rb - NA_KH // 2, 0, rows - NA_KROWS))
        idx = np.empty((NA_QROWS, NA_KROWS), np.int32)
        for qi in range(NA_QROWS):
            qr = NA_QROWS * rb + qi
            r0 = int(np.clip(qr - NA_KH // 2, 0, rows - NA_KH))
            for kj in range(NA_KROWS):
                kr = kr0 + kj
                idx[qi, kj] = kr - qr + (NA_KH - 1) if r0 <= kr < r0 + NA_KH else masked
        return idx

    nblk = rows // NA_QROWS
    interior = block_idx(1)
    assert all(np.array_equal(block_idx(rb), interior) for rb in range(1, nblk - 1))
    return np.stack([block_idx(0), interior, block_idx(nblk - 1)])


def _na_kernel(q_ref, k_ref, v_ref, kc_ref, vc_ref, rows_ref, *rest, n_cast):
    o_ref = _cast_riders(rest, n_cast)
    bias_ref, band_ref = rest[2 * n_cast + 1:]
    s_len = q_ref.shape[0]
    qb = NA_QROWS * GRID_W
    kb = NA_KROWS * GRID_W
    nblk = s_len // qb
    rows = s_len // GRID_W
    _na_build_bands(rows_ref, band_ref)

    idx = _na_block_index(rows)
    left = lax.broadcasted_iota(jnp.int32, (GRID_W, LANES), 1) < GRID_W
    for hh in range(2):
        for t in range(3):
            for qi in range(NA_QROWS):
                for j in range(NA_KROWS // 2):
                    a, b = int(idx[t, qi, 2 * j]), int(idx[t, qi, 2 * j + 1])
                    tile = band_ref[hh, a] if a == b else jnp.where(left, band_ref[hh, a], band_ref[hh, b])
                    bias_ref[hh, t, qi * GRID_W:(qi + 1) * GRID_W, j * LANES:(j + 1) * LANES] = tile

    lane = lax.broadcasted_iota(jnp.int32, (qb, LANES), 1)
    in_head = [(lane >= NA_HEAD_DIM * hh) & (lane < NA_HEAD_DIM * (hh + 1)) for hh in range(2)]
    dn_t = (((1,), (1,)), ((), ()))

    def head_values(v, hh):
        return jnp.concatenate([jnp.where(in_head[hh], v[r:r + qb], jnp.ones_like(v[r:r + qb]))
                                for r in range(0, v.shape[0], qb)], axis=0)

    kc = kc_ref[...]
    vc = [head_values(vc_ref[...], hh) for hh in range(2)]

    def body(it, carry):
        blocks = []
        for u in range(NA_UNROLL):
            rb = it * NA_UNROLL + u
            kr0 = jnp.clip(NA_QROWS * rb - NA_KH // 2, 0, rows - NA_KROWS)
            ks = pl.multiple_of(kr0 * GRID_W, qb)
            qs = pl.multiple_of(rb * qb, qb)
            btype = jnp.where(rb == 0, 0, jnp.where(rb == nblk - 1, 2, 1))
            blocks.append((qs, k_ref[pl.ds(ks, kb), :], v_ref[pl.ds(ks, kb), :], q_ref[pl.ds(qs, qb), :], btype))
        logits = []
        for qs, kw, vw, q, btype in blocks:
            q2 = jnp.concatenate([jnp.where(in_head[hh], q, jnp.zeros_like(q)) for hh in range(2)], axis=0)
            s_loc = lax.dot_general(q2, kw, dn_t, preferred_element_type=F32)
            s_ctx = lax.dot_general(q2, kc, dn_t, preferred_element_type=F32)
            for hh in range(2):
                logits.append((s_loc[hh * qb:(hh + 1) * qb] + bias_ref[hh, btype], s_ctx[hh * qb:(hh + 1) * qb]))
        probs = []
        for s_loc, s_ctx in logits:
            m = jnp.maximum(jnp.max(s_loc, axis=-1, keepdims=True), jnp.max(s_ctx, axis=-1, keepdims=True))
            probs.append((jnp.exp2(s_loc - m).astype(BF16), jnp.exp2(s_ctx - m).astype(BF16)))
        outs = []
        for i, (p_loc, p_ctx) in enumerate(probs):
            hh = i % 2
            o = (jnp.dot(p_loc, head_values(blocks[i // 2][2], hh), preferred_element_type=F32)
                 + jnp.dot(p_ctx, vc[hh], preferred_element_type=F32))
            outs.append(o / pltpu.roll(o, NA_HEAD_DIM, 1))
        for u, blk in enumerate(blocks):
            o_ref[pl.ds(blk[0], qb), :] = jnp.where(in_head[0], outs[2 * u], outs[2 * u + 1]).astype(o_ref.dtype)
        return carry

    lax.fori_loop(0, nblk // NA_UNROLL, body, 0)


def _cast_riders(rest, n_cast):
    for w_ref, w16_ref in zip(rest[:n_cast], rest[n_cast + 1:2 * n_cast + 1]):
        w16_ref[...] = w_ref[...].astype(BF16)
    return rest[n_cast]


def _rider_specs(casts, grid):
    steps = int(np.prod(grid))

    def step(*idx):
        flat = idx[0]
        for extent, i in zip(grid[1:], idx[1:]):
            flat = flat * extent + i
        return flat

    cast_in, cast_out, cast_shapes = [], [], []
    for w, layer in casts:
        _, rows, cols = w.shape
        blk_rows = rows // steps
        assert blk_rows * steps == rows and blk_rows % HALO == 0
        cast_in.append(pl.BlockSpec((None, blk_rows, cols), lambda *idx, layer=layer: (layer, step(*idx), 0)))
        cast_out.append(pl.BlockSpec((blk_rows, cols), lambda *idx: (step(*idx), 0)))
        cast_shapes.append(jax.ShapeDtypeStruct((rows, cols), BF16))
    return cast_in, cast_out, cast_shapes


def _na(p_lat, p_ctx, band, casts, *, batch, s_len, t_len):
    hp = NA_HEADS // 2
    kblk = NA_WIDTH // LANES
    qb = NA_QROWS * GRID_W
    kbk = NA_KROWS * GRID_W
    cast_in, cast_out, cast_shapes = _rider_specs(casts, (batch, hp))
    return pl.pallas_call(
        functools.partial(_na_kernel, n_cast=len(casts)),
        grid=(batch, hp),
        in_specs=[
            pl.BlockSpec((s_len, LANES), lambda b, h: (b, h)),
            pl.BlockSpec((s_len, LANES), lambda b, h: (b, kblk + h)),
            pl.BlockSpec((s_len, LANES), lambda b, h: (b, 2 * kblk + h)),
            pl.BlockSpec((t_len, LANES), lambda b, h: (b, kblk + h)),
            pl.BlockSpec((t_len, LANES), lambda b, h: (b, 2 * kblk + h)),
            pl.BlockSpec((2, 2 * NA_KH, LANES), lambda b, h: (h, 0, 0)),
        ] + cast_in,
        out_specs=[pl.BlockSpec((s_len, LANES), lambda b, h: (b, h))] + cast_out,
        out_shape=[jax.ShapeDtypeStruct((batch * s_len, NA_WIDTH), BF16)] + cast_shapes,
        scratch_shapes=[pltpu.VMEM((2, 3, qb, kbk), F32), pltpu.VMEM((2, 2 * NA_KH, GRID_W, LANES), F32)],
        compiler_params=_cparams(("arbitrary", "arbitrary")),
        name="na",
    )(p_lat, p_lat, p_lat, p_ctx, p_ctx, band, *[w for w, _ in casts])


def _hg_kernel(q_ref, ff_ref, fb_ref, i_ref, g_ref, cff_ref, cfb_ref, ci_ref, nw_ref, *rest, n_cast):
    o_ref = _cast_riders(rest, n_cast)
    acc_ref = rest[2 * n_cast + 1]
    c = HG_CHUNK
    s_len = q_ref.shape[0]
    t_len = ci_ref.shape[0]
    n = s_len // c
    dn_t = (((1,), (1,)), ((), ()))

    def gates(lf):
        return 1.0 - jnp.exp2(lf), lf

    def masked_sum(mask, lf):
        hi = lf.astype(BF16)
        lo = (lf - hi.astype(F32)).astype(BF16)
        r = jnp.dot(mask, jnp.concatenate([hi, lo], axis=1), preferred_element_type=F32)
        return r[:, :LANES] + r[:, LANES:]

    r_t = lax.broadcasted_iota(jnp.int32, (t_len, t_len), 0)
    c_t = lax.broadcasted_iota(jnp.int32, (t_len, t_len), 1)
    vct = ci_ref[...].astype(F32).T.astype(BF16)
    states = []
    for fref, after in ((cff_ref, c_t > r_t), (cfb_ref, c_t < r_t)):
        kk, lf = gates(fref[...])
        kd = (kk * jnp.exp2(masked_sum(after.astype(BF16), lf))).astype(BF16)
        states.append(jnp.dot(vct, kd, preferred_element_type=F32))

    r_c = lax.broadcasted_iota(jnp.int32, (c, c), 0)
    c_c = lax.broadcasted_iota(jnp.int32, (c, c), 1)
    tri = (c_c <= r_c, c_c >= r_c)
    tri_b = tuple(t.astype(BF16) for t in tri)

    def cum(d, lf):
        return masked_sum(tri_b[d], lf)

    def body(jj, carry, first_touch):
        chains = []
        for u in range(HG_UNROLL):
            j = jj * HG_UNROLL + u
            chains.append((0, pl.multiple_of(j * c, c)))
            chains.append((1, pl.multiple_of((n - 1 - j) * c, c)))
        pre = []
        for d, start in chains:
            rows = pl.ds(start, c)
            kk, lf = gates((ff_ref, fb_ref)[d][rows, :])
            pre.append((kk, lf, q_ref[rows, :].astype(F32), i_ref[rows, :]))
        bcs = [cum(d, p[1]) for (d, _), p in zip(chains, pre)]
        ops = []
        for (d, _), (kk, lf, qh, v), bc in zip(chains, pre, bcs):
            btot = bc[c - 1:c] if d == 0 else bc[0:1]
            mref = bc[c // 2:c // 2 + 1]
            qm = qh * jnp.exp2(bc - mref)
            km = kk * jnp.exp2(mref - bc)
            qe = (qm * jnp.exp2(mref)).astype(BF16)
            kd = (km * jnp.exp2(btot - mref)).astype(BF16)
            ops.append((qe, qm.astype(BF16), km.astype(BF16), kd, jnp.exp2(btot), v))
        atts = [jnp.where(tri[d], lax.dot_general(qm, km, dn_t, preferred_element_type=F32), 0.0).astype(BF16)
                for (d, _), (qe, qm, km, kd, eb, v) in zip(chains, ops)]
        upds = [lax.dot_general(v, kd, (((0,), (0,)), ((), ())), preferred_element_type=F32)
                for (qe, qm, km, kd, eb, v) in ops]
        intra = [jnp.dot(att, o[5], preferred_element_type=F32) for att, o in zip(atts, ops)]
        st = list(carry)
        for idx, (d, start) in enumerate(chains):
            qe, eb = ops[idx][0], ops[idx][4]
            o = intra[idx] + lax.dot_general(qe, st[d].astype(BF16), dn_t, preferred_element_type=F32)
            st[d] = eb * st[d] + upds[idx]
            if first_touch:
                acc_ref[pl.ds(start, c), :] = o
            else:
                acc_ref[pl.ds(start, c), :] += o
        return st[0], st[1]

    n_iter = n // HG_UNROLL
    assert n_iter % 2 == 0
    mid = lax.fori_loop(0, n_iter // 2, functools.partial(body, first_touch=True), (states[0], states[1]))
    lax.fori_loop(n_iter // 2, n_iter, functools.partial(body, first_touch=False), mid)

    rb = HG_READOUT_ROWS
    nw = nw_ref[...]

    def readout(j, carry):
        rows = pl.ds(pl.multiple_of(j * rb, rb), rb)
        o = acc_ref[rows, :]
        o = o * lax.rsqrt(jnp.mean(o * o, axis=-1, keepdims=True) + EPS) * nw
        half = 0.5 * g_ref[rows, :].astype(F32)
        o_ref[rows, :] = (o * (half + half * jnp.tanh(half))).astype(o_ref.dtype)
        return carry

    lax.fori_loop(0, s_len // rb, readout, 0)


def _hgrn(p_lat, lf_lat, p_ctx, lf_ctx, norm_w, casts, *, batch, s_len, t_len):
    nh = HG_HEADS
    per = HG_WIDTH // LANES
    cast_in, cast_out, cast_shapes = _rider_specs(casts, (batch, nh))

    def blk(rows, group):
        return pl.BlockSpec((rows, LANES), lambda b, h: (b, group * per + h))

    return pl.pallas_call(
        functools.partial(_hg_kernel, n_cast=len(casts)),
        grid=(batch, nh),
        in_specs=[blk(s_len, PROJ_HQ), blk(s_len, 0), blk(s_len, 1), blk(s_len, PROJ_HQ + 1),
                  blk(s_len, PROJ_HQ + 2), blk(t_len, 0), blk(t_len, 1), blk(t_len, PROJ_HQ + 1),
                  pl.BlockSpec((1, LANES), lambda b, h: (0, h))] + cast_in,
        out_specs=[pl.BlockSpec((s_len, LANES), lambda b, h: (b, h))] + cast_out,
        out_shape=[jax.ShapeDtypeStruct((batch * s_len, HG_WIDTH), BF16)] + cast_shapes,
        scratch_shapes=[pltpu.VMEM((s_len, LANES), F32)],
        compiler_params=_cparams(("arbitrary", "arbitrary")),
        name="hgrn",
    )(p_lat, lf_lat, lf_lat, p_lat, p_lat, lf_ctx, lf_ctx, p_ctx, norm_w, *[w for w, _ in casts])


def _halo_specs(tm, width, n_rows):
    per = tm // HALO
    last = n_rows // HALO - 1
    return [
        pl.BlockSpec((tm, width), lambda i: (i, 0)),
        pl.BlockSpec((HALO, width), lambda i: (jnp.maximum(i * per - 1, 0), 0)),
        pl.BlockSpec((HALO, width), lambda i: (jnp.minimum((i + 1) * per, last), 0)),
    ]


def _assemble(dst, main, prev, nxt, tm):
    dst[0:HALO, :] = prev[...]
    dst[HALO:HALO + tm, :] = main[...]
    dst[HALO + tm:, :] = nxt[...]


def _store_rows(he_s, h_prev, h_main, h_next, tm, tpb):
    t = pl.program_id(0) % tpb
    keep_prev = jnp.where(t != 0, 1.0, 0.0)
    keep_next = jnp.where(t != tpb - 1, 1.0, 0.0)
    he_s[0:HALO, :] = (h_prev * keep_prev).astype(BF16)
    he_s[HALO:HALO + tm, :] = h_main.astype(BF16)
    he_s[HALO + tm:, :] = (h_next * keep_next).astype(BF16)


def _store_mod_rows(he_s, xm, xp, xn, nw, shift, scale, tm, tpb):
    _store_rows(he_s, _rms_mod(xp[...], nw, shift, scale), _rms_mod(xm[...], nw, shift, scale),
                _rms_mod(xn[...], nw, shift, scale), tm, tpb)


def _conv3(a_ext, cw, cb, tm):
    rows = tm + 2 * HALO
    prev = pltpu.roll(a_ext, 1, 0)[HALO:HALO + tm]
    nxt = pltpu.roll(a_ext, rows - 1, 0)[HALO:HALO + tm]
    cur = a_ext[HALO:HALO + tm]
    return prev * cw[0:1] + cur * cw[1:2] + nxt * cw[2:3] + cb


def _gelu_exact(x):
    return 0.5 * x * (1.0 + lax.erf(x * np.float32(np.sqrt(0.5))))


def _convglu(he_s, act_s, wup_ref, cw_ref, cb_ref, wdown_ref, tm):
    d_ff = wdown_ref.shape[0]
    nchunk = d_ff // FF_CHUNK

    def cols(j, base=0):
        return slice(base + j * FF_CHUNK, base + (j + 1) * FF_CHUNK)

    def up(j):
        a = jnp.dot(he_s[...], wup_ref[:, cols(j)], preferred_element_type=F32)
        val = jnp.dot(he_s[HALO:HALO + tm, :], wup_ref[:, cols(j, d_ff)], preferred_element_type=F32)
        return a, val

    nxt = up(0)
    for j in range(nchunk):
        a, val = nxt
        if j + 1 < nchunk:
            nxt = up(j + 1)
        cv = _conv3(a, cw_ref[:, cols(j)], cb_ref[:, cols(j)], tm)
        act_s[:, cols(j)] = (_gelu_exact(cv) * val).astype(BF16)
    return jnp.dot(act_s[...], wdown_ref[...], preferred_element_type=F32)


def _ffn0_kernel(xm, xp, xn, am, ap, an, gm, gp, gn, gate_m, shift_f, scale_f, gate_f, nw,
                 wo_ref, wup_ref, cw_ref, cb_ref, wdown_ref, o_ref,
                 xe_s, ae_s, ge_s, he_s, act_s, *, tm, tpb):
    b = pl.program_id(0) // tpb
    _assemble(xe_s, xm, xp, xn, tm)
    _assemble(ae_s, am, ap, an, tm)
    _assemble(ge_s, gm, gp, gn, tm)
    y = (jnp.dot(ae_s[...], wo_ref[0], preferred_element_type=F32)
         + jnp.dot(ge_s[...], wo_ref[1], preferred_element_type=F32))
    x1 = xe_s[...] + gate_m[pl.ds(b, 1), :] * y
    xe_s[...] = x1
    h = _rms_mod(x1, nw[...], shift_f[pl.ds(b, 1), :], scale_f[pl.ds(b, 1), :])
    _store_rows(he_s, h[0:HALO], h[HALO:HALO + tm], h[HALO + tm:], tm, tpb)
    y2 = _convglu(he_s, act_s, wup_ref, cw_ref, cb_ref, wdown_ref, tm)
    o_ref[...] = xe_s[HALO:HALO + tm, :] + gate_f[pl.ds(b, 1), :] * y2


def _ffn1_kernel(xm, xp, xn, shift_f, scale_f, gate_f, nw, fnw,
                 wup_ref, cw_ref, cb_ref, wdown_ref, o_ref,
                 he_s, act_s, *, tm, tpb):
    b = pl.program_id(0) // tpb
    _store_mod_rows(he_s, xm, xp, xn, nw[...], shift_f[pl.ds(b, 1), :], scale_f[pl.ds(b, 1), :], tm, tpb)
    y2 = _convglu(he_s, act_s, wup_ref, cw_ref, cb_ref, wdown_ref, tm)
    x2 = xm[...] + gate_f[pl.ds(b, 1), :] * y2
    ms = jnp.mean(x2 * x2, axis=-1, keepdims=True)
    o_ref[...] = x2 * lax.rsqrt(ms + EPS) * fnw[...]


def _mod_spec(k):
    return pl.BlockSpec((SUBLANES, D_MODEL), lambda i: (0, k))


def _const_spec(shape):
    nd = len(shape)
    return pl.BlockSpec(shape, lambda i: (0,) * nd, pipeline_mode=pl.Buffered(1))


def _ffn_weight_specs(d_ff):
    return [
        _const_spec((D_MODEL, 2 * d_ff)),
        _const_spec((3, d_ff)),
        _const_spec((1, d_ff)),
        _const_spec((d_ff, D_MODEL)),
    ]


def _ffn0(x2d, a, g, mod, nw, wo, wup, cw, cb, wdown, *, tm, tpb):
    n_rows = x2d.shape[0]
    d_ff = wdown.shape[0]
    ext = tm + 2 * HALO
    kern = functools.partial(_ffn0_kernel, tm=tm, tpb=tpb)
    return pl.pallas_call(
        kern,
        grid=(n_rows // tm,),
        in_specs=(_halo_specs(tm, D_MODEL, n_rows) + _halo_specs(tm, NA_WIDTH, n_rows)
                  + _halo_specs(tm, HG_WIDTH, n_rows)
                  + [_mod_spec(0), _mod_spec(1), _mod_spec(2), _mod_spec(3),
                     _const_spec((1, D_MODEL)), _const_spec((2, NA_WIDTH, D_MODEL))]
                  + _ffn_weight_specs(d_ff)),
        out_specs=pl.BlockSpec((tm, D_MODEL), lambda i: (i, 0)),
        out_shape=jax.ShapeDtypeStruct((n_rows, D_MODEL), F32),
        scratch_shapes=[pltpu.VMEM((ext, D_MODEL), F32), pltpu.VMEM((ext, NA_WIDTH), BF16),
                        pltpu.VMEM((ext, HG_WIDTH), BF16), pltpu.VMEM((ext, D_MODEL), BF16),
                        pltpu.VMEM((tm, d_ff), BF16)],
        compiler_params=_cparams(("arbitrary",)),
        name="ffn0",
    )(x2d, x2d, x2d, a, a, a, g, g, g, mod, mod, mod, mod, nw, wo, wup, cw, cb, wdown)


def _ffn1(x2d, mod, nw, fnw, wup, cw, cb, wdown, *, tm, tpb):
    n_rows = x2d.shape[0]
    d_ff = wdown.shape[0]
    ext = tm + 2 * HALO
    kern = functools.partial(_ffn1_kernel, tm=tm, tpb=tpb)
    weight_specs = _ffn_weight_specs(d_ff)
    return pl.pallas_call(
        kern,
        grid=(n_rows // tm,),
        in_specs=(_halo_specs(tm, D_MODEL, n_rows)
                  + [_mod_spec(3), _mod_spec(4), _mod_spec(5),
                     _const_spec((1, D_MODEL)), _const_spec((1, D_MODEL))]
                  + weight_specs),
        out_specs=pl.BlockSpec((tm, D_MODEL), lambda i: (i, 0)),
        out_shape=jax.ShapeDtypeStruct((n_rows, D_MODEL), F32),
        scratch_shapes=[pltpu.VMEM((ext, D_MODEL), BF16), pltpu.VMEM((tm, d_ff), BF16)],
        compiler_params=_cparams(("arbitrary",)),
        name="ffn1",
    )(x2d, x2d, x2d, mod, mod, mod, nw, fnw, wup, cw, cb, wdown)


def _mix_kernel(xm, xp, xn, shift_m, scale_m, gate_m, nw, win_ref, cw_ref, cb_ref, wout_ref, *rest,
                tm, tpb, n_cast):
    o_ref = _cast_riders(rest, n_cast)
    he_s, z_s = rest[2 * n_cast + 1:]
    b = pl.program_id(0) // tpb
    width = wout_ref.shape[0]
    nchunk = width // FF_CHUNK
    _store_mod_rows(he_s, xm, xp, xn, nw[...], shift_m[pl.ds(b, 1), :], scale_m[pl.ds(b, 1), :], tm, tpb)

    def cols(j, base=0):
        return slice(base + j * FF_CHUNK, base + (j + 1) * FF_CHUNK)

    def w_in(j, base):
        return win_ref[:, cols(j, base)].astype(BF16)

    def up(j):
        gb = jnp.dot(he_s[HALO:HALO + tm, :], w_in(j, 0), preferred_element_type=F32)
        gc = jnp.dot(he_s[...], w_in(j, width), preferred_element_type=F32)
        u = jnp.dot(he_s[...], w_in(j, 2 * width), preferred_element_type=F32)
        return gb, gc, u

    nxt = up(0)
    for j in range(nchunk):
        gb, gc, u = nxt
        if j + 1 < nchunk:
            nxt = up(j + 1)
        z_s[:, cols(j)] = (gb * _conv3(gc * u, cw_ref[:, cols(j)], cb_ref[:, cols(j)], tm)).astype(BF16)
    y = jnp.dot(z_s[...], wout_ref[...].astype(BF16), preferred_element_type=F32)
    o_ref[...] = xm[...] + gate_m[pl.ds(b, 1), :] * y


def _mix(x2d, mod, nw, win, cw, cb, wout, casts, *, tm, tpb):
    n_rows = x2d.shape[0]
    width = wout.shape[0]
    ext = tm + 2 * HALO
    cast_in, cast_out, cast_shapes = _rider_specs(casts, (n_rows // tm,))
    kern = functools.partial(_mix_kernel, tm=tm, tpb=tpb, n_cast=len(casts))
    return pl.pallas_call(
        kern,
        grid=(n_rows // tm,),
        in_specs=(_halo_specs(tm, D_MODEL, n_rows)
                  + [_mod_spec(0), _mod_spec(1), _mod_spec(2), _const_spec((1, D_MODEL)),
                     _const_spec((D_MODEL, 3 * width)), _const_spec((3, width)),
                     _const_spec((1, width)), _const_spec((width, D_MODEL))] + cast_in),
        out_specs=[pl.BlockSpec((tm, D_MODEL), lambda i: (i, 0))] + cast_out,
        out_shape=[jax.ShapeDtypeStruct((n_rows, D_MODEL), F32)] + cast_shapes,
        scratch_shapes=[pltpu.VMEM((ext, D_MODEL), BF16), pltpu.VMEM((tm, width), BF16)],
        compiler_params=_cparams(("arbitrary",)),
        name="mix",
    )(x2d, x2d, x2d, mod, mod, mod, nw, win, cw, cb, wout, *[w for w, _ in casts])


def kernel(x, c, ctx, c_ctx, ada_w, ada_b, norm_mix_w, norm_ffn_w, ev_w_in, ev_w_out, na_rpb, hg_lb_logits,
           hg_norm_w, od_w_in, od_conv_w, od_conv_b, od_w_out, ffn_w_up, ffn_conv_w, ffn_conv_b, ffn_w_down,
           final_norm_w):
    batch, s_len, d = x.shape
    t_len = ctx.shape[1]
    n_rows = batch * s_len
    tpb = s_len // TM
    x2d = x.reshape(n_rows, d)
    ctx2d = ctx.reshape(batch * t_len, d)

    c_stack = jnp.concatenate([c, c_ctx[None], jnp.zeros((SUBLANES - batch - 1, d), F32)], axis=0)
    ct = c_stack.T
    ada_b3 = ada_b.reshape(ada_b.shape[0], 1, -1)
    mod_in = _ada(ct, ada_w, ada_b3, batch + 1, 0, 2 * d)

    nw_mix = norm_mix_w.reshape(-1, 1, d)
    nw_ffn = norm_ffn_w.reshape(-1, 1, d)
    p_lat, lf_lat, p_ctx, lf_ctx, mod0, mod1 = _proj(
        x2d, ctx2d, nw_mix[0], mod_in, ev_w_in[0], hg_lb_logits, ct, ada_w, ada_b3, batch + 1,
        [(0, 2 * d, 4 * d), (1, 0, 6 * d)], tm=TM, tpb=tpb, ctx_row=batch)
    a_lat, wup0, w_in1, w_out1 = _na(p_lat, p_ctx, _na_bias_rows(na_rpb[0]),
                                     [(ffn_w_up, 0), (od_w_in, 0), (od_w_out, 0)],
                                     batch=batch, s_len=s_len, t_len=t_len)
    g_lat, wo0, wdown0, wdown1 = _hgrn(p_lat, lf_lat, p_ctx, lf_ctx, hg_norm_w[0].reshape(1, HG_WIDTH),
                                       [(ev_w_out, 0), (ffn_w_down, 0), (ffn_w_down, 1)],
                                       batch=batch, s_len=s_len, t_len=t_len)
    x1 = _ffn0(x2d, a_lat, g_lat, mod0, nw_ffn[0], wo0.reshape(2, NA_WIDTH, d),
               wup0, ffn_conv_w[0], ffn_conv_b[0][None], wdown0, tm=TM, tpb=tpb)

    x2, wup1 = _mix(x1, mod1, nw_mix[1], w_in1, od_conv_w[0], od_conv_b[0][None], w_out1,
                    [(ffn_w_up, 1)], tm=TM, tpb=tpb)
    out = _ffn1(x2, mod1, nw_ffn[1], final_norm_w.reshape(1, d),
                wup1, ffn_conv_w[1], ffn_conv_b[1][None], wdown1, tm=TM, tpb=tpb)
    return out.reshape(batch, s_len, d)
```
